```python
import math
import jax, jax.numpy as jnp
from jax import lax
import numpy as np


D_MODEL = 2048
BATCH = 32
SEQ = 256
DEPTH = 2
DEC_BATCH = 2
DEC_SEQ = 4096
PAST_LEN = 512

GRID_W = 64
N_MIXERS = 2
N_HY = (DEPTH + 1) // 2
N_ML = DEPTH // 2
EXPAND = 2
E = EXPAND * D_MODEL
CONV_W = 3
N_BANDS = 16
EMB = 1 + 2 * N_BANDS
FILTER_ORDER = 64
DECAY_MIN = 3.07
DECAY_MAX = 15.35
N_HEADS = 8
DH = E // N_HEADS
DK = DH // 2
DV = DH
GATE_IN = 2 * N_HEADS * DK + N_HEADS * DV
CHUNK = 64
EPS = 1e-6

kernel_name = 'hybrid_hyena_mlstm_flow_step'


def rmsnorm(x, g):
    xf = x.astype(jnp.float32)
    r = lax.rsqrt(jnp.mean(xf * xf, axis=-1, keepdims=True) + EPS)
    return (xf * r * g.astype(jnp.float32)).astype(x.dtype)


def dwconv(x, w, b):
    C = x.shape[-1]
    y = lax.conv_general_dilated(x, w[:, None, :].astype(x.dtype), window_strides=(1,),
                                 padding=((CONV_W // 2, CONV_W // 2),),
                                 dimension_numbers=('NWC', 'WIO', 'NWC'), feature_group_count=C)
    return y + b.astype(x.dtype)


def hyena_filters(L, w1, b1, w2, b2, w3, b3, w4, freq, decay):
    f32 = jnp.float32
    t = jnp.linspace(0.0, 1.0, L, dtype=f32)[:, None]
    w = 2.0 * math.pi * jnp.arange(L, dtype=f32)[:, None] / L
    f = jnp.linspace(1e-4, N_BANDS - 1, N_BANDS, dtype=f32)[None, :]
    z = jnp.concatenate([t, jnp.cos(f * w), -jnp.sin(f * w)], axis=-1)
    fr = freq.astype(f32)
    h = jnp.sin(fr * (z @ w1.astype(f32) + b1.astype(f32)))
    h = jnp.sin(fr * (h @ w2.astype(f32) + b2.astype(f32)))
    h = jnp.sin(fr * (h @ w3.astype(f32) + b3.astype(f32)))
    h = (h @ w4.astype(f32)).reshape(L, 2, E)
    h = h * jnp.exp(-t[:, :, None] * jnp.abs(decay.astype(f32))[None])
    return h[:, 0], h[:, 1]


def fft_longconv(u, hf, hb, bias):
    L = u.shape[1]
    k = jnp.concatenate([hf, jnp.zeros((1, E), jnp.float32), jnp.flip(hb[1:], axis=0)], axis=0)
    uf = u.astype(jnp.float32)
    U = jnp.fft.rfft(uf, n=2 * L, axis=1)
    K = jnp.fft.rfft(k, n=2 * L, axis=0)
    y = jnp.fft.irfft(U * K[None], n=2 * L, axis=1)[:, :L]
    return (y + uf * bias.astype(jnp.float32)).astype(u.dtype)


def hyena_block(u, w_in, conv_w, conv_b, w1, b1, w2, b2, w3, b3, w4, freq, decay, bias, w_out):
    L = u.shape[1]
    p = u @ w_in
    xs, z = p[..., :3 * E], p[..., 3 * E:]
    xs = dwconv(xs, conv_w, conv_b)
    x0, x1, v = jnp.split(xs, 3, axis=-1)
    hf, hb = hyena_filters(L, w1, b1, w2, b2, w3, b3, w4, freq, decay)
    v = fft_longconv(v * x1, hf, hb, bias)
    y = v * x0 * jax.nn.silu(z)
    return y @ w_out


def mlstm_scan(q, k, v, i_pre, log_f, C0, n0, m0):
    B, H, L, _ = q.shape
    nc = L // CHUNK

    def chunks(a):
        return jnp.moveaxis(a.reshape(B, H, nc, CHUNK, *a.shape[3:]), 2, 0)

    bcum = jnp.cumsum(chunks(log_f), axis=-1)
    tril = jnp.tril(jnp.ones((CHUNK, CHUNK), bool))

    def step(carry, xs):
        C, n, m = carry
        qc, kc, vc, ic, bc = xs
        g = bc[..., -1]
        dmat = jnp.where(tril, bc[..., :, None] - bc[..., None, :] + ic[..., None, :], -jnp.inf)
        inter = bc + m[..., None]
        m_t = jnp.maximum(inter, jnp.max(dmat, axis=-1))
        s = jnp.einsum('bhtd,bhsd->bhts', qc, kc) * jnp.exp(dmat - m_t[..., None])
        w_inter = jnp.exp(inter - m_t)
        num = w_inter[..., None] * jnp.einsum('bhtd,bhde->bhte', qc, C) + jnp.einsum('bhts,bhse->bhte', s, vc)
        qn = w_inter * jnp.einsum('bhtd,bhd->bht', qc, n) + jnp.sum(s, axis=-1)
        den = jnp.maximum(jnp.abs(qn), jnp.exp(-m_t))
        h = num / den[..., None]
        a = g[..., None] - bc + ic
        m_new = jnp.maximum(g + m, jnp.max(a, axis=-1))
        wa = jnp.exp(a - m_new[..., None])
        dec = jnp.exp(g + m - m_new)
        C_new = dec[..., None, None] * C + jnp.einsum('bhs,bhsd,bhse->bhde', wa, kc, vc)
        n_new = dec[..., None] * n + jnp.einsum('bhs,bhsd->bhd', wa, kc)
        return (C_new, n_new, m_new), h

    (C, n, m), h = lax.scan(step, (C0, n0, m0), (chunks(q), chunks(k), chunks(v), chunks(i_pre), bcum))
    h = jnp.moveaxis(h, 0, 2).reshape(B, H, L, -1)
    return h, C, n, m


def mlstm_block(u, C0, n0, m0, w_in, conv_w, conv_b, wq, wk, wv, w_gate, b_gate, norm_w, skip, w_out):
    B, L, _ = u.shape
    f32 = jnp.float32
    x, z, o = jnp.split(u @ w_in, 3, axis=-1)
    xc = jax.nn.silu(dwconv(x, conv_w, conv_b))
    xch = xc.reshape(B, L, N_HEADS, DH)
    q = jnp.einsum('blhi,hid->bhld', xch, wq)
    k = jnp.einsum('blhi,hid->bhld', xch, wk) * (DK ** -0.5)
    v = jnp.einsum('blhi,hid->bhld', x.reshape(B, L, N_HEADS, DH), wv)
    flat = lambda a: jnp.swapaxes(a, 1, 2).reshape(B, L, -1)
    gate_in = jnp.concatenate([flat(q), flat(k), flat(v)], axis=-1)
    gates = (gate_in @ w_gate + b_gate).astype(f32).reshape(B, L, 2, 2, N_HEADS)
    gates = jnp.moveaxis(gates, 1, -1)
    i_pre = gates[:, :, 0]
    log_f = jax.nn.log_sigmoid(gates[:, :, 1])
    qf, kf, vf = q.astype(f32), k.astype(f32), v.astype(f32)
    C0, n0, m0 = C0.astype(f32), n0.astype(f32), m0.astype(f32)
    h_f, Cf, nf, mf = mlstm_scan(qf, kf, vf, i_pre[:, 0], log_f[:, 0], C0[:, 0], n0[:, 0], m0[:, 0])
    fl = lambda a: jnp.flip(a, axis=2)
    h_b, Cb, nb, mb = mlstm_scan(fl(qf), fl(kf), fl(vf), jnp.flip(i_pre[:, 1], axis=-1),
                                 jnp.flip(log_f[:, 1], axis=-1), C0[:, 1], n0[:, 1], m0[:, 1])
    h = jnp.swapaxes(h_f + fl(h_b), 1, 2)
    h = jax.nn.sigmoid(o.astype(f32)).reshape(B, L, N_HEADS, DV) * h
    mu = jnp.mean(h, axis=-1, keepdims=True)
    var = jnp.mean(jnp.square(h - mu), axis=-1, keepdims=True)
    h = ((h - mu) * lax.rsqrt(var + EPS)).reshape(B, L, E) * norm_w.astype(f32)
    h = h + skip.astype(f32) * xc.astype(f32)
    y = (h.astype(u.dtype) * jax.nn.silu(z)) @ w_out
    return (y, jnp.stack([Cf, Cb], axis=1), jnp.stack([nf, nb], axis=1), jnp.stack([mf, mb], axis=1))


def setup_inputs(seed: int = 0) -> dict:
    key = jax.random.key(seed)
    ks = iter(jax.random.split(key, 48))
    f32 = jnp.float32

    def nrm(shape, scale):
        return scale * jax.random.normal(next(ks), shape, f32)

    b_i = nrm((N_ML, 2, 1, N_HEADS), 0.1)
    b_f = jnp.linspace(3.0, 6.0, N_HEADS, dtype=f32) + nrm((N_ML, 2, 1, N_HEADS), 0.1)
    return {
        'x_prompt': nrm((BATCH, SEQ, D_MODEL), 1.0),
        'x_sample': nrm((DEC_BATCH, DEC_SEQ, D_MODEL), 1.0),
        'state_C': nrm((DEC_BATCH, N_ML, 2, N_HEADS, DK, DV), 0.3),
        'state_n': nrm((DEC_BATCH, N_ML, 2, N_HEADS, DK), 0.3),
        'state_m': nrm((DEC_BATCH, N_ML, 2, N_HEADS), 0.5),
        'c': nrm((DEC_BATCH, D_MODEL), 1.0),
        'c_ctx': nrm((D_MODEL,), 1.0),
        'norm_pre': 1.0 + nrm((DEPTH, D_MODEL), 0.02),
        'norm_post': 1.0 + nrm((DEPTH, D_MODEL), 0.02),
        'ada_w': nrm((DEPTH, D_MODEL, 3 * D_MODEL), D_MODEL ** -0.5),
        'ada_b': nrm((DEPTH, 3 * D_MODEL), 0.01),
        'hy_w_in': nrm((N_HY, D_MODEL, 4 * E), D_MODEL ** -0.5),
        'hy_conv_w': nrm((N_HY, CONV_W, 3 * E), CONV_W ** -0.5),
        'hy_conv_b': nrm((N_HY, 3 * E), 0.01),
        'hy_ffn_w1': nrm((N_HY, EMB, FILTER_ORDER), EMB ** -0.5),
        'hy_ffn_b1': nrm((N_HY, FILTER_ORDER), 0.1),
        'hy_ffn_w2': nrm((N_HY, FILTER_ORDER, FILTER_ORDER), FILTER_ORDER ** -0.5),
        'hy_ffn_b2': nrm((N_HY, FILTER_ORDER), 0.1),
        'hy_ffn_w3': nrm((N_HY, FILTER_ORDER, FILTER_ORDER), FILTER_ORDER ** -0.5),
        'hy_ffn_b3': nrm((N_HY, FILTER_ORDER), 0.1),
        'hy_ffn_w4': nrm((N_HY, FILTER_ORDER, 2 * E), FILTER_ORDER ** -0.5),
        'hy_sin_freq': 1.0 + nrm((N_HY, FILTER_ORDER), 0.05),
        'hy_decay': jnp.linspace(DECAY_MIN, DECAY_MAX, E, dtype=f32) * (1.0 + nrm((N_HY, 2, E), 0.05)),
        'hy_bias': nrm((N_HY, E), 1.0),
        'hy_w_out': nrm((N_HY, E, D_MODEL), E ** -0.5),
        'ml_w_in': nrm((N_ML, D_MODEL, 3 * E), D_MODEL ** -0.5),
        'ml_conv_w': nrm((N_ML, CONV_W, E), CONV_W ** -0.5),
        'ml_conv_b': nrm((N_ML, E), 0.01),
        'ml_wq': nrm((N_ML, N_HEADS, DH, DK), DH ** -0.5),
        'ml_wk': nrm((N_ML, N_HEADS, DH, DK), DH ** -0.5),
        'ml_wv': nrm((N_ML, N_HEADS, DH, DV), DH ** -0.5),
        'ml_w_gate': nrm((N_ML, GATE_IN, 4 * N_HEADS), GATE_IN ** -0.5),
        'ml_b_gate': jnp.concatenate([b_i, b_f], axis=2).reshape(N_ML, 4 * N_HEADS),
        'ml_norm': 1.0 + nrm((N_ML, E), 0.02),
        'ml_skip': 1.0 + nrm((N_ML, E), 0.02),
        'ml_w_out': nrm((N_ML, E, D_MODEL), E ** -0.5),
    }


def reference(x_prompt, x_sample, state_C, state_n, state_m, c, c_ctx, norm_pre, norm_post, ada_w, ada_b,
              hy_w_in, hy_conv_w, hy_conv_b, hy_ffn_w1, hy_ffn_b1, hy_ffn_w2, hy_ffn_b2, hy_ffn_w3, hy_ffn_b3,
              hy_ffn_w4, hy_sin_freq, hy_decay, hy_bias, hy_w_out, ml_w_in, ml_conv_w, ml_conv_b, ml_wq, ml_wk,
              ml_wv, ml_w_gate, ml_b_gate, ml_norm, ml_skip, ml_w_out):
    f32 = jnp.float32

    def run_trunk(x, cond, init_state):
        new_C, new_n, new_m = [], [], []
        for i in range(DEPTH):
            mod = jax.nn.silu(cond) @ ada_w[i] + ada_b[i]
            shift, scale, gate = [a[:, None, :] for a in jnp.split(mod, 3, axis=-1)]
            u = rmsnorm(x, norm_pre[i]) * (1.0 + scale) + shift
            j = i // N_MIXERS
            if i % N_MIXERS == 0:
                y = hyena_block(u, hy_w_in[j], hy_conv_w[j], hy_conv_b[j], hy_ffn_w1[j], hy_ffn_b1[j],
                                hy_ffn_w2[j], hy_ffn_b2[j], hy_ffn_w3[j], hy_ffn_b3[j], hy_ffn_w4[j],
                                hy_sin_freq[j], hy_decay[j], hy_bias[j], hy_w_out[j])
            else:
                C0, n0, m0 = init_state(j)
                y, Cj, nj, mj = mlstm_block(u, C0, n0, m0, ml_w_in[j], ml_conv_w[j], ml_conv_b[j], ml_wq[j],
                                            ml_wk[j], ml_wv[j], ml_w_gate[j], ml_b_gate[j], ml_norm[j],
                                            ml_skip[j], ml_w_out[j])
                new_C.append(Cj)
                new_n.append(nj)
                new_m.append(mj)
            x = x + gate * rmsnorm(y, norm_post[i])
        return x, new_C, new_n, new_m

    bp = x_prompt.shape[0]
    ctx_init = lambda j: (jnp.zeros((bp, 2, N_HEADS, DK, DV), f32), jnp.zeros((bp, 2, N_HEADS, DK), f32),
                          jnp.zeros((bp, 2, N_HEADS), f32))
    y_prompt, Cs, ns, ms = run_trunk(x_prompt, c_ctx[None, :], ctx_init)
    new_state_C = jnp.stack(Cs, axis=1)
    new_state_n = jnp.stack(ns, axis=1)
    new_state_m = jnp.stack(ms, axis=1)

    lat_init = lambda j: (state_C[:, j], state_n[:, j], state_m[:, j])
    y_sample, _, _, _ = run_trunk(x_sample, c, lat_init)

    return (y_prompt, y_sample, new_state_C, new_state_n, new_state_m)
```

```python
import functools
import math

import numpy as np
import jax
import jax.numpy as jnp
from jax import lax
from jax.experimental import pallas as pl
from jax.experimental.pallas import tpu as pltpu

F32 = jnp.float32
BF16 = jnp.bfloat16
EPS = 1e-6
N_BANDS = 16
HALO = 16
SCAN_CHUNK = 128
FFT_N2 = 128
DIRECT_CONV_MAX_L = 512
VMEM_LIMIT = 56 * 1024 * 1024
HI = lax.Precision.HIGHEST


def _cparams(sem):
    return pltpu.CompilerParams(dimension_semantics=sem, vmem_limit_bytes=VMEM_LIMIT)


def _tile(n, pref, mult):
    if n <= pref:
        return n
    t = (pref // mult) * mult
    while t >= mult:
        if n % t == 0:
            return t
        t -= mult
    return n


def _row_tile(batch, seq_len, pref, mult):
    if seq_len >= pref:
        return _tile(seq_len, pref, mult)
    k = max(1, pref // seq_len)
    while batch % k:
        k -= 1
    return seq_len * k


def _silu(x):
    return x * jax.nn.sigmoid(x)


def _ada_kernel(c_ref, w_ref, b_ref, o_ref):
    a = _silu(c_ref[...]).astype(BF16)
    o_ref[0] = jnp.dot(a, w_ref[0].astype(BF16), preferred_element_type=F32) + b_ref[0]


def _ada(conds, ada_w, ada_b):
    depth, d, n = ada_w.shape
    rows = conds.shape[0]
    tn = _tile(n, 512, 128)
    return pl.pallas_call(
        _ada_kernel,
        grid=(depth, n // tn),
        in_specs=[pl.BlockSpec((rows, d), lambda l, j: (0, 0)),
                  pl.BlockSpec((1, d, tn), lambda l, j: (l, 0, j)),
                  pl.BlockSpec((1, 1, tn), lambda l, j: (l, 0, j))],
        out_specs=pl.BlockSpec((1, rows, tn), lambda l, j: (l, 0, j)),
        out_shape=jax.ShapeDtypeStruct((depth, rows, n), F32),
        compiler_params=_cparams(("arbitrary", "arbitrary")),
        name="ada_mod",
    )(conds, ada_w, ada_b.reshape(depth, 1, n))


def _inproj_kernel(x_ref, g_ref, shift_ref, scale_ref, w_ref, o_ref, u_scr):
    @pl.when(pl.program_id(1) == 0)
    def _():
        x = x_ref[...]
        r = lax.rsqrt(jnp.mean(x * x, axis=-1, keepdims=True) + EPS)
        u = x * r * g_ref[...]
        u_scr[...] = (u * (1.0 + scale_ref[0]) + shift_ref[0]).astype(BF16)

    o_ref[...] = jnp.dot(u_scr[...], w_ref[...], preferred_element_type=F32).astype(o_ref.dtype)


def _cond_index(trunk, tm):
    batch, seq_len, cond0, per_seq = trunk
    if per_seq:
        return lambda i: cond0 + (i * tm) // seq_len
    return lambda i: cond0


def _cond_row_tile(trunk, pref):
    batch, seq_len, _, per_seq = trunk
    if per_seq:
        return _tile(seq_len, pref, 16)
    return _tile(batch * seq_len, pref, 16)


def _inproj(x, norm_g, mod, trunk, w, out_dtype):
    t, d = x.shape
    n = w.shape[1]
    tm = _cond_row_tile(trunk, 1024)
    tn = _tile(n, 1024, 128)
    cidx = _cond_index(trunk, tm)
    return pl.pallas_call(
        _inproj_kernel,
        grid=(t // tm, n // tn),
        in_specs=[pl.BlockSpec((tm, d), lambda i, j: (i, 0)),
                  pl.BlockSpec((1, d), lambda i, j: (0, 0)),
                  pl.BlockSpec((1, 1, d), lambda i, j: (cidx(i), 0, 0)),
                  pl.BlockSpec((1, 1, d), lambda i, j: (cidx(i), 0, 1)),
                  pl.BlockSpec((d, tn), lambda i, j: (0, j))],
        out_specs=pl.BlockSpec((tm, tn), lambda i, j: (i, j)),
        out_shape=jax.ShapeDtypeStruct((t, n), out_dtype),
        scratch_shapes=[pltpu.VMEM((tm, d), BF16)],
        compiler_params=_cparams(("arbitrary", "arbitrary")),
        name="in_proj",
    )(x, norm_g.reshape(1, d), mod, mod, w)


def _outproj_epilogue(acc, x_ref, g_ref, gate_ref, o_ref):
    y = acc
    r = lax.rsqrt(jnp.mean(y * y, axis=-1, keepdims=True) + EPS)
    o_ref[...] = x_ref[...] + gate_ref[0] * (y * r * g_ref[...])


def _outproj_kernel(y_ref, w_ref, x_ref, g_ref, gate_ref, o_ref, acc):
    k = pl.program_id(1)

    @pl.when(k == 0)
    def _():
        acc[...] = jnp.zeros_like(acc)

    acc[...] += jnp.dot(y_ref[...], w_ref[...], preferred_element_type=F32)

    @pl.when(k == pl.num_programs(1) - 1)
    def _():
        _outproj_epilogue(acc[...], x_ref, g_ref, gate_ref, o_ref)


def _outproj(y, w, x, norm_g, mod, trunk):
    t, e = y.shape
    d = w.shape[1]
    tm = _cond_row_tile(trunk, 512)
    tk = _tile(e, 1024, 128)
    cidx = _cond_index(trunk, tm)
    return pl.pallas_call(
        _outproj_kernel,
        grid=(t // tm, e // tk),
        in_specs=[pl.BlockSpec((tm, tk), lambda i, k: (i, k)),
                  pl.BlockSpec((tk, d), lambda i, k: (k, 0)),
                  pl.BlockSpec((tm, d), lambda i, k: (i, 0)),
                  pl.BlockSpec((1, d), lambda i, k: (0, 0)),
                  pl.BlockSpec((1, 1, d), lambda i, k: (cidx(i), 0, 2))],
        out_specs=pl.BlockSpec((tm, d), lambda i, k: (i, 0)),
        out_shape=jax.ShapeDtypeStruct((t, d), F32),
        scratch_shapes=[pltpu.VMEM((tm, d), F32)],
        compiler_params=_cparams(("arbitrary", "arbitrary")),
        name="out_proj",
    )(y, w, x, norm_g.reshape(1, d), mod)


def _conv3(x, prev_row, next_row, w, b, row0, seq_len):
    tm = x.shape[0]
    rows = lax.broadcasted_iota(jnp.int32, x.shape, 0)
    pos = (rows + row0) % seq_len
    prev = jnp.where(rows == 0, prev_row, pltpu.roll(x, 1, 0))
    nxt = jnp.where(rows == tm - 1, next_row, pltpu.roll(x, tm - 1, 0))
    prev = jnp.where(pos == 0, 0.0, prev)
    nxt = jnp.where(pos == seq_len - 1, 0.0, nxt)
    return w[0:1] * prev + w[1:2] * x + w[2:3] * nxt + b


def _halo_specs(tm, tc, nrows, colblk):
    nh = nrows // HALO
    r = tm // HALO
    return [pl.BlockSpec((tm, tc), lambda i, j: (i, colblk(j))),
            pl.BlockSpec((HALO, tc), lambda i, j: (jnp.maximum(i * r - 1, 0), colblk(j))),
            pl.BlockSpec((HALO, tc), lambda i, j: (jnp.minimum((i + 1) * r, nh - 1), colblk(j)))]


def _hy_gate_kernel(seq_len, x0, x0p, x0n, x1, x1p, x1n, v, vp, vn, z, cw, cb, vx_ref, g0_ref):
    tm = x0.shape[0]
    row0 = pl.program_id(0) * tm

    def conv(m, p, n, k):
        return _conv3(m[...].astype(F32), p[HALO - 1:HALO, :].astype(F32), n[0:1, :].astype(F32),
                      cw[k], cb[k], row0, seq_len)

    x0c = conv(x0, x0p, x0n, 0)
    x1c = conv(x1, x1p, x1n, 1)
    vc = conv(v, vp, vn, 2)
    vx_ref[...] = (vc * x1c).astype(vx_ref.dtype)
    g0_ref[...] = (x0c * _silu(z[...].astype(F32))).astype(g0_ref.dtype)


def _hy_gate(p, conv_w, conv_b, batch, seq_len, e):
    t = p.shape[0]
    tm = _row_tile(batch, seq_len, 256, HALO)
    tc = _tile(e, 1024, 128)
    nb = e // tc
    specs = []
    for part in range(3):
        specs += _halo_specs(tm, tc, t, lambda j, part=part: part * nb + j)
    specs += [pl.BlockSpec((tm, tc), lambda i, j: (i, 3 * nb + j)),
              pl.BlockSpec((3, 3, tc), lambda i, j: (0, 0, j)),
              pl.BlockSpec((3, 1, tc), lambda i, j: (0, 0, j))]
    cw = conv_w.reshape(3, 3, e).transpose(1, 0, 2)
    cb = conv_b.reshape(3, 1, e)
    return pl.pallas_call(
        functools.partial(_hy_gate_kernel, seq_len),
        grid=(t // tm, nb),
        in_specs=specs,
        out_specs=[pl.BlockSpec((tm, tc), lambda i, j: (i, j))] * 2,
        out_shape=[jax.ShapeDtypeStruct((t, e), BF16), jax.ShapeDtypeStruct((t, e), F32)],
        compiler_params=_cparams(("arbitrary", "arbitrary")),
        name="hy_gate",
    )(p, p, p, p, p, p, p, p, p, p, cw, cb)


def _filter_tables(seq_len):
    l = seq_len
    t = np.linspace(0.0, 1.0, l)
    w = 2.0 * math.pi * np.arange(l) / l
    f = np.linspace(1e-4, N_BANDS - 1, N_BANDS)
    z = np.concatenate([t[:, None], np.cos(f[None] * w[:, None]), -np.sin(f[None] * w[:, None])], axis=-1)
    pos = np.concatenate([np.arange(l), [0], np.arange(l - 1, 0, -1)])
    z2 = np.zeros((2 * l, 128), np.float32)
    z2[:, :z.shape[1]] = z[pos]
    aux = np.zeros((2 * l, 128), np.float32)
    aux[:, 0] = t[pos]
    aux[:, 1] = 1.0
    aux[l, 1] = 0.0
    return jnp.asarray(z2), jnp.asarray(aux)


def _filter_kernel(z_ref, aux_ref, w1, b1, w2, b2, w3, b3, fr, w4, dec, o_ref):
    f = fr[...]
    h = jnp.sin(f * (jnp.dot(z_ref[...], w1[...], precision=HI, preferred_element_type=F32) + b1[...]))
    h = jnp.sin(f * (jnp.dot(h, w2[...], precision=HI, preferred_element_type=F32) + b2[...]))
    h = jnp.sin(f * (jnp.dot(h, w3[...], precision=HI, preferred_element_type=F32) + b3[...]))
    k = jnp.dot(h, w4[...], precision=HI, preferred_element_type=F32)
    t = aux_ref[:, 0:1]
    keep = aux_ref[:, 1:2]
    o_ref[...] = k * jnp.exp(-t * jnp.abs(dec[0])) * keep


def _filters(seq_len, w1, b1, w2, b2, w3, b3, w4, freq, decay):
    e = decay.shape[-1]
    fo = w2.shape[0]
    z2, aux = _filter_tables(seq_len)
    w1p = jnp.zeros((128, fo), F32).at[:w1.shape[0]].set(w1)
    tr = _tile(seq_len, 512, 8)
    tc = _tile(e, 2048, 128)
    nb = e // tc
    nr = seq_len // tr
    small = lambda a: pl.BlockSpec(a.shape, lambda i, j: (0,) * a.ndim)
    b1r, b2r, b3r, frr = (a.reshape(1, fo) for a in (b1, b2, b3, freq))
    return pl.pallas_call(
        _filter_kernel,
        grid=(2 * nr, nb),
        in_specs=[pl.BlockSpec((tr, 128), lambda i, j: (i, 0)),
                  pl.BlockSpec((tr, 128), lambda i, j: (i, 0)),
                  small(w1p), small(b1r), small(w2), small(b2r), small(w3), small(b3r), small(frr),
                  pl.BlockSpec((fo, tc), lambda i, j: (0, (i // nr) * nb + j)),
                  pl.BlockSpec((1, 1, tc), lambda i, j: (i // nr, 0, j))],
        out_specs=pl.BlockSpec((tr, tc), lambda i, j: (i, j)),
        out_shape=jax.ShapeDtypeStruct((2 * seq_len, e), F32),
        compiler_params=_cparams(("arbitrary", "arbitrary")),
        name="hy_filters",
    )(z2, aux, w1p, b1r, w2, b2r, w3, b3r, frr, w4, decay.reshape(2, 1, e))


def _cmm_kernel(w_ref, x_ref, o_ref):
    o_ref[0] = jnp.dot(w_ref[...], x_ref[0], preferred_element_type=F32).astype(o_ref.dtype)


def _cmm_epi_kernel(w_ref, x_ref, vx_ref, g0_ref, bias_ref, o_ref):
    y = jnp.dot(w_ref[...], x_ref[0], preferred_element_type=F32)
    y = y + vx_ref[0].astype(F32) * bias_ref[...]
    o_ref[0] = (y * g0_ref[0]).astype(o_ref.dtype)


def _cmm(w, x, out_dtype, epilogue=None, bias_period=None):
    m, k = w.shape
    b, _, n = x.shape
    tn = _tile(n, 4096, 128) if epilogue is None else _tile(bias_period, 4096, 128)
    in_specs = [pl.BlockSpec((m, k), lambda bi, j: (0, 0)),
                pl.BlockSpec((1, k, tn), lambda bi, j: (bi, 0, j))]
    args = [w, x]
    kern = _cmm_kernel
    if epilogue is not None:
        vx, g0, bias = epilogue
        nbias = bias_period // tn
        in_specs += [pl.BlockSpec((1, m, tn), lambda bi, j: (bi, 0, j)),
                     pl.BlockSpec((1, m, tn), lambda bi, j: (bi, 0, j)),
                     pl.BlockSpec((1, tn), lambda bi, j: (0, j % nbias))]
        args += [vx, g0, bias]
        kern = _cmm_epi_kernel
    return pl.pallas_call(
        kern,
        grid=(b, n // tn),
        in_specs=in_specs,
        out_specs=pl.BlockSpec((1, m, tn), lambda bi, j: (bi, 0, j)),
        out_shape=jax.ShapeDtypeStruct((b, m, n), out_dtype),
        compiler_params=_cparams(("arbitrary", "arbitrary")),
        name="const_lhs_matmul",
    )(*args)


def _direct_mats(seq_len):
    l = seq_len
    n = 2 * l
    f = np.arange(l)[:, None]
    t = np.arange(n)[None, :]
    ang = 2.0 * math.pi * ((f * t) % n) / n
    fwd = np.concatenate([np.cos(ang), -np.sin(ang)], axis=0)
    fwd[l] = np.cos(math.pi * np.arange(n))
    tt = np.arange(l)[:, None]
    ff = np.arange(l)[None, :]
    ang2 = 2.0 * math.pi * ((tt * ff) % n) / n
    wgt = np.where(ff == 0, 1.0, 2.0) / n
    inv = np.concatenate([wgt * np.cos(ang2), -wgt * np.sin(ang2)], axis=1)
    inv[:, l] = np.cos(math.pi * np.arange(l)) / n
    return fwd, inv


def _direct_conv_kernel(seq_len, f_ref, v_ref, x_ref, k_ref, g0_ref, bias_ref, o_ref):
    l = seq_len
    x = x_ref[0]
    s = jnp.dot(f_ref[...], x, preferred_element_type=F32)
    sre, sim = s[:l], s[l:]
    kre, kim = k_ref[:l, :], k_ref[l:, :]
    row0 = lax.broadcasted_iota(jnp.int32, sre.shape, 0) == 0
    yre = sre * kre - jnp.where(row0, 0.0, sim * kim)
    yim = jnp.where(row0, sim * kim, sre * kim + sim * kre)
    y = jnp.concatenate([yre, yim], axis=0).astype(BF16)
    out = jnp.dot(v_ref[...], y, preferred_element_type=F32)
    out = out + x.astype(F32) * bias_ref[...]
    o_ref[0] = (out * g0_ref[0]).astype(o_ref.dtype)


def _longconv_direct(vx, g0, kfilt, bias, batch, seq_len):
    e = vx.shape[-1]
    l = seq_len
    fwd, inv = _direct_mats(l)
    f_full = jnp.asarray(fwd, BF16)
    f_data = jnp.asarray(fwd[:, :l], BF16)
    v_mat = jnp.asarray(inv, BF16)
    kspec = _cmm(f_full, kfilt.astype(BF16)[None], F32)[0]
    tc = _tile(e, 1024, 128)
    out = pl.pallas_call(
        functools.partial(_direct_conv_kernel, l),
        grid=(e // tc, batch),
        in_specs=[pl.BlockSpec((2 * l, l), lambda j, b: (0, 0)),
                  pl.BlockSpec((l, 2 * l), lambda j, b: (0, 0)),
                  pl.BlockSpec((1, l, tc), lambda j, b: (b, 0, j)),
                  pl.BlockSpec((2 * l, tc), lambda j, b: (0, j)),
                  pl.BlockSpec((1, l, tc), lambda j, b: (b, 0, j)),
                  pl.BlockSpec((1, tc), lambda j, b: (0, j))],
        out_specs=pl.BlockSpec((1, l, tc), lambda j, b: (b, 0, j)),
        out_shape=jax.ShapeDtypeStruct((batch, l, e), BF16),
        compiler_params=_cparams(("arbitrary", "arbitrary")),
        name="longconv_direct",
    )(f_data, v_mat, vx.reshape(batch, l, e), kspec, g0.reshape(batch, l, e), bias.reshape(1, e))
    return out.reshape(batch * l, e)


def _two_level_mats(seq_len):
    n = 2 * seq_len
    n2 = FFT_N2
    n1 = n // n2
    h2 = n2 // 2
    ns = 8 * ((h2 + 1 + 7) // 8)
    f2 = np.arange(h2 + 1)[:, None]
    t2 = np.arange(n2)[None, :]
    ang = 2.0 * math.pi * ((f2 * t2) % n2) / n2
    f1m = np.zeros((2 * ns, n2))
    f1m[0:2 * (h2 + 1):2] = np.cos(ang)
    f1m[1:2 * (h2 + 1):2] = -np.sin(ang)
    wgt = np.where((f2 == 0) | (f2 == h2), 1.0, 2.0) / n
    g1m = np.zeros((h2, 2 * ns))
    g1m[:, 0:2 * (h2 + 1):2] = (wgt * np.cos(ang[:, :h2])).T
    g1m[:, 1:2 * (h2 + 1):2] = -(wgt * np.sin(ang[:, :h2])).T
    t1 = np.arange(n1)[None, :]
    f1 = np.arange(n1)[:, None]
    rm = np.zeros((ns, 2 * n1, 2 * n1))
    pm = np.zeros((ns, 2 * n1, 2 * n1))
    for s in range(h2 + 1):
        a = 2.0 * math.pi * ((t1 * (n2 * f1 + s)) % n) / n
        mr, mi = np.cos(a), -np.sin(a)
        rm[s] = np.block([[mr, -mi], [mi, mr]])
        pr, pi = mr.T, -mi.T
        pm[s] = np.block([[pr, -pi], [pi, pr]])
    return n1, n2, ns, f1m, g1m, rm, pm


def _slab_spec_kernel(r_ref, a_ref, o_ref):
    n1 = a_ref.shape[3]
    a = jnp.concatenate([a_ref[0, 0, 0], a_ref[0, 0, 1]], axis=0)
    s = jnp.dot(r_ref[0], a, preferred_element_type=F32)
    o_ref[0, 0] = s[:n1]
    o_ref[0, 1] = s[n1:]


def _slab_conv_kernel(r_ref, p_ref, a_ref, k_ref, o_ref):
    n1 = a_ref.shape[3]
    a = jnp.concatenate([a_ref[0, 0, 0], a_ref[0, 0, 1]], axis=0)
    s = jnp.dot(r_ref[0], a, preferred_element_type=F32)
    sre, sim = s[:n1], s[n1:]
    kre, kim = k_ref[0, 0], k_ref[0, 1]
    y = jnp.concatenate([sre * kre - sim * kim, sre * kim + sim * kre], axis=0).astype(BF16)
    o = jnp.dot(p_ref[0], y, preferred_element_type=F32)
    o_ref[0, 0, 0] = o[:n1].astype(o_ref.dtype)
    o_ref[0, 0, 1] = o[n1:].astype(o_ref.dtype)


def _longconv_two_level(vx, g0, kfilt, bias, batch, seq_len):
    e = vx.shape[-1]
    l = seq_len
    n1, n2, ns, f1m, g1m, rm, pm = _two_level_mats(l)
    h2 = n2 // 2
    f1_full = jnp.asarray(f1m, BF16)
    f1_data = jnp.asarray(f1m[:, :h2], BF16)
    g1 = jnp.asarray(g1m, BF16)
    rmat = jnp.asarray(rm, BF16)
    pmat = jnp.asarray(pm, BF16)
    tc = _tile(e, 4096, 128)
    nc = e // tc

    ka = _cmm(f1_full, kfilt.astype(BF16).reshape(1, n2, n1 * e), BF16).reshape(1, ns, 2, n1, e)
    kspec = pl.pallas_call(
        _slab_spec_kernel,
        grid=(ns, nc),
        in_specs=[pl.BlockSpec((1, 2 * n1, 2 * n1), lambda s, j: (s, 0, 0)),
                  pl.BlockSpec((1, 1, 2, n1, tc), lambda s, j: (0, s, 0, 0, j))],
        out_specs=pl.BlockSpec((1, 2, n1, tc), lambda s, j: (s, 0, 0, j)),
        out_shape=jax.ShapeDtypeStruct((ns, 2, n1, e), F32),
        compiler_params=_cparams(("arbitrary", "arbitrary")),
        name="filter_slab_dft",
    )(rmat, ka)

    a = _cmm(f1_data, vx.reshape(batch, h2, n1 * e), BF16).reshape(batch, ns, 2, n1, e)
    a2 = pl.pallas_call(
        _slab_conv_kernel,
        grid=(ns, batch, nc),
        in_specs=[pl.BlockSpec((1, 2 * n1, 2 * n1), lambda s, b, j: (s, 0, 0)),
                  pl.BlockSpec((1, 2 * n1, 2 * n1), lambda s, b, j: (s, 0, 0)),
                  pl.BlockSpec((1, 1, 2, n1, tc), lambda s, b, j: (b, s, 0, 0, j)),
                  pl.BlockSpec((1, 2, n1, tc), lambda s, b, j: (s, 0, 0, j))],
        out_specs=pl.BlockSpec((1, 1, 2, n1, tc), lambda s, b, j: (b, s, 0, 0, j)),
        out_shape=jax.ShapeDtypeStruct((batch, ns, 2, n1, e), BF16),
        compiler_params=_cparams(("arbitrary", "arbitrary", "arbitrary")),
        name="slab_conv",
    )(rmat, pmat, a, kspec)
    a2 = a2.reshape(batch, 2 * ns, n1 * e)
    bias_t = bias.reshape(1, e)
    out = _cmm(g1, a2, BF16,
               epilogue=(vx.reshape(batch, h2, n1 * e), g0.reshape(batch, h2, n1 * e), bias_t),
               bias_period=e)
    return out.reshape(batch * l, e)


def _longconv(vx, g0, kfilt, bias, batch, seq_len):
    if seq_len <= DIRECT_CONV_MAX_L:
        return _longconv_direct(vx, g0, kfilt, bias, batch, seq_len)
    return _longconv_two_level(vx, g0, kfilt, bias, batch, seq_len)


def _ml_qkv_kernel(seq_len, n_heads, dk, x_ref, xp_ref, xn_ref, cw_ref, cb_ref, wq_ref, wk_ref, wv_ref,
                   wgq_ref, wgk_ref, wgv_ref, bg_ref, q_ref, k_ref, v_ref, xc_ref, g_ref, gt_ref, gacc):
    h = pl.program_id(1)
    tm = x_ref.shape[0]
    x = x_ref[...].astype(F32)
    conv = _conv3(x, xp_ref[HALO - 1:HALO, :].astype(F32), xn_ref[0:1, :].astype(F32),
                  cw_ref[...], cb_ref[...], pl.program_id(0) * tm, seq_len)
    xc = _silu(conv)
    xc_ref[...] = xc.astype(xc_ref.dtype)
    xcb = xc.astype(BF16)
    q = jnp.dot(xcb, wq_ref[0], preferred_element_type=F32).astype(BF16)
    k = (jnp.dot(xcb, wk_ref[0], preferred_element_type=F32) * (dk ** -0.5)).astype(BF16)
    v = jnp.dot(x.astype(BF16), wv_ref[0], preferred_element_type=F32).astype(BF16)
    q_ref[...] = q
    k_ref[...] = k
    v_ref[...] = v
    part = (jnp.dot(q, wgq_ref[0], preferred_element_type=F32)
            + jnp.dot(k, wgk_ref[0], preferred_element_type=F32)
            + jnp.dot(v, wgv_ref[0], preferred_element_type=F32))

    @pl.when(h == 0)
    def _():
        gacc[...] = part + bg_ref[...]

    @pl.when(h > 0)
    def _():
        gacc[...] += part

    @pl.when(h == n_heads - 1)
    def _():
        g = gacc[...]
        col = lax.broadcasted_iota(jnp.int32, g.shape, 1)
        is_forget = (col % (2 * n_heads)) >= n_heads
        g = jnp.where(is_forget, jax.nn.log_sigmoid(g), g)
        g_ref[...] = g
        q_len = gt_ref.shape[2]
        for c in range(tm // q_len):
            gt_ref[c] = g[c * q_len:(c + 1) * q_len, :].T


def _ml_qkv(p, conv_w, conv_b, wq, wk, wv, w_gate, b_gate, batch, seq_len, e):
    t = p.shape[0]
    n_heads, dh, dk = wq.shape
    dv = wv.shape[2]
    q_len = min(SCAN_CHUNK, seq_len)
    tm = _row_tile(batch, seq_len, 512, q_len)
    ng = w_gate.shape[1]
    gpad = 128
    wgq = jnp.zeros((n_heads, dk, gpad), BF16).at[:, :, :ng].set(
        w_gate[:n_heads * dk].reshape(n_heads, dk, ng).astype(BF16))
    wgk = jnp.zeros((n_heads, dk, gpad), BF16).at[:, :, :ng].set(
        w_gate[n_heads * dk:2 * n_heads * dk].reshape(n_heads, dk, ng).astype(BF16))
    wgv = jnp.zeros((n_heads, dv, gpad), BF16).at[:, :, :ng].set(
        w_gate[2 * n_heads * dk:].reshape(n_heads, dv, ng).astype(BF16))
    bg = jnp.zeros((1, gpad), F32).at[0, :ng].set(b_gate)
    specs = _halo_specs(tm, dh, t, lambda j: j)
    specs += [pl.BlockSpec((3, dh), lambda i, j: (0, j)),
              pl.BlockSpec((1, dh), lambda i, j: (0, j)),
              pl.BlockSpec((1, dh, dk), lambda i, j: (j, 0, 0)),
              pl.BlockSpec((1, dh, dk), lambda i, j: (j, 0, 0)),
              pl.BlockSpec((1, dh, dv), lambda i, j: (j, 0, 0)),
              pl.BlockSpec((1, dk, gpad), lambda i, j: (j, 0, 0)),
              pl.BlockSpec((1, dk, gpad), lambda i, j: (j, 0, 0)),
              pl.BlockSpec((1, dv, gpad), lambda i, j: (j, 0, 0)),
              pl.BlockSpec((1, gpad), lambda i, j: (0, 0))]
    return pl.pallas_call(
        functools.partial(_ml_qkv_kernel, seq_len, n_heads, dk),
        grid=(t // tm, n_heads),
        in_specs=specs,
        out_specs=[pl.BlockSpec((tm, dk), lambda i, j: (i, j)),
                   pl.BlockSpec((tm, dk), lambda i, j: (i, j)),
                   pl.BlockSpec((tm, dv), lambda i, j: (i, j)),
                   pl.BlockSpec((tm, dh), lambda i, j: (i, j)),
                   pl.BlockSpec((tm, gpad), lambda i, j: (i, 0)),
                   pl.BlockSpec((tm // q_len, gpad, q_len), lambda i, j: (i, 0, 0))],
        out_shape=[jax.ShapeDtypeStruct((t, n_heads * dk), BF16),
                   jax.ShapeDtypeStruct((t, n_heads * dk), BF16),
                   jax.ShapeDtypeStruct((t, n_heads * dv), BF16),
                   jax.ShapeDtypeStruct((t, e), F32),
                   jax.ShapeDtypeStruct((t, gpad), F32),
                   jax.ShapeDtypeStruct((t // q_len, gpad, q_len), F32)],
        scratch_shapes=[pltpu.VMEM((tm, gpad), F32)],
        compiler_params=_cparams(("arbitrary", "arbitrary")),
        name="ml_qkv_gates",
    )(p, p, p, conv_w, conv_b.reshape(1, e), wq.astype(BF16), wk.astype(BF16), wv.astype(BF16),
      wgq, wgk, wgv, bg)


def _ml_scan_kernel(n_heads, has_init, *refs):
    if has_init:
        (q_ref, k_ref, v_ref, g_ref, gt_ref, c0_ref, n0_ref, m0_ref,
         h_ref, cout_ref, nout_ref, mout_ref, c_scr, n_scr, m_scr) = refs
    else:
        (q_ref, k_ref, v_ref, g_ref, gt_ref,
         h_ref, cout_ref, nout_ref, mout_ref, c_scr, n_scr, m_scr) = refs
    head = pl.program_id(1)
    seq_len = q_ref.shape[0]
    q_len = gt_ref.shape[2]
    n_chunks = seq_len // q_len

    for d in range(2):
        if has_init:
            c_scr[d] = c0_ref[0, 0, d, 0]
            n_scr[d] = n0_ref[0, 0, d]
            m_scr[d] = m0_ref[0, 0, d]
        else:
            c_scr[d] = jnp.zeros(c_scr.shape[1:], F32)
            n_scr[d] = jnp.zeros(n_scr.shape[1:], F32)
            m_scr[d] = jnp.zeros(m_scr.shape[1:], F32)
    h_ref[...] = jnp.zeros_like(h_ref)

    row_i = lax.broadcasted_iota(jnp.int32, (q_len, q_len), 0)
    col_i = lax.broadcasted_iota(jnp.int32, (q_len, q_len), 1)
    lane_g = lax.broadcasted_iota(jnp.int32, (q_len, g_ref.shape[1]), 1)

    def direction(d, c):
        icol = d * 2 * n_heads + head
        fcol = icol + n_heads
        rows = pl.ds(pl.multiple_of(c * q_len, q_len), q_len)
        g = g_ref[rows, :]
        i_col = jnp.sum(jnp.where(lane_g == icol, g, 0.0), axis=1, keepdims=True)
        lf_col = jnp.sum(jnp.where(lane_g == fcol, g, 0.0), axis=1, keepdims=True)
        i_row = gt_ref[c, pl.ds(icol, 1), :]
        lf_row = gt_ref[c, pl.ds(fcol, 1), :]
        mask = (col_i <= row_i) if d == 0 else (col_i >= row_i)
        maskt = (row_i <= col_i) if d == 0 else (row_i >= col_i)
        b_col = jnp.sum(jnp.where(mask, lf_row, 0.0), axis=1, keepdims=True)
        b_row = jnp.sum(jnp.where(maskt, lf_col, 0.0), axis=0, keepdims=True)
        g_tot = jnp.sum(lf_row, axis=1, keepdims=True)
        m_prev = m_scr[d][:, 0:1]
        n_prev = n_scr[d]
        c_prev = c_scr[d]

        qc = q_ref[rows, :]
        kc = k_ref[rows, :]
        vc = v_ref[rows, :]
        dmat = jnp.where(mask, b_col - b_row + i_row, -jnp.inf)
        inter = b_col + m_prev
        m_t = jnp.maximum(inter, jnp.max(dmat, axis=1, keepdims=True))
        qk = lax.dot_general(qc, kc, (((1,), (1,)), ((), ())), preferred_element_type=F32)
        s = qk * jnp.exp(dmat - m_t)
        w_inter = jnp.exp(inter - m_t)
        num = (w_inter * jnp.dot(qc, c_prev.astype(BF16), preferred_element_type=F32)
               + jnp.dot(s.astype(BF16), vc, preferred_element_type=F32))
        qn = (w_inter * jnp.sum(qc.astype(F32) * n_prev, axis=1, keepdims=True)
              + jnp.sum(s, axis=1, keepdims=True))
        den = jnp.maximum(jnp.abs(qn), jnp.exp(-m_t))
        h_ref[rows, :] += num / den

        a_col = g_tot - b_col + i_col
        m_new = jnp.maximum(g_tot + m_prev, jnp.max(a_col, axis=0, keepdims=True))
        wa = jnp.exp(a_col - m_new)
        dec = jnp.exp(g_tot + m_prev - m_new)
        kw = kc.astype(F32) * wa
        c_scr[d] = dec * c_prev + lax.dot_general(kw.astype(BF16), vc, (((0,), (0,)), ((), ())),
                                                  preferred_element_type=F32)
        n_scr[d] = dec * n_prev + jnp.sum(kw, axis=0, keepdims=True)
        m_scr[d] = jnp.broadcast_to(m_new, m_scr.shape[1:])

    def body(j, carry):
        direction(0, j)
        direction(1, n_chunks - 1 - j)
        return carry

    lax.fori_loop(0, n_chunks, body, 0)
    for d in range(2):
        cout_ref[0, 0, d, 0] = c_scr[d]
        nout_ref[0, 0, d] = n_scr[d]
        mout_ref[0, 0, d] = m_scr[d]


def _ml_scan(q, k, v, g, gt, batch, seq_len, n_heads, init=None):
    dk = q.shape[1] // n_heads
    dv = v.shape[1] // n_heads
    q_len = gt.shape[2]
    gpad = g.shape[1]
    nq = seq_len // q_len
    in_specs = [pl.BlockSpec((seq_len, dk), lambda b, h: (b, h)),
                pl.BlockSpec((seq_len, dk), lambda b, h: (b, h)),
                pl.BlockSpec((seq_len, dv), lambda b, h: (b, h)),
                pl.BlockSpec((seq_len, gpad), lambda b, h: (b, 0)),
                pl.BlockSpec((nq, gpad, q_len), lambda b, h: (b, 0, 0))]
    args = [q, k, v, g, gt]
    if init is not None:
        c0, n0, m0 = init
        in_specs += [pl.BlockSpec((1, 1, 2, 1, dk, dv), lambda b, h: (b, 0, 0, h, 0, 0)),
                     pl.BlockSpec((1, 1, 2, 1, dk), lambda b, h: (b, h, 0, 0, 0)),
                     pl.BlockSpec((1, 1, 2, 1, 128), lambda b, h: (b, h, 0, 0, 0))]
        args += [c0, n0, m0]
    out = pl.pallas_call(
        functools.partial(_ml_scan_kernel, n_heads, init is not None),
        grid=(batch, n_heads),
        in_specs=in_specs,
        out_specs=[pl.BlockSpec((seq_len, dv), lambda b, h: (b, h)),
                   pl.BlockSpec((1, 1, 2, 1, dk, dv), lambda b, h: (b, 0, 0, h, 0, 0)),
                   pl.BlockSpec((1, 1, 2, 1, dk), lambda b, h: (b, h, 0, 0, 0)),
                   pl.BlockSpec((1, 1, 2, 1, 128), lambda b, h: (b, h, 0, 0, 0))],
        out_shape=[jax.ShapeDtypeStruct((batch * seq_len, n_heads * dv), F32),
                   jax.ShapeDtypeStruct((batch, 1, 2, n_heads, dk, dv), F32),
                   jax.ShapeDtypeStruct((batch, n_heads, 2, 1, dk), F32),
                   jax.ShapeDtypeStruct((batch, n_heads, 2, 1, 128), F32)],
        scratch_shapes=[pltpu.VMEM((2, dk, dv), F32), pltpu.VMEM((2, 1, dk), F32),
                        pltpu.VMEM((2, 1, 128), F32)],
        compiler_params=_cparams(("arbitrary", "arbitrary")),
        name="ml_scan",
    )(*args)
    return out


def _ml_out_kernel(h_ref, o_ref_in, xc_ref, z_ref, nw_ref, sk_ref, w_ref, x_ref, g_ref, gate_ref, out_ref, acc):
    k = pl.program_id(1)

    @pl.when(k == 0)
    def _():
        acc[...] = jnp.zeros_like(acc)

    hh = jax.nn.sigmoid(o_ref_in[...].astype(F32)) * h_ref[...]
    mu = jnp.mean(hh, axis=-1, keepdims=True)
    var = jnp.mean(jnp.square(hh - mu), axis=-1, keepdims=True)
    hn = (hh - mu) * lax.rsqrt(var + EPS) * nw_ref[...]
    hn = hn + sk_ref[...] * xc_ref[...].astype(F32)
    y = (hn * _silu(z_ref[...].astype(F32))).astype(BF16)
    acc[...] += jnp.dot(y, w_ref[...], preferred_element_type=F32)

    @pl.when(k == pl.num_programs(1) - 1)
    def _():
        _outproj_epilogue(acc[...], x_ref, g_ref, gate_ref, out_ref)


def _ml_out(h, p, xc, norm_w, skip, w, x, norm_g, mod, trunk, n_heads):
    t, e = h.shape
    d = w.shape[1]
    dv = e // n_heads
    tm = _cond_row_tile(trunk, 512)
    cidx = _cond_index(trunk, tm)
    return pl.pallas_call(
        _ml_out_kernel,
        grid=(t // tm, n_heads),
        in_specs=[pl.BlockSpec((tm, dv), lambda i, k: (i, k)),
                  pl.BlockSpec((tm, dv), lambda i, k: (i, 2 * n_heads + k)),
                  pl.BlockSpec((tm, dv), lambda i, k: (i, k)),
                  pl.BlockSpec((tm, dv), lambda i, k: (i, n_heads + k)),
                  pl.BlockSpec((1, dv), lambda i, k: (0, k)),
                  pl.BlockSpec((1, dv), lambda i, k: (0, k)),
                  pl.BlockSpec((dv, d), lambda i, k: (k, 0)),
                  pl.BlockSpec((tm, d), lambda i, k: (i, 0)),
                  pl.BlockSpec((1, d), lambda i, k: (0, 0)),
                  pl.BlockSpec((1, 1, d), lambda i, k: (cidx(i), 0, 2))],
        out_specs=pl.BlockSpec((tm, d), lambda i, k: (i, 0)),
        out_shape=jax.ShapeDtypeStruct((t, d), F32),
        scratch_shapes=[pltpu.VMEM((tm, d), F32)],
        compiler_params=_cparams(("arbitrary", "arbitrary")),
        name="ml_out_proj",
    )(h, p, xc, p, norm_w.reshape(1, e), skip.reshape(1, e), w, x, norm_g.reshape(1, d), mod)


def _hyena_layer(x, mod, trunk, norm_pre, norm_post, hw):
    (w_in, conv_w, conv_b, w1, b1, w2, b2, w3, b3, w4, freq, decay, bias, w_out) = hw
    batch, seq_len = trunk[0], trunk[1]
    e = w_out.shape[0]
    p = _inproj(x, norm_pre, mod, trunk, w_in, F32)
    vx, g0 = _hy_gate(p, conv_w, conv_b, batch, seq_len, e)
    kfilt = _filters(seq_len, w1, b1, w2, b2, w3, b3, w4, freq, decay)
    y = _longconv(vx, g0, kfilt, bias, batch, seq_len)
    return _outproj(y, w_out, x, norm_post, mod, trunk)


def _mlstm_layer(x, mod, trunk, norm_pre, norm_post, mw, init):
    (w_in, conv_w, conv_b, wq, wk, wv, w_gate, b_gate, norm_w, skip, w_out) = mw
    batch, seq_len = trunk[0], trunk[1]
    e = w_out.shape[0]
    n_heads = wq.shape[0]
    p = _inproj(x, norm_pre, mod, trunk, w_in, F32)
    q, k, v, xc, g, gt = _ml_qkv(p, conv_w, conv_b, wq, wk, wv, w_gate, b_gate, batch, seq_len, e)
    h, c_new, n_new, m_new = _ml_scan(q, k, v, g, gt, batch, seq_len, n_heads, init)
    x_new = _ml_out(h, p, xc, norm_w, skip, w_out, x, norm_post, mod, trunk, n_heads)
    return x_new, c_new, n_new, m_new


def kernel(x_prompt, x_sample, state_C, state_n, state_m, c, c_ctx, norm_pre, norm_post, ada_w, ada_b,
           hy_w_in, hy_conv_w, hy_conv_b, hy_ffn_w1, hy_ffn_b1, hy_ffn_w2, hy_ffn_b2, hy_ffn_w3, hy_ffn_b3,
           hy_ffn_w4, hy_sin_freq, hy_decay, hy_bias, hy_w_out, ml_w_in, ml_conv_w, ml_conv_b, ml_wq, ml_wk,
           ml_wv, ml_w_gate, ml_b_gate, ml_norm, ml_skip, ml_w_out):
    depth = norm_pre.shape[0]
    bp, lp, d = x_prompt.shape
    bs, ls, _ = x_sample.shape
    n_heads = ml_wq.shape[1]

    nrow = 8 * ((1 + bs + 7) // 8)
    conds = jnp.zeros((nrow, d), F32).at[0].set(c_ctx).at[1:1 + bs].set(c)
    mods = _ada(conds, ada_w, ada_b)

    hy_w_in_b, hy_w_out_b = hy_w_in.astype(BF16), hy_w_out.astype(BF16)
    ml_w_in_b, ml_w_out_b = ml_w_in.astype(BF16), ml_w_out.astype(BF16)

    def run_trunk(x3, cond0, per_seq, init_states):
        batch, seq_len, _ = x3.shape
        trunk = (batch, seq_len, cond0, per_seq)
        x = x3.reshape(batch * seq_len, d)
        new_c, new_n, new_m = [], [], []
        for i in range(depth):
            mod = mods[i].reshape(nrow, 1, 3 * d)
            j = i // 2
            if i % 2 == 0:
                hw = (hy_w_in_b[j], hy_conv_w[j], hy_conv_b[j], hy_ffn_w1[j], hy_ffn_b1[j], hy_ffn_w2[j],
                      hy_ffn_b2[j], hy_ffn_w3[j], hy_ffn_b3[j], hy_ffn_w4[j], hy_sin_freq[j], hy_decay[j],
                      hy_bias[j], hy_w_out_b[j])
                x = _hyena_layer(x, mod, trunk, norm_pre[i], norm_post[i], hw)
            else:
                mw = (ml_w_in_b[j], ml_conv_w[j], ml_conv_b[j], ml_wq[j], ml_wk[j], ml_wv[j], ml_w_gate[j],
                      ml_b_gate[j], ml_norm[j], ml_skip[j], ml_w_out_b[j])
                init = None if init_states is None else init_states(j)
                x, cj, nj, mj = _mlstm_layer(x, mod, trunk, norm_pre[i], norm_post[i], mw, init)
                new_c.append(cj)
                new_n.append(nj)
                new_m.append(mj)
        return x.reshape(batch, seq_len, d), new_c, new_n, new_m

    y_prompt, cs, ns, ms = run_trunk(x_prompt, 0, False, None)
    dk = ml_wq.shape[3]
    new_state_c = jnp.concatenate(cs, axis=1)
    new_state_n = jnp.stack([jnp.swapaxes(n[:, :, :, 0, :], 1, 2) for n in ns], axis=1)
    new_state_m = jnp.stack([jnp.swapaxes(m[:, :, :, 0, 0], 1, 2) for m in ms], axis=1)

    def lat_init(j):
        c0 = state_C[:, j:j + 1]
        n0 = jnp.swapaxes(state_n[:, j], 1, 2)[:, :, :, None, :]
        m0 = jnp.broadcast_to(jnp.swapaxes(state_m[:, j], 1, 2)[:, :, :, None, None],
                              (bs, n_heads, 2, 1, 128))
        return c0, n0, m0

    y_sample, _, _, _ = run_trunk(x_sample, 1, True, lat_init)
    return (y_prompt, y_sample, new_state_c, new_state_n, new_state_m)
```

```python
import functools
import math

import numpy as np
import jax
import jax.numpy as jnp
from jax import lax
from jax.experimental import pallas as pl
from jax.experimental.pallas import tpu as pltpu

F32 = jnp.float32
BF16 = jnp.bfloat16
EPS = 1e-6
N_BANDS = 16
SUBLANES = 8
SUB_TILES = 2
HALO = 16
SCAN_CHUNK = 128
FFT_N2 = 128
DIRECT_CONV_MAX_L = 512
VMEM_LIMIT = 56 * 1024 * 1024
HI = lax.Precision.HIGHEST


def _cparams(sem):
    return pltpu.CompilerParams(dimension_semantics=sem, vmem_limit_bytes=VMEM_LIMIT)


def _tile(n, pref, mult):
    if n <= pref:
        return n
    t = (pref // mult) * mult
    while t >= mult:
        if n % t == 0:
            return t
        t -= mult
    return n


def _row_tile(batch, seq_len, pref, mult):
    if seq_len >= pref:
        return _tile(seq_len, pref, mult)
    k = max(1, pref // seq_len)
    while batch % k:
        k -= 1
    return seq_len * k


def _silu(x):
    return x * jax.nn.sigmoid(x)


def _ada_kernel(c_ref, w_ref, b_ref, o_ref):
    a = _silu(c_ref[...]).astype(BF16)
    o_ref[0] = jnp.dot(a, w_ref[0].astype(BF16), preferred_element_type=F32) + b_ref[0]


def _ada(conds, ada_w, ada_b):
    depth, d, n = ada_w.shape
    rows = conds.shape[0]
    tn = _tile(n, 512, 128)
    return pl.pallas_call(
        _ada_kernel,
        grid=(depth, n // tn),
        in_specs=[pl.BlockSpec((rows, d), lambda l, j: (0, 0)),
                  pl.BlockSpec((1, d, tn), lambda l, j: (l, 0, j)),
                  pl.BlockSpec((1, 1, tn), lambda l, j: (l, 0, j))],
        out_specs=pl.BlockSpec((1, rows, tn), lambda l, j: (l, 0, j)),
        out_shape=jax.ShapeDtypeStruct((depth, rows, n), F32),
        compiler_params=_cparams(("arbitrary", "arbitrary")),
        name="ada_mod",
    )(conds, ada_w, ada_b.reshape(depth, 1, n))


def _inproj_kernel(x_ref, g_ref, shift_ref, scale_ref, w_ref, o_ref, u_scr):
    @pl.when(pl.program_id(1) == 0)
    def _():
        x = x_ref[...]
        r = lax.rsqrt(jnp.mean(x * x, axis=-1, keepdims=True) + EPS)
        u = x * r * g_ref[...]
        u_scr[...] = (u * (1.0 + scale_ref[0]) + shift_ref[0]).astype(BF16)

    o_ref[...] = jnp.dot(u_scr[...], w_ref[...], preferred_element_type=F32).astype(o_ref.dtype)


def _cond_index(trunk, tm):
    batch, seq_len, cond0, per_seq = trunk
    if per_seq:
        return lambda i: cond0 + (i * tm) // seq_len
    return lambda i: cond0


def _cond_row_tile(trunk, pref):
    batch, seq_len, _, per_seq = trunk
    if per_seq:
        return _tile(seq_len, pref, 16)
    return _tile(batch * seq_len, pref, 16)


def _inproj(x, norm_g, mod, trunk, w, out_dtype):
    t, d = x.shape
    n = w.shape[1]
    tm = _cond_row_tile(trunk, 1024)
    tn = _tile(n, 1024, 128)
    cidx = _cond_index(trunk, tm)
    return pl.pallas_call(
        _inproj_kernel,
        grid=(t // tm, n // tn),
        in_specs=[pl.BlockSpec((tm, d), lambda i, j: (i, 0)),
                  pl.BlockSpec((1, d), lambda i, j: (0, 0)),
                  pl.BlockSpec((1, 1, d), lambda i, j: (cidx(i), 0, 0)),
                  pl.BlockSpec((1, 1, d), lambda i, j: (cidx(i), 0, 1)),
                  pl.BlockSpec((d, tn), lambda i, j: (0, j))],
        out_specs=pl.BlockSpec((tm, tn), lambda i, j: (i, j)),
        out_shape=jax.ShapeDtypeStruct((t, n), out_dtype),
        scratch_shapes=[pltpu.VMEM((tm, d), BF16)],
        compiler_params=_cparams(("arbitrary", "arbitrary")),
        name="in_proj",
    )(x, norm_g.reshape(1, d), mod, mod, w)


def _outproj_epilogue(acc, x_ref, g_ref, gate_ref, o_ref):
    y = acc
    r = lax.rsqrt(jnp.mean(y * y, axis=-1, keepdims=True) + EPS)
    o_ref[...] = x_ref[...] + gate_ref[0] * (y * r * g_ref[...])


def _outproj_kernel(y_ref, w_ref, x_ref, g_ref, gate_ref, o_ref, acc):
    k = pl.program_id(1)

    @pl.when(k == 0)
    def _():
        acc[...] = jnp.zeros_like(acc)

    acc[...] += jnp.dot(y_ref[...], w_ref[...], preferred_element_type=F32)

    @pl.when(k == pl.num_programs(1) - 1)
    def _():
        _outproj_epilogue(acc[...], x_ref, g_ref, gate_ref, o_ref)


def _outproj(y, w, x, norm_g, mod, trunk):
    t, e = y.shape
    d = w.shape[1]
    tm = _cond_row_tile(trunk, 512)
    tk = _tile(e, 1024, 128)
    cidx = _cond_index(trunk, tm)
    return pl.pallas_call(
        _outproj_kernel,
        grid=(t // tm, e // tk),
        in_specs=[pl.BlockSpec((tm, tk), lambda i, k: (i, k)),
                  pl.BlockSpec((tk, d), lambda i, k: (k, 0)),
                  pl.BlockSpec((tm, d), lambda i, k: (i, 0)),
                  pl.BlockSpec((1, d), lambda i, k: (0, 0)),
                  pl.BlockSpec((1, 1, d), lambda i, k: (cidx(i), 0, 2))],
        out_specs=pl.BlockSpec((tm, d), lambda i, k: (i, 0)),
        out_shape=jax.ShapeDtypeStruct((t, d), F32),
        scratch_shapes=[pltpu.VMEM((tm, d), F32)],
        compiler_params=_cparams(("arbitrary", "arbitrary")),
        name="out_proj",
    )(y, w, x, norm_g.reshape(1, d), mod)


def _conv3(x, prev_row, next_row, w, b, row0, seq_len):
    tm = x.shape[0]
    rows = lax.broadcasted_iota(jnp.int32, x.shape, 0)
    pos = (rows + row0) % seq_len
    prev = jnp.where(rows == 0, prev_row, pltpu.roll(x, 1, 0))
    nxt = jnp.where(rows == tm - 1, next_row, pltpu.roll(x, tm - 1, 0))
    prev = jnp.where(pos == 0, 0.0, prev)
    nxt = jnp.where(pos == seq_len - 1, 0.0, nxt)
    return w[0:1] * prev + w[1:2] * x + w[2:3] * nxt + b


def _halo_specs(tm, tc, nrows, colblk):
    nh = nrows // HALO
    r = tm // HALO
    return [pl.BlockSpec((tm, tc), lambda i, j: (i, colblk(j))),
            pl.BlockSpec((HALO, tc), lambda i, j: (jnp.maximum(i * r - 1, 0), colblk(j))),
            pl.BlockSpec((HALO, tc), lambda i, j: (jnp.minimum((i + 1) * r, nh - 1), colblk(j)))]


def _hy_gate_kernel(seq_len, x0, x0p, x0n, x1, x1p, x1n, v, vp, vn, z, cw, cb, vx_ref, g0_ref):
    tm = x0.shape[0]
    row0 = pl.program_id(0) * tm

    def conv(m, p, n, k):
        return _conv3(m[...].astype(F32), p[HALO - 1:HALO, :].astype(F32), n[0:1, :].astype(F32),
                      cw[k], cb[k], row0, seq_len)

    x0c = conv(x0, x0p, x0n, 0)
    x1c = conv(x1, x1p, x1n, 1)
    vc = conv(v, vp, vn, 2)
    vx_ref[...] = (vc * x1c).astype(vx_ref.dtype)
    g0_ref[...] = (x0c * _silu(z[...].astype(F32))).astype(g0_ref.dtype)


def _hy_gate(p, conv_w, conv_b, batch, seq_len, e):
    t = p.shape[0]
    tm = _row_tile(batch, seq_len, 256, HALO)
    tc = _tile(e, 1024, 128)
    nb = e // tc
    specs = []
    for part in range(3):
        specs += _halo_specs(tm, tc, t, lambda j, part=part: part * nb + j)
    specs += [pl.BlockSpec((tm, tc), lambda i, j: (i, 3 * nb + j)),
              pl.BlockSpec((3, 3, tc), lambda i, j: (0, 0, j)),
              pl.BlockSpec((3, 1, tc), lambda i, j: (0, 0, j))]
    cw = conv_w.reshape(3, 3, e).transpose(1, 0, 2)
    cb = conv_b.reshape(3, 1, e)
    return pl.pallas_call(
        functools.partial(_hy_gate_kernel, seq_len),
        grid=(t // tm, nb),
        in_specs=specs,
        out_specs=[pl.BlockSpec((tm, tc), lambda i, j: (i, j))] * 2,
        out_shape=[jax.ShapeDtypeStruct((t, e), BF16), jax.ShapeDtypeStruct((t, e), BF16)],
        compiler_params=_cparams(("arbitrary", "arbitrary")),
        name="hy_gate",
    )(p, p, p, p, p, p, p, p, p, p, cw, cb)


def _filter_tables(seq_len):
    l = seq_len
    t = np.linspace(0.0, 1.0, l)
    w = 2.0 * math.pi * np.arange(l) / l
    f = np.linspace(1e-4, N_BANDS - 1, N_BANDS)
    z = np.concatenate([t[:, None], np.cos(f[None] * w[:, None]), -np.sin(f[None] * w[:, None])], axis=-1)
    pos = np.concatenate([np.arange(l), [0], np.arange(l - 1, 0, -1)])
    z2 = np.zeros((2 * l, 128), np.float32)
    z2[:, :z.shape[1]] = z[pos]
    aux = np.zeros((2 * l, 128), np.float32)
    aux[:, 0] = t[pos]
    aux[:, 1] = 1.0
    aux[l, 1] = 0.0
    return jnp.asarray(z2), jnp.asarray(aux)


def _filter_kernel(z_ref, aux_ref, w1, b1, w2, b2, w3, b3, fr, w4, dec, o_ref):
    f = fr[...]
    h = jnp.sin(f * (jnp.dot(z_ref[...], w1[...], precision=HI, preferred_element_type=F32) + b1[...]))
    h = jnp.sin(f * (jnp.dot(h, w2[...], precision=HI, preferred_element_type=F32) + b2[...]))
    h = jnp.sin(f * (jnp.dot(h, w3[...], precision=HI, preferred_element_type=F32) + b3[...]))
    k = jnp.dot(h.astype(BF16), w4[...].astype(BF16), preferred_element_type=F32)
    t = aux_ref[:, 0:1]
    keep = aux_ref[:, 1:2]
    o_ref[...] = k * jnp.exp(-t * jnp.abs(dec[0])) * keep


def _filters(seq_len, w1, b1, w2, b2, w3, b3, w4, freq, decay):
    e = decay.shape[-1]
    fo = w2.shape[0]
    z2, aux = _filter_tables(seq_len)
    w1p = jnp.zeros((128, fo), F32).at[:w1.shape[0]].set(w1)
    tr = _tile(seq_len, 512, 8)
    tc = _tile(e, 2048, 128)
    nb = e // tc
    nr = seq_len // tr
    small = lambda a: pl.BlockSpec(a.shape, lambda i, j: (0,) * a.ndim)
    b1r, b2r, b3r, frr = (a.reshape(1, fo) for a in (b1, b2, b3, freq))
    return pl.pallas_call(
        _filter_kernel,
        grid=(2 * nr, nb),
        in_specs=[pl.BlockSpec((tr, 128), lambda i, j: (i, 0)),
                  pl.BlockSpec((tr, 128), lambda i, j: (i, 0)),
                  small(w1p), small(b1r), small(w2), small(b2r), small(w3), small(b3r), small(frr),
                  pl.BlockSpec((fo, tc), lambda i, j: (0, (i // nr) * nb + j)),
                  pl.BlockSpec((1, 1, tc), lambda i, j: (i // nr, 0, j))],
        out_specs=pl.BlockSpec((tr, tc), lambda i, j: (i, j)),
        out_shape=jax.ShapeDtypeStruct((2 * seq_len, e), F32),
        compiler_params=_cparams(("arbitrary", "arbitrary")),
        name="hy_filters",
    )(z2, aux, w1p, b1r, w2, b2r, w3, b3r, frr, w4, decay.reshape(2, 1, e))


def _cmm_kernel(w_ref, x_ref, o_ref):
    o_ref[0] = jnp.dot(w_ref[...], x_ref[0], preferred_element_type=F32).astype(o_ref.dtype)


def _cmm(w, x, out_dtype):
    m, k = w.shape
    b, _, n = x.shape
    tn = _tile(n, 4096, 128)
    return pl.pallas_call(
        _cmm_kernel,
        grid=(b, n // tn),
        in_specs=[pl.BlockSpec((m, k), lambda bi, j: (0, 0)),
                  pl.BlockSpec((1, k, tn), lambda bi, j: (bi, 0, j))],
        out_specs=pl.BlockSpec((1, m, tn), lambda bi, j: (bi, 0, j)),
        out_shape=jax.ShapeDtypeStruct((b, m, n), out_dtype),
        compiler_params=_cparams(("arbitrary", "arbitrary")),
        name="const_lhs_matmul",
    )(w, x)


def _direct_mats(seq_len):
    l = seq_len
    n = 2 * l
    f = np.arange(l)[:, None]
    t = np.arange(n)[None, :]
    ang = 2.0 * math.pi * ((f * t) % n) / n
    fwd = np.concatenate([np.cos(ang), -np.sin(ang)], axis=0)
    fwd[l] = np.cos(math.pi * np.arange(n))
    tt = np.arange(l)[:, None]
    ff = np.arange(l)[None, :]
    ang2 = 2.0 * math.pi * ((tt * ff) % n) / n
    wgt = np.where(ff == 0, 1.0, 2.0) / n
    inv = np.concatenate([wgt * np.cos(ang2), -wgt * np.sin(ang2)], axis=1)
    inv[:, l] = np.cos(math.pi * np.arange(l)) / n
    return fwd, inv


def _direct_conv_kernel(seq_len, f_ref, v_ref, x_ref, k_ref, g0_ref, bias_ref, o_ref):
    l = seq_len
    x = x_ref[0]
    s = jnp.dot(f_ref[...], x, preferred_element_type=F32)
    sre, sim = s[:l], s[l:]
    kre, kim = k_ref[:l, :], k_ref[l:, :]
    row0 = lax.broadcasted_iota(jnp.int32, sre.shape, 0) == 0
    yre = sre * kre - jnp.where(row0, 0.0, sim * kim)
    yim = jnp.where(row0, sim * kim, sre * kim + sim * kre)
    y = jnp.concatenate([yre, yim], axis=0).astype(BF16)
    out = jnp.dot(v_ref[...], y, preferred_element_type=F32)
    out = out + x.astype(F32) * bias_ref[...]
    o_ref[0] = (out * g0_ref[0].astype(F32)).astype(o_ref.dtype)


def _longconv_direct(vx, g0, kfilt, bias, batch, seq_len):
    e = vx.shape[-1]
    l = seq_len
    fwd, inv = _direct_mats(l)
    f_full = jnp.asarray(fwd, BF16)
    f_data = jnp.asarray(fwd[:, :l], BF16)
    v_mat = jnp.asarray(inv, BF16)
    kspec = _cmm(f_full, kfilt.astype(BF16)[None], F32)[0]
    tc = _tile(e, 1024, 128)
    out = pl.pallas_call(
        functools.partial(_direct_conv_kernel, l),
        grid=(e // tc, batch),
        in_specs=[pl.BlockSpec((2 * l, l), lambda j, b: (0, 0)),
                  pl.BlockSpec((l, 2 * l), lambda j, b: (0, 0)),
                  pl.BlockSpec((1, l, tc), lambda j, b: (b, 0, j)),
                  pl.BlockSpec((2 * l, tc), lambda j, b: (0, j)),
                  pl.BlockSpec((1, l, tc), lambda j, b: (b, 0, j)),
                  pl.BlockSpec((1, tc), lambda j, b: (0, j))],
        out_specs=pl.BlockSpec((1, l, tc), lambda j, b: (b, 0, j)),
        out_shape=jax.ShapeDtypeStruct((batch, l, e), BF16),
        compiler_params=_cparams(("arbitrary", "arbitrary")),
        name="longconv_direct",
    )(f_data, v_mat, vx.reshape(batch, l, e), kspec, g0.reshape(batch, l, e), bias.reshape(1, e))
    return out.reshape(batch * l, e)


def _two_level_mats(seq_len):
    n = 2 * seq_len
    n2 = FFT_N2
    n1 = n // n2
    h2 = n2 // 2
    ns = 8 * ((h2 + 1 + 7) // 8)
    f2 = np.arange(h2 + 1)[:, None]
    t2 = np.arange(n2)[None, :]
    ang = 2.0 * math.pi * ((f2 * t2) % n2) / n2
    f1m = np.zeros((2 * ns, n2))
    f1m[0:2 * (h2 + 1):2] = np.cos(ang)
    f1m[1:2 * (h2 + 1):2] = -np.sin(ang)
    wgt = np.where((f2 == 0) | (f2 == h2), 1.0, 2.0) / n
    g1m = np.zeros((h2, 2 * ns))
    g1m[:, 0:2 * (h2 + 1):2] = (wgt * np.cos(ang[:, :h2])).T
    g1m[:, 1:2 * (h2 + 1):2] = -(wgt * np.sin(ang[:, :h2])).T
    t1 = np.arange(n1)[None, :]
    f1 = np.arange(n1)[:, None]
    rm = np.zeros((ns, 2 * n1, 2 * n1))
    pm = np.zeros((ns, 2 * n1, 2 * n1))
    for s in range(h2 + 1):
        a = 2.0 * math.pi * ((t1 * (n2 * f1 + s)) % n) / n
        mr, mi = np.cos(a), -np.sin(a)
        rm[s] = np.block([[mr, -mi], [mi, mr]])
        pr, pi = mr.T, -mi.T
        pm[s] = np.block([[pr, -pi], [pi, pr]])
    return n1, n2, ns, f1m, g1m, rm, pm


def _slab_spec_kernel(r_ref, a_ref, o_ref):
    n1 = a_ref.shape[3]
    a = jnp.concatenate([a_ref[0, 0, 0], a_ref[0, 0, 1]], axis=0)
    s = jnp.dot(r_ref[0], a, preferred_element_type=F32)
    o_ref[0, 0] = s[:n1]
    o_ref[0, 1] = s[n1:]


def _slab_conv_kernel(r_ref, p_ref, a_ref, k_ref, o_ref):
    n1 = a_ref.shape[3]
    a = jnp.concatenate([a_ref[0, 0, 0], a_ref[0, 0, 1]], axis=0)
    s = jnp.dot(r_ref[0], a, preferred_element_type=F32)
    sre, sim = s[:n1], s[n1:]
    kre, kim = k_ref[0, 0], k_ref[0, 1]
    y = jnp.concatenate([sre * kre - sim * kim, sre * kim + sim * kre], axis=0).astype(BF16)
    o = jnp.dot(p_ref[0], y, preferred_element_type=F32)
    o_ref[0, 0, 0] = o[:n1].astype(o_ref.dtype)
    o_ref[0, 0, 1] = o[n1:].astype(o_ref.dtype)


def _dft_l1_kernel(k_ref, x_ref, o_ref):
    x = x_ref[0].astype(F32)
    nt2, _, tc = x.shape
    halves = []
    for g in range(SUB_TILES):
        xg = x[:, SUBLANES * g:SUBLANES * (g + 1), :].reshape(nt2 * SUBLANES, tc).astype(BF16)
        a = jnp.dot(k_ref[...], xg, preferred_element_type=F32)
        halves.append(a.reshape(-1, SUBLANES, tc))
    o_ref[0] = jnp.concatenate(halves, axis=1).astype(o_ref.dtype)


def _dft_l1_inv_kernel(g_ref, a_ref, vx_ref, g0_ref, bias_ref, o_ref):
    a = a_ref[0].astype(F32)
    nr, _, tc = a.shape
    halves = []
    for g in range(SUB_TILES):
        ag = a[:, SUBLANES * g:SUBLANES * (g + 1), :].reshape(nr * SUBLANES, tc).astype(BF16)
        y = jnp.dot(g_ref[...], ag, preferred_element_type=F32)
        halves.append(y.reshape(-1, SUBLANES, tc))
    y = jnp.concatenate(halves, axis=1)
    y = y + vx_ref[0].astype(F32) * bias_ref[...]
    o_ref[0] = (y * g0_ref[0].astype(F32)).astype(o_ref.dtype)


def _dft_l1(kmat, x4, out_dtype):
    b, nt2, n1, e = x4.shape
    m = kmat.shape[0] // SUBLANES
    tc = _tile(e, 1024, 128)
    rows = SUBLANES * SUB_TILES
    return pl.pallas_call(
        _dft_l1_kernel,
        grid=(b, n1 // rows, e // tc),
        in_specs=[pl.BlockSpec(kmat.shape, lambda bi, i, j: (0, 0)),
                  pl.BlockSpec((1, nt2, rows, tc), lambda bi, i, j: (bi, 0, i, j))],
        out_specs=pl.BlockSpec((1, m, rows, tc), lambda bi, i, j: (bi, 0, i, j)),
        out_shape=jax.ShapeDtypeStruct((b, m, n1, e), out_dtype),
        compiler_params=_cparams(("arbitrary", "arbitrary", "arbitrary")),
        name="dft_level1",
    )(kmat, x4)


def _longconv_two_level(vx, g0, kfilt, bias, batch, seq_len):
    e = vx.shape[-1]
    l = seq_len
    n1, n2, ns, f1m, g1m, rm, pm = _two_level_mats(l)
    h2 = n2 // 2
    eye = np.eye(SUBLANES)
    k_full = jnp.asarray(np.kron(f1m, eye), BF16)
    k_data = jnp.asarray(np.kron(f1m[:, :h2], eye), BF16)
    g8 = jnp.asarray(np.kron(g1m, eye), BF16)
    rmat = jnp.asarray(rm, BF16)
    pmat = jnp.asarray(pm, BF16)
    tc = _tile(e, 4096, 128)
    nc = e // tc

    ka = _dft_l1(k_full, kfilt.reshape(1, n2, n1, e), BF16).reshape(1, ns, 2, n1, e)
    kspec = pl.pallas_call(
        _slab_spec_kernel,
        grid=(ns, nc),
        in_specs=[pl.BlockSpec((1, 2 * n1, 2 * n1), lambda s, j: (s, 0, 0)),
                  pl.BlockSpec((1, 1, 2, n1, tc), lambda s, j: (0, s, 0, 0, j))],
        out_specs=pl.BlockSpec((1, 2, n1, tc), lambda s, j: (s, 0, 0, j)),
        out_shape=jax.ShapeDtypeStruct((ns, 2, n1, e), F32),
        compiler_params=_cparams(("arbitrary", "arbitrary")),
        name="filter_slab_dft",
    )(rmat, ka)

    vx4 = vx.reshape(batch, h2, n1, e)
    a = _dft_l1(k_data, vx4, BF16).reshape(batch, ns, 2, n1, e)
    a2 = pl.pallas_call(
        _slab_conv_kernel,
        grid=(ns, batch, nc),
        in_specs=[pl.BlockSpec((1, 2 * n1, 2 * n1), lambda s, b, j: (s, 0, 0)),
                  pl.BlockSpec((1, 2 * n1, 2 * n1), lambda s, b, j: (s, 0, 0)),
                  pl.BlockSpec((1, 1, 2, n1, tc), lambda s, b, j: (b, s, 0, 0, j)),
                  pl.BlockSpec((1, 2, n1, tc), lambda s, b, j: (s, 0, 0, j))],
        out_specs=pl.BlockSpec((1, 1, 2, n1, tc), lambda s, b, j: (b, s, 0, 0, j)),
        out_shape=jax.ShapeDtypeStruct((batch, ns, 2, n1, e), BF16),
        compiler_params=_cparams(("arbitrary", "arbitrary", "arbitrary")),
        name="slab_conv",
    )(rmat, pmat, a, kspec)
    a2 = a2.reshape(batch, 2 * ns, n1, e)
    rows = SUBLANES * SUB_TILES
    tci = _tile(e, 1024, 128)
    blk = lambda r: pl.BlockSpec((1, r, rows, tci), lambda bi, i, j: (bi, 0, i, j))
    out = pl.pallas_call(
        _dft_l1_inv_kernel,
        grid=(batch, n1 // rows, e // tci),
        in_specs=[pl.BlockSpec(g8.shape, lambda bi, i, j: (0, 0)),
                  blk(2 * ns), blk(h2), blk(h2),
                  pl.BlockSpec((1, 1, tci), lambda bi, i, j: (0, 0, j))],
        out_specs=blk(h2),
        out_shape=jax.ShapeDtypeStruct((batch, h2, n1, e), BF16),
        compiler_params=_cparams(("arbitrary", "arbitrary", "arbitrary")),
        name="dft_level1_inverse",
    )(g8, a2, vx4, g0.reshape(batch, h2, n1, e), bias.reshape(1, 1, e))
    return out.reshape(batch * l, e)


def _longconv(vx, g0, kfilt, bias, batch, seq_len):
    if seq_len <= DIRECT_CONV_MAX_L:
        return _longconv_direct(vx, g0, kfilt, bias, batch, seq_len)
    return _longconv_two_level(vx, g0, kfilt, bias, batch, seq_len)


def _ml_qkv_kernel(seq_len, n_heads, dk, x_ref, xp_ref, xn_ref, cw_ref, cb_ref, wq_ref, wk_ref, wv_ref,
                   wgq_ref, wgk_ref, wgv_ref, bg_ref, q_ref, k_ref, v_ref, xc_ref, g_ref, gt_ref, gacc):
    h = pl.program_id(1)
    tm = x_ref.shape[0]
    x = x_ref[...].astype(F32)
    conv = _conv3(x, xp_ref[HALO - 1:HALO, :].astype(F32), xn_ref[0:1, :].astype(F32),
                  cw_ref[...], cb_ref[...], pl.program_id(0) * tm, seq_len)
    xc = _silu(conv)
    xc_ref[...] = xc.astype(xc_ref.dtype)
    xcb = xc.astype(BF16)
    q = jnp.dot(xcb, wq_ref[0], preferred_element_type=F32).astype(BF16)
    k = (jnp.dot(xcb, wk_ref[0], preferred_element_type=F32) * (dk ** -0.5)).astype(BF16)
    v = jnp.dot(x.astype(BF16), wv_ref[0], preferred_element_type=F32).astype(BF16)
    q_ref[...] = q
    k_ref[...] = k
    v_ref[...] = v
    part = (jnp.dot(q, wgq_ref[0], preferred_element_type=F32)
            + jnp.dot(k, wgk_ref[0], preferred_element_type=F32)
            + jnp.dot(v, wgv_ref[0], preferred_element_type=F32))

    @pl.when(h == 0)
    def _():
        gacc[...] = part + bg_ref[...]

    @pl.when(h > 0)
    def _():
        gacc[...] += part

    @pl.when(h == n_heads - 1)
    def _():
        g = gacc[...]
        col = lax.broadcasted_iota(jnp.int32, g.shape, 1)
        is_forget = (col % (2 * n_heads)) >= n_heads
        g = jnp.where(is_forget, jax.nn.log_sigmoid(g), g)
        g_ref[...] = g
        q_len = gt_ref.shape[2]
        for c in range(tm // q_len):
            gt_ref[c] = g[c * q_len:(c + 1) * q_len, :].T


def _ml_qkv(p, conv_w, conv_b, wq, wk, wv, w_gate, b_gate, batch, seq_len, e):
    t = p.shape[0]
    n_heads, dh, dk = wq.shape
    dv = wv.shape[2]
    q_len = min(SCAN_CHUNK, seq_len)
    tm = _row_tile(batch, seq_len, 512, q_len)
    ng = w_gate.shape[1]
    gpad = 128
    wgq = jnp.zeros((n_heads, dk, gpad), BF16).at[:, :, :ng].set(
        w_gate[:n_heads * dk].reshape(n_heads, dk, ng).astype(BF16))
    wgk = jnp.zeros((n_heads, dk, gpad), BF16).at[:, :, :ng].set(
        w_gate[n_heads * dk:2 * n_heads * dk].reshape(n_heads, dk, ng).astype(BF16))
    wgv = jnp.zeros((n_heads, dv, gpad), BF16).at[:, :, :ng].set(
        w_gate[2 * n_heads * dk:].reshape(n_heads, dv, ng).astype(BF16))
    bg = jnp.zeros((1, gpad), F32).at[0, :ng].set(b_gate)
    specs = _halo_specs(tm, dh, t, lambda j: j)
    specs += [pl.BlockSpec((3, dh), lambda i, j: (0, j)),
              pl.BlockSpec((1, dh), lambda i, j: (0, j)),
              pl.BlockSpec((1, dh, dk), lambda i, j: (j, 0, 0)),
              pl.BlockSpec((1, dh, dk), lambda i, j: (j, 0, 0)),
              pl.BlockSpec((1, dh, dv), lambda i, j: (j, 0, 0)),
              pl.BlockSpec((1, dk, gpad), lambda i, j: (j, 0, 0)),
              pl.BlockSpec((1, dk, gpad), lambda i, j: (j, 0, 0)),
              pl.BlockSpec((1, dv, gpad), lambda i, j: (j, 0, 0)),
              pl.BlockSpec((1, gpad), lambda i, j: (0, 0))]
    return pl.pallas_call(
        functools.partial(_ml_qkv_kernel, seq_len, n_heads, dk),
        grid=(t // tm, n_heads),
        in_specs=specs,
        out_specs=[pl.BlockSpec((tm, dk), lambda i, j: (i, j)),
                   pl.BlockSpec((tm, dk), lambda i, j: (i, j)),
                   pl.BlockSpec((tm, dv), lambda i, j: (i, j)),
                   pl.BlockSpec((tm, dh), lambda i, j: (i, j)),
                   pl.BlockSpec((tm, gpad), lambda i, j: (i, 0)),
                   pl.BlockSpec((tm // q_len, gpad, q_len), lambda i, j: (i, 0, 0))],
        out_shape=[jax.ShapeDtypeStruct((t, n_heads * dk), BF16),
                   jax.ShapeDtypeStruct((t, n_heads * dk), BF16),
                   jax.ShapeDtypeStruct((t, n_heads * dv), BF16),
                   jax.ShapeDtypeStruct((t, e), BF16),
                   jax.ShapeDtypeStruct((t, gpad), F32),
                   jax.ShapeDtypeStruct((t // q_len, gpad, q_len), F32)],
        scratch_shapes=[pltpu.VMEM((tm, gpad), F32)],
        compiler_params=_cparams(("arbitrary", "arbitrary")),
        name="ml_qkv_gates",
    )(p, p, p, conv_w, conv_b.reshape(1, e), wq.astype(BF16), wk.astype(BF16), wv.astype(BF16),
      wgq, wgk, wgv, bg)


def _ml_scan_kernel(n_heads, has_init, *refs):
    if has_init:
        (q_ref, k_ref, v_ref, g_ref, gt_ref, c0_ref, n0_ref, m0_ref,
         h_ref, cout_ref, nout_ref, mout_ref, c_scr, n_scr, m_scr) = refs
    else:
        (q_ref, k_ref, v_ref, g_ref, gt_ref,
         h_ref, cout_ref, nout_ref, mout_ref, c_scr, n_scr, m_scr) = refs
    head = pl.program_id(1)
    seq_len = q_ref.shape[0]
    q_len = gt_ref.shape[2]
    n_chunks = seq_len // q_len

    for d in range(2):
        if has_init:
            c_scr[d] = c0_ref[0, 0, d, 0]
            n_scr[d] = n0_ref[0, 0, d]
            m_scr[d] = m0_ref[0, 0, d]
        else:
            c_scr[d] = jnp.zeros(c_scr.shape[1:], F32)
            n_scr[d] = jnp.zeros(n_scr.shape[1:], F32)
            m_scr[d] = jnp.zeros(m_scr.shape[1:], F32)
    h_ref[...] = jnp.zeros_like(h_ref)

    row_i = lax.broadcasted_iota(jnp.int32, (q_len, q_len), 0)
    col_i = lax.broadcasted_iota(jnp.int32, (q_len, q_len), 1)
    lane_g = lax.broadcasted_iota(jnp.int32, (q_len, g_ref.shape[1]), 1)

    def direction(d, c):
        icol = d * 2 * n_heads + head
        fcol = icol + n_heads
        rows = pl.ds(pl.multiple_of(c * q_len, q_len), q_len)
        g = g_ref[rows, :]
        i_col = jnp.sum(jnp.where(lane_g == icol, g, 0.0), axis=1, keepdims=True)
        lf_col = jnp.sum(jnp.where(lane_g == fcol, g, 0.0), axis=1, keepdims=True)
        i_row = gt_ref[c, pl.ds(icol, 1), :]
        lf_row = gt_ref[c, pl.ds(fcol, 1), :]
        mask = (col_i <= row_i) if d == 0 else (col_i >= row_i)
        maskt = (row_i <= col_i) if d == 0 else (row_i >= col_i)
        b_col = jnp.sum(jnp.where(mask, lf_row, 0.0), axis=1, keepdims=True)
        b_row = jnp.sum(jnp.where(maskt, lf_col, 0.0), axis=0, keepdims=True)
        g_tot = jnp.sum(lf_row, axis=1, keepdims=True)
        m_prev = m_scr[d][:, 0:1]
        n_prev = n_scr[d]
        c_prev = c_scr[d]

        qc = q_ref[rows, :]
        kc = k_ref[rows, :]
        vc = v_ref[rows, :]
        dmat = jnp.where(mask, b_col - b_row + i_row, -jnp.inf)
        inter = b_col + m_prev
        m_t = jnp.maximum(inter, jnp.max(dmat, axis=1, keepdims=True))
        qk = lax.dot_general(qc, kc, (((1,), (1,)), ((), ())), preferred_element_type=F32)
        s = qk * jnp.exp(dmat - m_t)
        w_inter = jnp.exp(inter - m_t)
        num = (w_inter * jnp.dot(qc, c_prev.astype(BF16), preferred_element_type=F32)
               + jnp.dot(s.astype(BF16), vc, preferred_element_type=F32))
        qn = (w_inter * jnp.sum(qc.astype(F32) * n_prev, axis=1, keepdims=True)
              + jnp.sum(s, axis=1, keepdims=True))
        den = jnp.maximum(jnp.abs(qn), jnp.exp(-m_t))
        h_ref[rows, :] += num / den

        a_col = g_tot - b_col + i_col
        m_new = jnp.maximum(g_tot + m_prev, jnp.max(a_col, axis=0, keepdims=True))
        wa = jnp.exp(a_col - m_new)
        dec = jnp.exp(g_tot + m_prev - m_new)
        kw = kc.astype(F32) * wa
        c_scr[d] = dec * c_prev + lax.dot_general(kw.astype(BF16), vc, (((0,), (0,)), ((), ())),
                                                  preferred_element_type=F32)
        n_scr[d] = dec * n_prev + jnp.sum(kw, axis=0, keepdims=True)
        m_scr[d] = jnp.broadcast_to(m_new, m_scr.shape[1:])

    def body(j, carry):
        direction(0, j)
        direction(1, n_chunks - 1 - j)
        return carry

    lax.fori_loop(0, n_chunks, body, 0)
    for d in range(2):
        cout_ref[0, 0, d, 0] = c_scr[d]
        nout_ref[0, 0, d] = n_scr[d]
        mout_ref[0, 0, d] = m_scr[d]


def _ml_scan(q, k, v, g, gt, batch, seq_len, n_heads, init=None):
    dk = q.shape[1] // n_heads
    dv = v.shape[1] // n_heads
    q_len = gt.shape[2]
    gpad = g.shape[1]
    nq = seq_len // q_len
    in_specs = [pl.BlockSpec((seq_len, dk), lambda b, h: (b, h)),
                pl.BlockSpec((seq_len, dk), lambda b, h: (b, h)),
                pl.BlockSpec((seq_len, dv), lambda b, h: (b, h)),
                pl.BlockSpec((seq_len, gpad), lambda b, h: (b, 0)),
                pl.BlockSpec((nq, gpad, q_len), lambda b, h: (b, 0, 0))]
    args = [q, k, v, g, gt]
    if init is not None:
        c0, n0, m0 = init
        in_specs += [pl.BlockSpec((1, 1, 2, 1, dk, dv), lambda b, h: (b, 0, 0, h, 0, 0)),
                     pl.BlockSpec((1, 1, 2, 1, dk), lambda b, h: (b, h, 0, 0, 0)),
                     pl.BlockSpec((1, 1, 2, 1, 128), lambda b, h: (b, h, 0, 0, 0))]
        args += [c0, n0, m0]
    out = pl.pallas_call(
        functools.partial(_ml_scan_kernel, n_heads, init is not None),
        grid=(batch, n_heads),
        in_specs=in_specs,
        out_specs=[pl.BlockSpec((seq_len, dv), lambda b, h: (b, h)),
                   pl.BlockSpec((1, 1, 2, 1, dk, dv), lambda b, h: (b, 0, 0, h, 0, 0)),
                   pl.BlockSpec((1, 1, 2, 1, dk), lambda b, h: (b, h, 0, 0, 0)),
                   pl.BlockSpec((1, 1, 2, 1, 128), lambda b, h: (b, h, 0, 0, 0))],
        out_shape=[jax.ShapeDtypeStruct((batch * seq_len, n_heads * dv), F32),
                   jax.ShapeDtypeStruct((batch, 1, 2, n_heads, dk, dv), F32),
                   jax.ShapeDtypeStruct((batch, n_heads, 2, 1, dk), F32),
                   jax.ShapeDtypeStruct((batch, n_heads, 2, 1, 128), F32)],
        scratch_shapes=[pltpu.VMEM((2, dk, dv), F32), pltpu.VMEM((2, 1, dk), F32),
                        pltpu.VMEM((2, 1, 128), F32)],
        compiler_params=_cparams(("arbitrary", "arbitrary")),
        name="ml_scan",
    )(*args)
    return out


def _ml_out_kernel(h_ref, o_ref_in, xc_ref, z_ref, nw_ref, sk_ref, w_ref, x_ref, g_ref, gate_ref, out_ref, acc):
    k = pl.program_id(1)

    @pl.when(k == 0)
    def _():
        acc[...] = jnp.zeros_like(acc)

    hh = jax.nn.sigmoid(o_ref_in[...].astype(F32)) * h_ref[...]
    mu = jnp.mean(hh, axis=-1, keepdims=True)
    var = jnp.mean(jnp.square(hh - mu), axis=-1, keepdims=True)
    hn = (hh - mu) * lax.rsqrt(var + EPS) * nw_ref[...]
    hn = hn + sk_ref[...] * xc_ref[...].astype(F32)
    y = (hn * _silu(z_ref[...].astype(F32))).astype(BF16)
    acc[...] += jnp.dot(y, w_ref[...], preferred_element_type=F32)

    @pl.when(k == pl.num_programs(1) - 1)
    def _():
        _outproj_epilogue(acc[...], x_ref, g_ref, gate_ref, out_ref)


def _ml_out(h, p, xc, norm_w, skip, w, x, norm_g, mod, trunk, n_heads):
    t, e = h.shape
    d = w.shape[1]
    dv = e // n_heads
    tm = _cond_row_tile(trunk, 512)
    cidx = _cond_index(trunk, tm)
    return pl.pallas_call(
        _ml_out_kernel,
        grid=(t // tm, n_heads),
        in_specs=[pl.BlockSpec((tm, dv), lambda i, k: (i, k)),
                  pl.BlockSpec((tm, dv), lambda i, k: (i, 2 * n_heads + k)),
                  pl.BlockSpec((tm, dv), lambda i, k: (i, k)),
                  pl.BlockSpec((tm, dv), lambda i, k: (i, n_heads + k)),
                  pl.BlockSpec((1, dv), lambda i, k: (0, k)),
                  pl.BlockSpec((1, dv), lambda i, k: (0, k)),
                  pl.BlockSpec((dv, d), lambda i, k: (k, 0)),
                  pl.BlockSpec((tm, d), lambda i, k: (i, 0)),
                  pl.BlockSpec((1, d), lambda i, k: (0, 0)),
                  pl.BlockSpec((1, 1, d), lambda i, k: (cidx(i), 0, 2))],
        out_specs=pl.BlockSpec((tm, d), lambda i, k: (i, 0)),
        out_shape=jax.ShapeDtypeStruct((t, d), F32),
        scratch_shapes=[pltpu.VMEM((tm, d), F32)],
        compiler_params=_cparams(("arbitrary", "arbitrary")),
        name="ml_out_proj",
    )(h, p, xc, p, norm_w.reshape(1, e), skip.reshape(1, e), w, x, norm_g.reshape(1, d), mod)


def _hyena_layer(x, mod, trunk, norm_pre, norm_post, hw):
    (w_in, conv_w, conv_b, w1, b1, w2, b2, w3, b3, w4, freq, decay, bias, w_out) = hw
    batch, seq_len = trunk[0], trunk[1]
    e = w_out.shape[0]
    p = _inproj(x, norm_pre, mod, trunk, w_in, BF16)
    vx, g0 = _hy_gate(p, conv_w, conv_b, batch, seq_len, e)
    kfilt = _filters(seq_len, w1, b1, w2, b2, w3, b3, w4, freq, decay)
    y = _longconv(vx, g0, kfilt, bias, batch, seq_len)
    return _outproj(y, w_out, x, norm_post, mod, trunk)


def _mlstm_layer(x, mod, trunk, norm_pre, norm_post, mw, init):
    (w_in, conv_w, conv_b, wq, wk, wv, w_gate, b_gate, norm_w, skip, w_out) = mw
    batch, seq_len = trunk[0], trunk[1]
    e = w_out.shape[0]
    n_heads = wq.shape[0]
    p = _inproj(x, norm_pre, mod, trunk, w_in, BF16)
    q, k, v, xc, g, gt = _ml_qkv(p, conv_w, conv_b, wq, wk, wv, w_gate, b_gate, batch, seq_len, e)
    h, c_new, n_new, m_new = _ml_scan(q, k, v, g, gt, batch, seq_len, n_heads, init)
    x_new = _ml_out(h, p, xc, norm_w, skip, w_out, x, norm_post, mod, trunk, n_heads)
    return x_new, c_new, n_new, m_new


def kernel(x_prompt, x_sample, state_C, state_n, state_m, c, c_ctx, norm_pre, norm_post, ada_w, ada_b,
           hy_w_in, hy_conv_w, hy_conv_b, hy_ffn_w1, hy_ffn_b1, hy_ffn_w2, hy_ffn_b2, hy_ffn_w3, hy_ffn_b3,
           hy_ffn_w4, hy_sin_freq, hy_decay, hy_bias, hy_w_out, ml_w_in, ml_conv_w, ml_conv_b, ml_wq, ml_wk,
           ml_wv, ml_w_gate, ml_b_gate, ml_norm, ml_skip, ml_w_out):
    depth = norm_pre.shape[0]
    bp, lp, d = x_prompt.shape
    bs, ls, _ = x_sample.shape
    n_heads = ml_wq.shape[1]

    nrow = 8 * ((1 + bs + 7) // 8)
    conds = jnp.zeros((nrow, d), F32).at[0].set(c_ctx).at[1:1 + bs].set(c)
    mods = _ada(conds, ada_w, ada_b)

    hy_w_in_b, hy_w_out_b = hy_w_in.astype(BF16), hy_w_out.astype(BF16)
    ml_w_in_b, ml_w_out_b = ml_w_in.astype(BF16), ml_w_out.astype(BF16)

    def run_trunk(x3, cond0, per_seq, init_states):
        batch, seq_len, _ = x3.shape
        trunk = (batch, seq_len, cond0, per_seq)
        x = x3.reshape(batch * seq_len, d)
        new_c, new_n, new_m = [], [], []
        for i in range(depth):
            mod = mods[i].reshape(nrow, 1, 3 * d)
            j = i // 2
            if i % 2 == 0:
                hw = (hy_w_in_b[j], hy_conv_w[j], hy_conv_b[j], hy_ffn_w1[j], hy_ffn_b1[j], hy_ffn_w2[j],
                      hy_ffn_b2[j], hy_ffn_w3[j], hy_ffn_b3[j], hy_ffn_w4[j], hy_sin_freq[j], hy_decay[j],
                      hy_bias[j], hy_w_out_b[j])
                x = _hyena_layer(x, mod, trunk, norm_pre[i], norm_post[i], hw)
            else:
                mw = (ml_w_in_b[j], ml_conv_w[j], ml_conv_b[j], ml_wq[j], ml_wk[j], ml_wv[j], ml_w_gate[j],
                      ml_b_gate[j], ml_norm[j], ml_skip[j], ml_w_out_b[j])
                init = None if init_states is None else init_states(j)
                x, cj, nj, mj = _mlstm_layer(x, mod, trunk, norm_pre[i], norm_post[i], mw, init)
                new_c.append(cj)
                new_n.append(nj)
                new_m.append(mj)
        return x.reshape(batch, seq_len, d), new_c, new_n, new_m

    y_prompt, cs, ns, ms = run_trunk(x_prompt, 0, False, None)
    dk = ml_wq.shape[3]
    new_state_c = jnp.concatenate(cs, axis=1)
    new_state_n = jnp.stack([jnp.swapaxes(n[:, :, :, 0, :], 1, 2) for n in ns], axis=1)
    new_state_m = jnp.stack([jnp.swapaxes(m[:, :, :, 0, 0], 1, 2) for m in ms], axis=1)

    def lat_init(j):
        c0 = state_C[:, j:j + 1]
        n0 = jnp.swapaxes(state_n[:, j], 1, 2)[:, :, :, None, :]
        m0 = jnp.broadcast_to(jnp.swapaxes(state_m[:, j], 1, 2)[:, :, :, None, None],
                              (bs, n_heads, 2, 1, 128))
        return c0, n0, m0

    y_sample, _, _, _ = run_trunk(x_sample, 1, True, lat_init)
    return (y_prompt, y_sample, new_state_c, new_state_n, new_state_m)
```

```python
import functools
import math

import numpy as np
import jax
import jax.numpy as jnp
from jax import lax
from jax.experimental import pallas as pl
from jax.experimental.pallas import tpu as pltpu

F32 = jnp.float32
BF16 = jnp.bfloat16
EPS = 1e-6
N_BANDS = 16
SUBLANES = 8
SUB_TILES = 2
HALO = 16
SCAN_CHUNK = 256
FFT_N2 = 128
DIRECT_CONV_MAX_L = 512
VMEM_LIMIT = 56 * 1024 * 1024
HI = lax.Precision.HIGHEST


def _cparams(sem):
    return pltpu.CompilerParams(dimension_semantics=sem, vmem_limit_bytes=VMEM_LIMIT)


def _tile(n, pref, mult):
    if n <= pref:
        return n
    t = (pref // mult) * mult
    while t >= mult:
        if n % t == 0:
            return t
        t -= mult
    return n


def _row_tile(batch, seq_len, pref, mult):
    if seq_len >= pref:
        return _tile(seq_len, pref, mult)
    k = max(1, pref // seq_len)
    while batch % k:
        k -= 1
    return seq_len * k


def _silu(x):
    return x * jax.nn.sigmoid(x)


def _ada_kernel(c_ref, w_ref, b_ref, o_ref):
    a = _silu(c_ref[...]).astype(BF16)
    o_ref[0] = jnp.dot(a, w_ref[0].astype(BF16), preferred_element_type=F32) + b_ref[0]


def _ada(conds, ada_w, ada_b):
    depth, d, n = ada_w.shape
    rows = conds.shape[0]
    tn = _tile(n, 512, 128)
    return pl.pallas_call(
        _ada_kernel,
        grid=(depth, n // tn),
        in_specs=[pl.BlockSpec((rows, d), lambda l, j: (0, 0)),
                  pl.BlockSpec((1, d, tn), lambda l, j: (l, 0, j)),
                  pl.BlockSpec((1, 1, tn), lambda l, j: (l, 0, j))],
        out_specs=pl.BlockSpec((1, rows, tn), lambda l, j: (l, 0, j)),
        out_shape=jax.ShapeDtypeStruct((depth, rows, n), F32),
        compiler_params=_cparams(("arbitrary", "arbitrary")),
        name="ada_mod",
    )(conds, ada_w, ada_b.reshape(depth, 1, n))


def _inproj_kernel(x_ref, g_ref, shift_ref, scale_ref, w_ref, o_ref, u_scr):
    @pl.when(pl.program_id(1) == 0)
    def _():
        x = x_ref[...]
        r = lax.rsqrt(jnp.mean(x * x, axis=-1, keepdims=True) + EPS)
        u = x * r * g_ref[...]
        u_scr[...] = (u * (1.0 + scale_ref[0]) + shift_ref[0]).astype(BF16)

    o_ref[...] = jnp.dot(u_scr[...], w_ref[...], preferred_element_type=F32).astype(o_ref.dtype)


def _cond_index(trunk, tm):
    batch, seq_len, cond0, per_seq = trunk
    if per_seq:
        return lambda i: cond0 + (i * tm) // seq_len
    return lambda i: cond0


def _cond_row_tile(trunk, pref):
    batch, seq_len, _, per_seq = trunk
    if per_seq:
        return _tile(seq_len, pref, 16)
    return _tile(batch * seq_len, pref, 16)


def _inproj(x, norm_g, mod, trunk, w, out_dtype):
    t, d = x.shape
    n = w.shape[1]
    tm = _cond_row_tile(trunk, 1024)
    tn = _tile(n, 1024, 128)
    cidx = _cond_index(trunk, tm)
    return pl.pallas_call(
        _inproj_kernel,
        grid=(t // tm, n // tn),
        in_specs=[pl.BlockSpec((tm, d), lambda i, j: (i, 0)),
                  pl.BlockSpec((1, d), lambda i, j: (0, 0)),
                  pl.BlockSpec((1, 1, d), lambda i, j: (cidx(i), 0, 0)),
                  pl.BlockSpec((1, 1, d), lambda i, j: (cidx(i), 0, 1)),
                  pl.BlockSpec((d, tn), lambda i, j: (0, j))],
        out_specs=pl.BlockSpec((tm, tn), lambda i, j: (i, j)),
        out_shape=jax.ShapeDtypeStruct((t, n), out_dtype),
        scratch_shapes=[pltpu.VMEM((tm, d), BF16)],
        compiler_params=_cparams(("arbitrary", "arbitrary")),
        name="in_proj",
    )(x, norm_g.reshape(1, d), mod, mod, w)


def _outproj_epilogue(acc, x_ref, g_ref, gate_ref, o_ref):
    y = acc
    r = lax.rsqrt(jnp.mean(y * y, axis=-1, keepdims=True) + EPS)
    o_ref[...] = x_ref[...] + gate_ref[0] * (y * r * g_ref[...])


def _outproj_kernel(y_ref, w_ref, x_ref, g_ref, gate_ref, o_ref, acc):
    k = pl.program_id(1)

    @pl.when(k == 0)
    def _():
        acc[...] = jnp.zeros_like(acc)

    acc[...] += jnp.dot(y_ref[...], w_ref[...], preferred_element_type=F32)

    @pl.when(k == pl.num_programs(1) - 1)
    def _():
        _outproj_epilogue(acc[...], x_ref, g_ref, gate_ref, o_ref)


def _outproj(y, w, x, norm_g, mod, trunk):
    t, e = y.shape
    d = w.shape[1]
    tm = _cond_row_tile(trunk, 512)
    tk = _tile(e, 1024, 128)
    cidx = _cond_index(trunk, tm)
    return pl.pallas_call(
        _outproj_kernel,
        grid=(t // tm, e // tk),
        in_specs=[pl.BlockSpec((tm, tk), lambda i, k: (i, k)),
                  pl.BlockSpec((tk, d), lambda i, k: (k, 0)),
                  pl.BlockSpec((tm, d), lambda i, k: (i, 0)),
                  pl.BlockSpec((1, d), lambda i, k: (0, 0)),
                  pl.BlockSpec((1, 1, d), lambda i, k: (cidx(i), 0, 2))],
        out_specs=pl.BlockSpec((tm, d), lambda i, k: (i, 0)),
        out_shape=jax.ShapeDtypeStruct((t, d), F32),
        scratch_shapes=[pltpu.VMEM((tm, d), F32)],
        compiler_params=_cparams(("arbitrary", "arbitrary")),
        name="out_proj",
    )(y, w, x, norm_g.reshape(1, d), mod)


def _conv3(x, prev_row, next_row, w, b, row0, seq_len):
    tm = x.shape[0]
    rows = lax.broadcasted_iota(jnp.int32, x.shape, 0)
    pos = (rows + row0) % seq_len
    prev = jnp.where(rows == 0, prev_row, pltpu.roll(x, 1, 0))
    nxt = jnp.where(rows == tm - 1, next_row, pltpu.roll(x, tm - 1, 0))
    prev = jnp.where(pos == 0, 0.0, prev)
    nxt = jnp.where(pos == seq_len - 1, 0.0, nxt)
    return w[0:1] * prev + w[1:2] * x + w[2:3] * nxt + b


def _halo_specs(tm, tc, nrows, colblk):
    nh = nrows // HALO
    r = tm // HALO
    return [pl.BlockSpec((tm, tc), lambda i, j: (i, colblk(j))),
            pl.BlockSpec((HALO, tc), lambda i, j: (jnp.maximum(i * r - 1, 0), colblk(j))),
            pl.BlockSpec((HALO, tc), lambda i, j: (jnp.minimum((i + 1) * r, nh - 1), colblk(j)))]


def _hy_inproj_kernel(seq_len, x_ref, xp_ref, xn_ref, g_ref, shift_ref, scale_ref,
                      w0_ref, w1_ref, w2_ref, wz_ref, cw, cb, vx_ref, g0_ref, u_scr):
    tm = x_ref.shape[0]
    ext = tm + 2 * HALO

    @pl.when(pl.program_id(1) == 0)
    def _():
        def modulated(x):
            r = lax.rsqrt(jnp.mean(x * x, axis=-1, keepdims=True) + EPS)
            return ((x * r * g_ref[...]) * (1.0 + scale_ref[0]) + shift_ref[0]).astype(BF16)

        u_scr[0:HALO] = modulated(xp_ref[...])
        u_scr[HALO:HALO + tm] = modulated(x_ref[...])
        u_scr[HALO + tm:ext] = modulated(xn_ref[...])

    u = u_scr[...]
    rows = lax.broadcasted_iota(jnp.int32, (tm, 1), 0)
    pos = (rows + pl.program_id(0) * tm) % seq_len
    first = pos == 0
    last = pos == seq_len - 1

    def conv(w_ref, k):
        r = jnp.dot(u, w_ref[...], preferred_element_type=F32)
        prev = jnp.where(first, 0.0, pltpu.roll(r, 1, 0)[HALO:HALO + tm])
        nxt = jnp.where(last, 0.0, pltpu.roll(r, ext - 1, 0)[HALO:HALO + tm])
        w = cw[k]
        return w[0:1] * prev + w[1:2] * r[HALO:HALO + tm] + w[2:3] * nxt + cb[k]

    x1c = conv(w1_ref, 1)
    vc = conv(w2_ref, 2)
    vx_ref[...] = (vc * x1c).astype(vx_ref.dtype)
    x0c = conv(w0_ref, 0)
    z = jnp.dot(u_scr[HALO:HALO + tm], wz_ref[...], preferred_element_type=F32)
    g0_ref[...] = (x0c * _silu(z)).astype(g0_ref.dtype)


def _hy_inproj(x, norm_g, mod, trunk, w, conv_w, conv_b, e):
    t, d = x.shape
    seq_len = trunk[1]
    tm = _cond_row_tile(trunk, 1024)
    tc = _tile(e, 256, 128)
    nb = e // tc
    cidx = _cond_index(trunk, tm)
    nh = t // HALO
    r = tm // HALO
    wspec = lambda part: pl.BlockSpec((d, tc), lambda i, j: (0, part * nb + j))
    cw = conv_w.reshape(3, 3, e).transpose(1, 0, 2)
    cb = conv_b.reshape(3, 1, e)
    return pl.pallas_call(
        functools.partial(_hy_inproj_kernel, seq_len),
        grid=(t // tm, nb),
        in_specs=[pl.BlockSpec((tm, d), lambda i, j: (i, 0)),
                  pl.BlockSpec((HALO, d), lambda i, j: (jnp.maximum(i * r - 1, 0), 0)),
                  pl.BlockSpec((HALO, d), lambda i, j: (jnp.minimum((i + 1) * r, nh - 1), 0)),
                  pl.BlockSpec((1, d), lambda i, j: (0, 0)),
                  pl.BlockSpec((1, 1, d), lambda i, j: (cidx(i), 0, 0)),
                  pl.BlockSpec((1, 1, d), lambda i, j: (cidx(i), 0, 1)),
                  wspec(0), wspec(1), wspec(2), wspec(3),
                  pl.BlockSpec((3, 3, tc), lambda i, j: (0, 0, j)),
                  pl.BlockSpec((3, 1, tc), lambda i, j: (0, 0, j))],
        out_specs=[pl.BlockSpec((tm, tc), lambda i, j: (i, j))] * 2,
        out_shape=[jax.ShapeDtypeStruct((t, e), BF16), jax.ShapeDtypeStruct((t, e), BF16)],
        scratch_shapes=[pltpu.VMEM((tm + 2 * HALO, d), BF16)],
        compiler_params=_cparams(("arbitrary", "arbitrary")),
        name="hy_in_proj_gate",
    )(x, x, x, norm_g.reshape(1, d), mod, mod, w, w, w, w, cw, cb)


def _filter_tables(seq_len):
    l = seq_len
    t = np.linspace(0.0, 1.0, l)
    w = 2.0 * math.pi * np.arange(l) / l
    f = np.linspace(1e-4, N_BANDS - 1, N_BANDS)
    z = np.concatenate([t[:, None], np.cos(f[None] * w[:, None]), -np.sin(f[None] * w[:, None])], axis=-1)
    pos = np.concatenate([np.arange(l), [0], np.arange(l - 1, 0, -1)])
    z2 = np.zeros((2 * l, 128), np.float32)
    z2[:, :z.shape[1]] = z[pos]
    aux = np.zeros((2 * l, 128), np.float32)
    aux[:, 0] = t[pos]
    aux[:, 1] = 1.0
    aux[l, 1] = 0.0
    return jnp.asarray(z2), jnp.asarray(aux)


def _filter_ffn_kernel(z_ref, w1, b1, w2, b2, w3, b3, fr, o_ref):
    f = fr[...]
    h = jnp.sin(f * (jnp.dot(z_ref[...], w1[...], precision=HI, preferred_element_type=F32) + b1[...]))
    h = jnp.sin(f * (jnp.dot(h, w2[...], precision=HI, preferred_element_type=F32) + b2[...]))
    h = jnp.sin(f * (jnp.dot(h, w3[...], precision=HI, preferred_element_type=F32) + b3[...]))
    o_ref[...] = h.astype(o_ref.dtype)


def _filter_expand_kernel(h_ref, aux_ref, w4, dec, o_ref):
    k = jnp.dot(h_ref[...], w4[...].astype(BF16), preferred_element_type=F32)
    t = aux_ref[:, 0:1]
    keep = aux_ref[:, 1:2]
    o_ref[...] = k * jnp.exp(-t * jnp.abs(dec[0])) * keep


def _filters(seq_len, w1, b1, w2, b2, w3, b3, w4, freq, decay):
    e = decay.shape[-1]
    fo = w2.shape[0]
    z2, aux = _filter_tables(seq_len)
    w1p = jnp.zeros((128, fo), F32).at[:w1.shape[0]].set(w1)
    tr = _tile(seq_len, 1024, 8)
    tc = _tile(e, 2048, 128)
    nb = e // tc
    nr = seq_len // tr
    small = lambda a: pl.BlockSpec(a.shape, lambda *_: (0,) * a.ndim)
    b1r, b2r, b3r, frr = (a.reshape(1, fo) for a in (b1, b2, b3, freq))
    h3 = pl.pallas_call(
        _filter_ffn_kernel,
        grid=(2 * nr,),
        in_specs=[pl.BlockSpec((tr, 128), lambda i: (i, 0)),
                  small(w1p), small(b1r), small(w2), small(b2r), small(w3), small(b3r), small(frr)],
        out_specs=pl.BlockSpec((tr, fo), lambda i: (i, 0)),
        out_shape=jax.ShapeDtypeStruct((2 * seq_len, fo), BF16),
        compiler_params=_cparams(("arbitrary",)),
        name="hy_filter_ffn",
    )(z2, w1p, b1r, w2, b2r, w3, b3r, frr)
    return pl.pallas_call(
        _filter_expand_kernel,
        grid=(2 * nr, nb),
        in_specs=[pl.BlockSpec((tr, fo), lambda i, j: (i, 0)),
                  pl.BlockSpec((tr, 128), lambda i, j: (i, 0)),
                  pl.BlockSpec((fo, tc), lambda i, j: (0, (i // nr) * nb + j)),
                  pl.BlockSpec((1, 1, tc), lambda i, j: (i // nr, 0, j))],
        out_specs=pl.BlockSpec((tr, tc), lambda i, j: (i, j)),
        out_shape=jax.ShapeDtypeStruct((2 * seq_len, e), F32),
        compiler_params=_cparams(("arbitrary", "arbitrary")),
        name="hy_filter_expand",
    )(h3, aux, w4, decay.reshape(2, 1, e))


def _cmm_kernel(w_ref, x_ref, o_ref):
    o_ref[0] = jnp.dot(w_ref[...], x_ref[0], preferred_element_type=F32).astype(o_ref.dtype)


def _cmm(w, x, out_dtype):
    m, k = w.shape
    b, _, n = x.shape
    tn = _tile(n, 4096, 128)
    return pl.pallas_call(
        _cmm_kernel,
        grid=(b, n // tn),
        in_specs=[pl.BlockSpec((m, k), lambda bi, j: (0, 0)),
                  pl.BlockSpec((1, k, tn), lambda bi, j: (bi, 0, j))],
        out_specs=pl.BlockSpec((1, m, tn), lambda bi, j: (bi, 0, j)),
        out_shape=jax.ShapeDtypeStruct((b, m, n), out_dtype),
        compiler_params=_cparams(("arbitrary", "arbitrary")),
        name="const_lhs_matmul",
    )(w, x)


def _direct_mats(seq_len):
    l = seq_len
    n = 2 * l
    f = np.arange(l)[:, None]
    t = np.arange(n)[None, :]
    ang = 2.0 * math.pi * ((f * t) % n) / n
    fwd = np.concatenate([np.cos(ang), -np.sin(ang)], axis=0)
    fwd[l] = np.cos(math.pi * np.arange(n))
    tt = np.arange(l)[:, None]
    ff = np.arange(l)[None, :]
    ang2 = 2.0 * math.pi * ((tt * ff) % n) / n
    wgt = np.where(ff == 0, 1.0, 2.0) / n
    inv = np.concatenate([wgt * np.cos(ang2), -wgt * np.sin(ang2)], axis=1)
    inv[:, l] = np.cos(math.pi * np.arange(l)) / n
    return fwd, inv


def _direct_conv_kernel(seq_len, f_ref, v_ref, x_ref, k_ref, g0_ref, bias_ref, o_ref):
    l = seq_len
    x = x_ref[0]
    s = jnp.dot(f_ref[...], x, preferred_element_type=F32)
    sre, sim = s[:l], s[l:]
    kre, kim = k_ref[:l, :], k_ref[l:, :]
    row0 = lax.broadcasted_iota(jnp.int32, sre.shape, 0) == 0
    yre = sre * kre - jnp.where(row0, 0.0, sim * kim)
    yim = jnp.where(row0, sim * kim, sre * kim + sim * kre)
    y = jnp.concatenate([yre, yim], axis=0).astype(BF16)
    out = jnp.dot(v_ref[...], y, preferred_element_type=F32)
    out = out + x.astype(F32) * bias_ref[...]
    o_ref[0] = (out * g0_ref[0].astype(F32)).astype(o_ref.dtype)


def _longconv_direct(vx, g0, kfilt, bias, batch, seq_len):
    e = vx.shape[-1]
    l = seq_len
    fwd, inv = _direct_mats(l)
    f_full = jnp.asarray(fwd, BF16)
    f_data = jnp.asarray(fwd[:, :l], BF16)
    v_mat = jnp.asarray(inv, BF16)
    kspec = _cmm(f_full, kfilt.astype(BF16)[None], F32)[0]
    tc = _tile(e, 1024, 128)
    out = pl.pallas_call(
        functools.partial(_direct_conv_kernel, l),
        grid=(e // tc, batch),
        in_specs=[pl.BlockSpec((2 * l, l), lambda j, b: (0, 0)),
                  pl.BlockSpec((l, 2 * l), lambda j, b: (0, 0)),
                  pl.BlockSpec((1, l, tc), lambda j, b: (b, 0, j)),
                  pl.BlockSpec((2 * l, tc), lambda j, b: (0, j)),
                  pl.BlockSpec((1, l, tc), lambda j, b: (b, 0, j)),
                  pl.BlockSpec((1, tc), lambda j, b: (0, j))],
        out_specs=pl.BlockSpec((1, l, tc), lambda j, b: (b, 0, j)),
        out_shape=jax.ShapeDtypeStruct((batch, l, e), BF16),
        compiler_params=_cparams(("arbitrary", "arbitrary")),
        name="longconv_direct",
    )(f_data, v_mat, vx.reshape(batch, l, e), kspec, g0.reshape(batch, l, e), bias.reshape(1, e))
    return out.reshape(batch * l, e)


def _two_level_mats(seq_len):
    n = 2 * seq_len
    n2 = FFT_N2
    n1 = n // n2
    h2 = n2 // 2
    ns = 8 * ((h2 + 1 + 7) // 8)
    f2 = np.arange(h2 + 1)[:, None]
    t2 = np.arange(n2)[None, :]
    ang = 2.0 * math.pi * ((f2 * t2) % n2) / n2
    f1m = np.zeros((2 * ns, n2))
    f1m[0:2 * (h2 + 1):2] = np.cos(ang)
    f1m[1:2 * (h2 + 1):2] = -np.sin(ang)
    wgt = np.where((f2 == 0) | (f2 == h2), 1.0, 2.0) / n
    g1m = np.zeros((h2, 2 * ns))
    g1m[:, 0:2 * (h2 + 1):2] = (wgt * np.cos(ang[:, :h2])).T
    g1m[:, 1:2 * (h2 + 1):2] = -(wgt * np.sin(ang[:, :h2])).T
    t1 = np.arange(n1)[None, :]
    f1 = np.arange(n1)[:, None]
    rm = np.zeros((ns, 2 * n1, 2 * n1))
    pm = np.zeros((ns, 2 * n1, 2 * n1))
    for s in range(h2 + 1):
        a = 2.0 * math.pi * ((t1 * (n2 * f1 + s)) % n) / n
        mr, mi = np.cos(a), -np.sin(a)
        rm[s] = np.block([[mr, -mi], [mi, mr]])
        pr, pi = mr.T, -mi.T
        pm[s] = np.block([[pr, -pi], [pi, pr]])
    return n1, n2, ns, f1m, g1m, rm, pm


def _slab_spec_kernel(r_ref, a_ref, o_ref):
    n1 = a_ref.shape[3]
    a = jnp.concatenate([a_ref[0, 0, 0], a_ref[0, 0, 1]], axis=0)
    s = jnp.dot(r_ref[0], a, preferred_element_type=F32)
    o_ref[0, 0] = s[:n1]
    o_ref[0, 1] = s[n1:]


def _slab_conv_kernel(r_ref, p_ref, a_ref, k_ref, o_ref):
    n1 = a_ref.shape[3]
    a = jnp.concatenate([a_ref[0, 0, 0], a_ref[0, 0, 1]], axis=0)
    s = jnp.dot(r_ref[0], a, preferred_element_type=F32)
    sre, sim = s[:n1], s[n1:]
    kre, kim = k_ref[0, 0], k_ref[0, 1]
    y = jnp.concatenate([sre * kre - sim * kim, sre * kim + sim * kre], axis=0).astype(BF16)
    o = jnp.dot(p_ref[0], y, preferred_element_type=F32)
    o_ref[0, 0, 0] = o[:n1].astype(o_ref.dtype)
    o_ref[0, 0, 1] = o[n1:].astype(o_ref.dtype)


def _dft_l1_kernel(k_ref, x_ref, o_ref):
    x = x_ref[0].astype(F32)
    nt2, _, tc = x.shape
    halves = []
    for g in range(SUB_TILES):
        xg = x[:, SUBLANES * g:SUBLANES * (g + 1), :].reshape(nt2 * SUBLANES, tc).astype(BF16)
        a = jnp.dot(k_ref[...], xg, preferred_element_type=F32)
        halves.append(a.reshape(-1, SUBLANES, tc))
    o_ref[0] = jnp.concatenate(halves, axis=1).astype(o_ref.dtype)


def _dft_l1_inv_kernel(g_ref, a_ref, vx_ref, g0_ref, bias_ref, o_ref):
    a = a_ref[0].astype(F32)
    nr, _, tc = a.shape
    halves = []
    for g in range(SUB_TILES):
        ag = a[:, SUBLANES * g:SUBLANES * (g + 1), :].reshape(nr * SUBLANES, tc).astype(BF16)
        y = jnp.dot(g_ref[...], ag, preferred_element_type=F32)
        halves.append(y.reshape(-1, SUBLANES, tc))
    y = jnp.concatenate(halves, axis=1)
    y = y + vx_ref[0].astype(F32) * bias_ref[...]
    o_ref[0] = (y * g0_ref[0].astype(F32)).astype(o_ref.dtype)


def _dft_l1(kmat, x4, out_dtype):
    b, nt2, n1, e = x4.shape
    m = kmat.shape[0] // SUBLANES
    tc = _tile(e, 1024, 128)
    rows = SUBLANES * SUB_TILES
    return pl.pallas_call(
        _dft_l1_kernel,
        grid=(b, n1 // rows, e // tc),
        in_specs=[pl.BlockSpec(kmat.shape, lambda bi, i, j: (0, 0)),
                  pl.BlockSpec((1, nt2, rows, tc), lambda bi, i, j: (bi, 0, i, j))],
        out_specs=pl.BlockSpec((1, m, rows, tc), lambda bi, i, j: (bi, 0, i, j)),
        out_shape=jax.ShapeDtypeStruct((b, m, n1, e), out_dtype),
        compiler_params=_cparams(("arbitrary", "arbitrary", "arbitrary")),
        name="dft_level1",
    )(kmat, x4)


def _longconv_two_level(vx, g0, kfilt, bias, batch, seq_len):
    e = vx.shape[-1]
    l = seq_len
    n1, n2, ns, f1m, g1m, rm, pm = _two_level_mats(l)
    h2 = n2 // 2
    eye = np.eye(SUBLANES)
    k_full = jnp.asarray(np.kron(f1m, eye), BF16)
    k_data = jnp.asarray(np.kron(f1m[:, :h2], eye), BF16)
    g8 = jnp.asarray(np.kron(g1m, eye), BF16)
    rmat = jnp.asarray(rm, BF16)
    pmat = jnp.asarray(pm, BF16)
    tc = _tile(e, 4096, 128)
    nc = e // tc

    ka = _dft_l1(k_full, kfilt.reshape(1, n2, n1, e), BF16).reshape(1, ns, 2, n1, e)
    kspec = pl.pallas_call(
        _slab_spec_kernel,
        grid=(ns, nc),
        in_specs=[pl.BlockSpec((1, 2 * n1, 2 * n1), lambda s, j: (s, 0, 0)),
                  pl.BlockSpec((1, 1, 2, n1, tc), lambda s, j: (0, s, 0, 0, j))],
        out_specs=pl.BlockSpec((1, 2, n1, tc), lambda s, j: (s, 0, 0, j)),
        out_shape=jax.ShapeDtypeStruct((ns, 2, n1, e), F32),
        compiler_params=_cparams(("arbitrary", "arbitrary")),
        name="filter_slab_dft",
    )(rmat, ka)

    vx4 = vx.reshape(batch, h2, n1, e)
    a = _dft_l1(k_data, vx4, BF16).reshape(batch, ns, 2, n1, e)
    a2 = pl.pallas_call(
        _slab_conv_kernel,
        grid=(ns, batch, nc),
        in_specs=[pl.BlockSpec((1, 2 * n1, 2 * n1), lambda s, b, j: (s, 0, 0)),
                  pl.BlockSpec((1, 2 * n1, 2 * n1), lambda s, b, j: (s, 0, 0)),
                  pl.BlockSpec((1, 1, 2, n1, tc), lambda s, b, j: (b, s, 0, 0, j)),
                  pl.BlockSpec((1, 2, n1, tc), lambda s, b, j: (s, 0, 0, j))],
        out_specs=pl.BlockSpec((1, 1, 2, n1, tc), lambda s, b, j: (b, s, 0, 0, j)),
        out_shape=jax.ShapeDtypeStruct((batch, ns, 2, n1, e), BF16),
        compiler_params=_cparams(("arbitrary", "arbitrary", "arbitrary")),
        name="slab_conv",
    )(rmat, pmat, a, kspec)
    a2 = a2.reshape(batch, 2 * ns, n1, e)
    rows = SUBLANES * SUB_TILES
    tci = _tile(e, 1024, 128)
    blk = lambda r: pl.BlockSpec((1, r, rows, tci), lambda bi, i, j: (bi, 0, i, j))
    out = pl.pallas_call(
        _dft_l1_inv_kernel,
        grid=(batch, n1 // rows, e // tci),
        in_specs=[pl.BlockSpec(g8.shape, lambda bi, i, j: (0, 0)),
                  blk(2 * ns), blk(h2), blk(h2),
                  pl.BlockSpec((1, 1, tci), lambda bi, i, j: (0, 0, j))],
        out_specs=blk(h2),
        out_shape=jax.ShapeDtypeStruct((batch, h2, n1, e), BF16),
        compiler_params=_cparams(("arbitrary", "arbitrary", "arbitrary")),
        name="dft_level1_inverse",
    )(g8, a2, vx4, g0.reshape(batch, h2, n1, e), bias.reshape(1, 1, e))
    return out.reshape(batch * l, e)


def _longconv(vx, g0, kfilt, bias, batch, seq_len):
    if seq_len <= DIRECT_CONV_MAX_L:
        return _longconv_direct(vx, g0, kfilt, bias, batch, seq_len)
    return _longconv_two_level(vx, g0, kfilt, bias, batch, seq_len)


def _ml_qkv_kernel(seq_len, n_heads, dk, x_ref, xp_ref, xn_ref, cw_ref, cb_ref, wq_ref, wk_ref, wv_ref,
                   wgq_ref, wgk_ref, wgv_ref, bg_ref, q_ref, k_ref, kt_ref, v_ref, xc_ref, g_ref, gt_ref, gacc):
    h = pl.program_id(1)
    tm = x_ref.shape[0]
    q_len = gt_ref.shape[2]
    x = x_ref[...].astype(F32)
    conv = _conv3(x, xp_ref[HALO - 1:HALO, :].astype(F32), xn_ref[0:1, :].astype(F32),
                  cw_ref[...], cb_ref[...], pl.program_id(0) * tm, seq_len)
    xc = _silu(conv)
    xc_ref[...] = xc.astype(xc_ref.dtype)
    xcb = xc.astype(BF16)
    q = jnp.dot(xcb, wq_ref[0], preferred_element_type=F32).astype(BF16)
    kf = jnp.dot(xcb, wk_ref[0], preferred_element_type=F32) * (dk ** -0.5)
    k = kf.astype(BF16)
    v = jnp.dot(x.astype(BF16), wv_ref[0], preferred_element_type=F32).astype(BF16)
    q_ref[...] = q
    k_ref[...] = k
    v_ref[...] = v
    for c in range(tm // q_len):
        kt_ref[c] = kf[c * q_len:(c + 1) * q_len, :].T.astype(BF16)
    part = (jnp.dot(q, wgq_ref[0], preferred_element_type=F32)
            + jnp.dot(k, wgk_ref[0], preferred_element_type=F32)
            + jnp.dot(v, wgv_ref[0], preferred_element_type=F32))

    @pl.when(h == 0)
    def _():
        gacc[...] = part + bg_ref[...]

    @pl.when(h > 0)
    def _():
        gacc[...] += part

    @pl.when(h == n_heads - 1)
    def _():
        g = gacc[...]
        col = lax.broadcasted_iota(jnp.int32, g.shape, 1)
        is_forget = (col % (2 * n_heads)) >= n_heads
        g = jnp.where(is_forget, jax.nn.log_sigmoid(g), g)
        col_q = lax.broadcasted_iota(jnp.int32, (q_len, g.shape[1]), 1)
        forget_q = (col_q % (2 * n_heads)) >= n_heads
        bwd_q = (col_q // (2 * n_heads)) == 1
        r_i = lax.broadcasted_iota(jnp.int32, (q_len, q_len), 0)
        c_i = lax.broadcasted_iota(jnp.int32, (q_len, q_len), 1)
        tril = (c_i <= r_i).astype(F32)
        for c in range(tm // q_len):
            gc = g[c * q_len:(c + 1) * q_len, :]
            pre = jnp.dot(tril, gc, precision=HI, preferred_element_type=F32)
            tot = jnp.sum(gc, axis=0, keepdims=True)
            suf = tot - pre + gc
            gc = jnp.where(forget_q, jnp.where(bwd_q, suf, pre), gc)
            g_ref[c * q_len:(c + 1) * q_len, :] = gc
            gt_ref[c] = gc.T


def _ml_qkv(p, conv_w, conv_b, wq, wk, wv, w_gate, b_gate, batch, seq_len, e):
    t = p.shape[0]
    n_heads, dh, dk = wq.shape
    dv = wv.shape[2]
    q_len = min(SCAN_CHUNK, seq_len)
    tm = _row_tile(batch, seq_len, 512, q_len)
    ng = w_gate.shape[1]
    gpad = 128
    wgq = jnp.zeros((n_heads, dk, gpad), BF16).at[:, :, :ng].set(
        w_gate[:n_heads * dk].reshape(n_heads, dk, ng).astype(BF16))
    wgk = jnp.zeros((n_heads, dk, gpad), BF16).at[:, :, :ng].set(
        w_gate[n_heads * dk:2 * n_heads * dk].reshape(n_heads, dk, ng).astype(BF16))
    wgv = jnp.zeros((n_heads, dv, gpad), BF16).at[:, :, :ng].set(
        w_gate[2 * n_heads * dk:].reshape(n_heads, dv, ng).astype(BF16))
    bg = jnp.zeros((1, gpad), F32).at[0, :ng].set(b_gate)
    specs = _halo_specs(tm, dh, t, lambda j: j)
    specs += [pl.BlockSpec((3, dh), lambda i, j: (0, j)),
              pl.BlockSpec((1, dh), lambda i, j: (0, j)),
              pl.BlockSpec((1, dh, dk), lambda i, j: (j, 0, 0)),
              pl.BlockSpec((1, dh, dk), lambda i, j: (j, 0, 0)),
              pl.BlockSpec((1, dh, dv), lambda i, j: (j, 0, 0)),
              pl.BlockSpec((1, dk, gpad), lambda i, j: (j, 0, 0)),
              pl.BlockSpec((1, dk, gpad), lambda i, j: (j, 0, 0)),
              pl.BlockSpec((1, dv, gpad), lambda i, j: (j, 0, 0)),
              pl.BlockSpec((1, gpad), lambda i, j: (0, 0))]
    return pl.pallas_call(
        functools.partial(_ml_qkv_kernel, seq_len, n_heads, dk),
        grid=(t // tm, n_heads),
        in_specs=specs,
        out_specs=[pl.BlockSpec((tm, dk), lambda i, j: (i, j)),
                   pl.BlockSpec((tm, dk), lambda i, j: (i, j)),
                   pl.BlockSpec((tm // q_len, dk, q_len), lambda i, j: (i, j, 0)),
                   pl.BlockSpec((tm, dv), lambda i, j: (i, j)),
                   pl.BlockSpec((tm, dh), lambda i, j: (i, j)),
                   pl.BlockSpec((tm, gpad), lambda i, j: (i, 0)),
                   pl.BlockSpec((tm // q_len, gpad, q_len), lambda i, j: (i, 0, 0))],
        out_shape=[jax.ShapeDtypeStruct((t, n_heads * dk), BF16),
                   jax.ShapeDtypeStruct((t, n_heads * dk), BF16),
                   jax.ShapeDtypeStruct((t // q_len, n_heads * dk, q_len), BF16),
                   jax.ShapeDtypeStruct((t, n_heads * dv), BF16),
                   jax.ShapeDtypeStruct((t, e), BF16),
                   jax.ShapeDtypeStruct((t, gpad), F32),
                   jax.ShapeDtypeStruct((t // q_len, gpad, q_len), F32)],
        scratch_shapes=[pltpu.VMEM((tm, gpad), F32)],
        compiler_params=_cparams(("arbitrary", "arbitrary")),
        name="ml_qkv_gates",
    )(p, p, p, conv_w, conv_b.reshape(1, e), wq.astype(BF16), wk.astype(BF16), wv.astype(BF16),
      wgq, wgk, wgv, bg)


def _ml_scan_kernel(n_heads, has_init, *refs):
    if has_init:
        (q_ref, k_ref, kt_ref, v_ref, g_ref, gt_ref, c0_ref, n0_ref, m0_ref,
         h_ref, cout_ref, nout_ref, mout_ref, c_scr, n_scr, m_scr) = refs
    else:
        (q_ref, k_ref, kt_ref, v_ref, g_ref, gt_ref,
         h_ref, cout_ref, nout_ref, mout_ref, c_scr, n_scr, m_scr) = refs
    head = pl.program_id(1)
    seq_len = q_ref.shape[0]
    q_len = gt_ref.shape[2]
    n_chunks = seq_len // q_len

    for d in range(2):
        if has_init:
            c_scr[d] = c0_ref[0, 0, d, 0]
            n_scr[d] = n0_ref[0, 0, d]
            m_scr[d] = m0_ref[0, 0, d]
        else:
            c_scr[d] = jnp.zeros(c_scr.shape[1:], F32)
            n_scr[d] = jnp.zeros(n_scr.shape[1:], F32)
            m_scr[d] = jnp.zeros(m_scr.shape[1:], F32)
    h_ref[...] = jnp.zeros_like(h_ref)

    row_i = lax.broadcasted_iota(jnp.int32, (q_len, q_len), 0)
    col_i = lax.broadcasted_iota(jnp.int32, (q_len, q_len), 1)
    lane_g = lax.broadcasted_iota(jnp.int32, (q_len, g_ref.shape[1]), 1)

    def direction(d, c):
        icol = d * 2 * n_heads + head
        fcol = icol + n_heads
        rows = pl.ds(pl.multiple_of(c * q_len, q_len), q_len)
        g = g_ref[rows, :]
        i_col = jnp.sum(jnp.where(lane_g == icol, g, 0.0), axis=1, keepdims=True)
        b_col = jnp.sum(jnp.where(lane_g == fcol, g, 0.0), axis=1, keepdims=True)
        i_row = gt_ref[c, pl.ds(icol, 1), :]
        b_row = gt_ref[c, pl.ds(fcol, 1), :]
        mask = (col_i <= row_i) if d == 0 else (col_i >= row_i)
        last = q_len - 1 if d == 0 else 0
        g_tot = jnp.sum(jnp.where(col_i[0:1, :] == last, b_row, 0.0), axis=1, keepdims=True)
        m_prev = m_scr[d][:, 0:1]
        n_prev = n_scr[d]
        c_prev = c_scr[d]

        qc = q_ref[rows, :]
        kc = k_ref[rows, :]
        vc = v_ref[rows, :]
        dmat = jnp.where(mask, b_col - b_row + i_row, -jnp.inf)
        inter = b_col + m_prev
        m_t = jnp.maximum(inter, jnp.max(dmat, axis=1, keepdims=True))
        qk = lax.dot_general(qc, kc, (((1,), (1,)), ((), ())), preferred_element_type=F32)
        s = qk * jnp.exp(dmat - m_t)
        w_inter = jnp.exp(inter - m_t)
        num = (w_inter * jnp.dot(qc, c_prev.astype(BF16), preferred_element_type=F32)
               + jnp.dot(s.astype(BF16), vc, preferred_element_type=F32))
        qn = (w_inter * jnp.sum(qc.astype(F32) * n_prev, axis=1, keepdims=True)
              + jnp.sum(s, axis=1, keepdims=True))
        den = jnp.maximum(jnp.abs(qn), jnp.exp(-m_t))
        h_ref[rows, :] += num / den

        a_col = g_tot - b_col + i_col
        a_row = g_tot - b_row + i_row
        m_new = jnp.maximum(g_tot + m_prev, jnp.max(a_row, axis=1, keepdims=True))
        dec = jnp.exp(g_tot + m_prev - m_new)
        kwt = (kt_ref[c].astype(F32) * jnp.exp(a_row - m_new)).astype(BF16)
        c_scr[d] = dec * c_prev + jnp.dot(kwt, vc, preferred_element_type=F32)
        n_scr[d] = dec * n_prev + jnp.sum(kc.astype(F32) * jnp.exp(a_col - m_new), axis=0, keepdims=True)
        m_scr[d] = jnp.broadcast_to(m_new, m_scr.shape[1:])

    def body(j, carry):
        direction(0, j)
        direction(1, n_chunks - 1 - j)
        return carry

    lax.fori_loop(0, n_chunks, body, 0)
    for d in range(2):
        cout_ref[0, 0, d, 0] = c_scr[d]
        nout_ref[0, 0, d] = n_scr[d]
        mout_ref[0, 0, d] = m_scr[d]


def _ml_scan(q, k, kt, v, g, gt, batch, seq_len, n_heads, init=None):
    dk = q.shape[1] // n_heads
    dv = v.shape[1] // n_heads
    q_len = gt.shape[2]
    gpad = g.shape[1]
    nq = seq_len // q_len
    in_specs = [pl.BlockSpec((seq_len, dk), lambda b, h: (b, h)),
                pl.BlockSpec((seq_len, dk), lambda b, h: (b, h)),
                pl.BlockSpec((nq, dk, q_len), lambda b, h: (b, h, 0)),
                pl.BlockSpec((seq_len, dv), lambda b, h: (b, h)),
                pl.BlockSpec((seq_len, gpad), lambda b, h: (b, 0)),
                pl.BlockSpec((nq, gpad, q_len), lambda b, h: (b, 0, 0))]
    args = [q, k, kt, v, g, gt]
    if init is not None:
        c0, n0, m0 = init
        in_specs += [pl.BlockSpec((1, 1, 2, 1, dk, dv), lambda b, h: (b, 0, 0, h, 0, 0)),
                     pl.BlockSpec((1, 1, 2, 1, dk), lambda b, h: (b, h, 0, 0, 0)),
                     pl.BlockSpec((1, 1, 2, 1, 128), lambda b, h: (b, h, 0, 0, 0))]
        args += [c0, n0, m0]
    out = pl.pallas_call(
        functools.partial(_ml_scan_kernel, n_heads, init is not None),
        grid=(batch, n_heads),
        in_specs=in_specs,
        out_specs=[pl.BlockSpec((seq_len, dv), lambda b, h: (b, h)),
                   pl.BlockSpec((1, 1, 2, 1, dk, dv), lambda b, h: (b, 0, 0, h, 0, 0)),
                   pl.BlockSpec((1, 1, 2, 1, dk), lambda b, h: (b, h, 0, 0, 0)),
                   pl.BlockSpec((1, 1, 2, 1, 128), lambda b, h: (b, h, 0, 0, 0))],
        out_shape=[jax.ShapeDtypeStruct((batch * seq_len, n_heads * dv), F32),
                   jax.ShapeDtypeStruct((batch, 1, 2, n_heads, dk, dv), F32),
                   jax.ShapeDtypeStruct((batch, n_heads, 2, 1, dk), F32),
                   jax.ShapeDtypeStruct((batch, n_heads, 2, 1, 128), F32)],
        scratch_shapes=[pltpu.VMEM((2, dk, dv), F32), pltpu.VMEM((2, 1, dk), F32),
                        pltpu.VMEM((2, 1, 128), F32)],
        compiler_params=_cparams(("arbitrary", "arbitrary")),
        name="ml_scan",
    )(*args)
    return out


def _ml_out_kernel(h_ref, o_ref_in, xc_ref, z_ref, nw_ref, sk_ref, w_ref, x_ref, g_ref, gate_ref, out_ref, acc):
    k = pl.program_id(1)

    @pl.when(k == 0)
    def _():
        acc[...] = jnp.zeros_like(acc)

    hh = jax.nn.sigmoid(o_ref_in[...].astype(F32)) * h_ref[...]
    mu = jnp.mean(hh, axis=-1, keepdims=True)
    var = jnp.mean(jnp.square(hh - mu), axis=-1, keepdims=True)
    hn = (hh - mu) * lax.rsqrt(var + EPS) * nw_ref[...]
    hn = hn + sk_ref[...] * xc_ref[...].astype(F32)
    y = (hn * _silu(z_ref[...].astype(F32))).astype(BF16)
    acc[...] += jnp.dot(y, w_ref[...], preferred_element_type=F32)

    @pl.when(k == pl.num_programs(1) - 1)
    def _():
        _outproj_epilogue(acc[...], x_ref, g_ref, gate_ref, out_ref)


def _ml_out(h, p, xc, norm_w, skip, w, x, norm_g, mod, trunk, n_heads):
    t, e = h.shape
    d = w.shape[1]
    dv = e // n_heads
    tm = _cond_row_tile(trunk, 512)
    cidx = _cond_index(trunk, tm)
    return pl.pallas_call(
        _ml_out_kernel,
        grid=(t // tm, n_heads),
        in_specs=[pl.BlockSpec((tm, dv), lambda i, k: (i, k)),
                  pl.BlockSpec((tm, dv), lambda i, k: (i, 2 * n_heads + k)),
                  pl.BlockSpec((tm, dv), lambda i, k: (i, k)),
                  pl.BlockSpec((tm, dv), lambda i, k: (i, n_heads + k)),
                  pl.BlockSpec((1, dv), lambda i, k: (0, k)),
                  pl.BlockSpec((1, dv), lambda i, k: (0, k)),
                  pl.BlockSpec((dv, d), lambda i, k: (k, 0)),
                  pl.BlockSpec((tm, d), lambda i, k: (i, 0)),
                  pl.BlockSpec((1, d), lambda i, k: (0, 0)),
                  pl.BlockSpec((1, 1, d), lambda i, k: (cidx(i), 0, 2))],
        out_specs=pl.BlockSpec((tm, d), lambda i, k: (i, 0)),
        out_shape=jax.ShapeDtypeStruct((t, d), F32),
        scratch_shapes=[pltpu.VMEM((tm, d), F32)],
        compiler_params=_cparams(("arbitrary", "arbitrary")),
        name="ml_out_proj",
    )(h, p, xc, p, norm_w.reshape(1, e), skip.reshape(1, e), w, x, norm_g.reshape(1, d), mod)


def _hyena_layer(x, mod, trunk, norm_pre, norm_post, hw):
    (w_in, conv_w, conv_b, w1, b1, w2, b2, w3, b3, w4, freq, decay, bias, w_out) = hw
    batch, seq_len = trunk[0], trunk[1]
    e = w_out.shape[0]
    vx, g0 = _hy_inproj(x, norm_pre, mod, trunk, w_in, conv_w, conv_b, e)
    kfilt = _filters(seq_len, w1, b1, w2, b2, w3, b3, w4, freq, decay)
    y = _longconv(vx, g0, kfilt, bias, batch, seq_len)
    return _outproj(y, w_out, x, norm_post, mod, trunk)


def _mlstm_layer(x, mod, trunk, norm_pre, norm_post, mw, init):
    (w_in, conv_w, conv_b, wq, wk, wv, w_gate, b_gate, norm_w, skip, w_out) = mw
    batch, seq_len = trunk[0], trunk[1]
    e = w_out.shape[0]
    n_heads = wq.shape[0]
    p = _inproj(x, norm_pre, mod, trunk, w_in, BF16)
    q, k, kt, v, xc, g, gt = _ml_qkv(p, conv_w, conv_b, wq, wk, wv, w_gate, b_gate, batch, seq_len, e)
    h, c_new, n_new, m_new = _ml_scan(q, k, kt, v, g, gt, batch, seq_len, n_heads, init)
    x_new = _ml_out(h, p, xc, norm_w, skip, w_out, x, norm_post, mod, trunk, n_heads)
    return x_new, c_new, n_new, m_new


def kernel(x_prompt, x_sample, state_C, state_n, state_m, c, c_ctx, norm_pre, norm_post, ada_w, ada_b,
           hy_w_in, hy_conv_w, hy_conv_b, hy_ffn_w1, hy_ffn_b1, hy_ffn_w2, hy_ffn_b2, hy_ffn_w3, hy_ffn_b3,
           hy_ffn_w4, hy_sin_freq, hy_decay, hy_bias, hy_w_out, ml_w_in, ml_conv_w, ml_conv_b, ml_wq, ml_wk,
           ml_wv, ml_w_gate, ml_b_gate, ml_norm, ml_skip, ml_w_out):
    depth = norm_pre.shape[0]
    bp, lp, d = x_prompt.shape
    bs, ls, _ = x_sample.shape
    n_heads = ml_wq.shape[1]

    nrow = 8 * ((1 + bs + 7) // 8)
    conds = jnp.zeros((nrow, d), F32).at[0].set(c_ctx).at[1:1 + bs].set(c)
    mods = _ada(conds, ada_w, ada_b)

    hy_w_in_b, hy_w_out_b = hy_w_in.astype(BF16), hy_w_out.astype(BF16)
    ml_w_in_b, ml_w_out_b = ml_w_in.astype(BF16), ml_w_out.astype(BF16)

    def run_trunk(x3, cond0, per_seq, init_states):
        batch, seq_len, _ = x3.shape
        trunk = (batch, seq_len, cond0, per_seq)
        x = x3.reshape(batch * seq_len, d)
        new_c, new_n, new_m = [], [], []
        for i in range(depth):
            mod = mods[i].reshape(nrow, 1, 3 * d)
            j = i // 2
            if i % 2 == 0:
                hw = (hy_w_in_b[j], hy_conv_w[j], hy_conv_b[j], hy_ffn_w1[j], hy_ffn_b1[j], hy_ffn_w2[j],
                      hy_ffn_b2[j], hy_ffn_w3[j], hy_ffn_b3[j], hy_ffn_w4[j], hy_sin_freq[j], hy_decay[j],
                      hy_bias[j], hy_w_out_b[j])
                x = _hyena_layer(x, mod, trunk, norm_pre[i], norm_post[i], hw)
            else:
                mw = (ml_w_in_b[j], ml_conv_w[j], ml_conv_b[j], ml_wq[j], ml_wk[j], ml_wv[j], ml_w_gate[j],
                      ml_b_gate[j], ml_norm[j], ml_skip[j], ml_w_out_b[j])
                init = None if init_states is None else init_states(j)
                x, cj, nj, mj = _mlstm_layer(x, mod, trunk, norm_pre[i], norm_post[i], mw, init)
                new_c.append(cj)
                new_n.append(nj)
                new_m.append(mj)
        return x.reshape(batch, seq_len, d), new_c, new_n, new_m

    y_prompt, cs, ns, ms = run_trunk(x_prompt, 0, False, None)
    dk = ml_wq.shape[3]
    new_state_c = jnp.concatenate(cs, axis=1)
    new_state_n = jnp.stack([jnp.swapaxes(n[:, :, :, 0, :], 1, 2) for n in ns], axis=1)
    new_state_m = jnp.stack([jnp.swapaxes(m[:, :, :, 0, 0], 1, 2) for m in ms], axis=1)

    def lat_init(j):
        c0 = state_C[:, j:j + 1]
        n0 = jnp.swapaxes(state_n[:, j], 1, 2)[:, :, :, None, :]
        m0 = jnp.broadcast_to(jnp.swapaxes(state_m[:, j], 1, 2)[:, :, :, None, None],
                              (bs, n_heads, 2, 1, 128))
        return c0, n0, m0

    y_sample, _, _, _ = run_trunk(x_sample, 1, True, lat_init)
    return (y_prompt, y_sample, new_state_c, new_state_n, new_state_m)
```

```python
import functools
import math

import numpy as np
import jax
import jax.numpy as jnp
from jax import lax
from jax.experimental import pallas as pl
from jax.experimental.pallas import tpu as pltpu

F32 = jnp.float32
BF16 = jnp.bfloat16
EPS = 1e-6
N_BANDS = 16
SUBLANES = 8
SUB_TILES = 2
HALO = 16
SCAN_CHUNK = 256
FFT_N2 = 128
DIRECT_CONV_MAX_L = 512
VMEM_LIMIT = 56 * 1024 * 1024
HI = lax.Precision.HIGHEST


def _cparams(sem):
    return pltpu.CompilerParams(dimension_semantics=sem, vmem_limit_bytes=VMEM_LIMIT)


def _tile(n, pref, mult):
    if n <= pref:
        return n
    t = (pref // mult) * mult
    while t >= mult:
        if n % t == 0:
            return t
        t -= mult
    return n


def _row_tile(batch, seq_len, pref, mult):
    if seq_len >= pref:
        return _tile(seq_len, pref, mult)
    k = max(1, pref // seq_len)
    while batch % k:
        k -= 1
    return seq_len * k


def _silu(x):
    return x * jax.nn.sigmoid(x)


def _ada_kernel(c_ref, w_ref, b_ref, o_ref):
    a = _silu(c_ref[...]).astype(BF16)
    o_ref[0] = jnp.dot(a, w_ref[0].astype(BF16), preferred_element_type=F32) + b_ref[0]


def _ada(conds, ada_w, ada_b):
    depth, d, n = ada_w.shape
    rows = conds.shape[0]
    tn = _tile(n, 512, 128)
    return pl.pallas_call(
        _ada_kernel,
        grid=(depth, n // tn),
        in_specs=[pl.BlockSpec((rows, d), lambda l, j: (0, 0)),
                  pl.BlockSpec((1, d, tn), lambda l, j: (l, 0, j)),
                  pl.BlockSpec((1, 1, tn), lambda l, j: (l, 0, j))],
        out_specs=pl.BlockSpec((1, rows, tn), lambda l, j: (l, 0, j)),
        out_shape=jax.ShapeDtypeStruct((depth, rows, n), F32),
        compiler_params=_cparams(("arbitrary", "arbitrary")),
        name="ada_mod",
    )(conds, ada_w, ada_b.reshape(depth, 1, n))


def _inproj_kernel(x_ref, g_ref, shift_ref, scale_ref, w_ref, o_ref, u_scr):
    @pl.when(pl.program_id(1) == 0)
    def _():
        x = x_ref[...]
        r = lax.rsqrt(jnp.mean(x * x, axis=-1, keepdims=True) + EPS)
        u = x * r * g_ref[...]
        u_scr[...] = (u * (1.0 + scale_ref[0]) + shift_ref[0]).astype(BF16)

    o_ref[...] = jnp.dot(u_scr[...], w_ref[...].astype(BF16), preferred_element_type=F32).astype(o_ref.dtype)


def _cond_index(trunk, tm):
    batch, seq_len, cond0, per_seq = trunk
    if per_seq:
        return lambda i: cond0 + (i * tm) // seq_len
    return lambda i: cond0


def _cond_row_tile(trunk, pref):
    batch, seq_len, _, per_seq = trunk
    if per_seq:
        return _tile(seq_len, pref, 16)
    return _tile(batch * seq_len, pref, 16)


def _inproj(x, norm_g, mod, trunk, w, out_dtype):
    t, d = x.shape
    n = w.shape[1]
    tm = _cond_row_tile(trunk, 2048)
    tn = _tile(n, 512, 128)
    cidx = _cond_index(trunk, tm)
    return pl.pallas_call(
        _inproj_kernel,
        grid=(t // tm, n // tn),
        in_specs=[pl.BlockSpec((tm, d), lambda i, j: (i, 0), pipeline_mode=pl.Buffered(1)),
                  pl.BlockSpec((1, d), lambda i, j: (0, 0)),
                  pl.BlockSpec((1, 1, d), lambda i, j: (cidx(i), 0, 0)),
                  pl.BlockSpec((1, 1, d), lambda i, j: (cidx(i), 0, 1)),
                  pl.BlockSpec((d, tn), lambda i, j: (0, j))],
        out_specs=pl.BlockSpec((tm, tn), lambda i, j: (i, j)),
        out_shape=jax.ShapeDtypeStruct((t, n), out_dtype),
        scratch_shapes=[pltpu.VMEM((tm, d), BF16)],
        compiler_params=_cparams(("arbitrary", "arbitrary")),
        name="in_proj",
    )(x, norm_g.reshape(1, d), mod, mod, w)


def _outproj_epilogue(acc, x_ref, g_ref, gate_ref, o_ref):
    y = acc
    r = lax.rsqrt(jnp.mean(y * y, axis=-1, keepdims=True) + EPS)
    o_ref[...] = x_ref[...] + gate_ref[0] * (y * r * g_ref[...])


def _outproj_kernel(y_ref, w_ref, x_ref, g_ref, gate_ref, o_ref, acc):
    k = pl.program_id(1)

    @pl.when(k == 0)
    def _():
        acc[...] = jnp.zeros_like(acc)

    acc[...] += jnp.dot(y_ref[...], w_ref[...].astype(BF16), preferred_element_type=F32)

    @pl.when(k == pl.num_programs(1) - 1)
    def _():
        _outproj_epilogue(acc[...], x_ref, g_ref, gate_ref, o_ref)


def _outproj(y, w, x, norm_g, mod, trunk):
    t, e = y.shape
    d = w.shape[1]
    tm = _cond_row_tile(trunk, 1024)
    tk = _tile(e, 512, 128)
    cidx = _cond_index(trunk, tm)
    return pl.pallas_call(
        _outproj_kernel,
        grid=(t // tm, e // tk),
        in_specs=[pl.BlockSpec((tm, tk), lambda i, k: (i, k)),
                  pl.BlockSpec((tk, d), lambda i, k: (k, 0)),
                  pl.BlockSpec((tm, d), lambda i, k: (i, 0), pipeline_mode=pl.Buffered(1)),
                  pl.BlockSpec((1, d), lambda i, k: (0, 0)),
                  pl.BlockSpec((1, 1, d), lambda i, k: (cidx(i), 0, 2))],
        out_specs=pl.BlockSpec((tm, d), lambda i, k: (i, 0)),
        out_shape=jax.ShapeDtypeStruct((t, d), F32),
        scratch_shapes=[pltpu.VMEM((tm, d), F32)],
        compiler_params=_cparams(("arbitrary", "arbitrary")),
        name="out_proj",
    )(y, w, x, norm_g.reshape(1, d), mod)


def _conv3(x, prev_row, next_row, w, b, row0, seq_len):
    tm = x.shape[0]
    rows = lax.broadcasted_iota(jnp.int32, x.shape, 0)
    pos = (rows + row0) % seq_len
    prev = jnp.where(rows == 0, prev_row, pltpu.roll(x, 1, 0))
    nxt = jnp.where(rows == tm - 1, next_row, pltpu.roll(x, tm - 1, 0))
    prev = jnp.where(pos == 0, 0.0, prev)
    nxt = jnp.where(pos == seq_len - 1, 0.0, nxt)
    return w[0:1] * prev + w[1:2] * x + w[2:3] * nxt + b


def _halo_specs(tm, tc, nrows, colblk):
    nh = nrows // HALO
    r = tm // HALO
    return [pl.BlockSpec((tm, tc), lambda i, j: (i, colblk(j))),
            pl.BlockSpec((HALO, tc), lambda i, j: (jnp.maximum(i * r - 1, 0), colblk(j))),
            pl.BlockSpec((HALO, tc), lambda i, j: (jnp.minimum((i + 1) * r, nh - 1), colblk(j)))]


def _hy_inproj_kernel(seq_len, x_ref, xp_ref, xn_ref, g_ref, shift_ref, scale_ref,
                      w0_ref, w1_ref, w2_ref, wz_ref, cw, cb, vx_ref, g0_ref, u_scr):
    tm = x_ref.shape[0]
    ext = tm + 2 * HALO

    @pl.when(pl.program_id(1) == 0)
    def _():
        def modulated(x):
            r = lax.rsqrt(jnp.mean(x * x, axis=-1, keepdims=True) + EPS)
            return ((x * r * g_ref[...]) * (1.0 + scale_ref[0]) + shift_ref[0]).astype(BF16)

        u_scr[0:HALO] = modulated(xp_ref[...])
        u_scr[HALO:HALO + tm] = modulated(x_ref[...])
        u_scr[HALO + tm:ext] = modulated(xn_ref[...])

    u = u_scr[...]
    rows = lax.broadcasted_iota(jnp.int32, (tm, 1), 0)
    pos = (rows + pl.program_id(0) * tm) % seq_len
    first = pos == 0
    last = pos == seq_len - 1

    def conv(w_ref, k):
        r = jnp.dot(u, w_ref[...].astype(BF16), preferred_element_type=F32)
        prev = jnp.where(first, 0.0, pltpu.roll(r, 1, 0)[HALO:HALO + tm])
        nxt = jnp.where(last, 0.0, pltpu.roll(r, ext - 1, 0)[HALO:HALO + tm])
        w = cw[k]
        return w[0:1] * prev + w[1:2] * r[HALO:HALO + tm] + w[2:3] * nxt + cb[k]

    x1c = conv(w1_ref, 1)
    vc = conv(w2_ref, 2)
    vx_ref[...] = (vc * x1c).astype(vx_ref.dtype)
    x0c = conv(w0_ref, 0)
    z = jnp.dot(u_scr[HALO:HALO + tm], wz_ref[...].astype(BF16), preferred_element_type=F32)
    g0_ref[...] = (x0c * _silu(z)).astype(g0_ref.dtype)


def _hy_inproj(x, norm_g, mod, trunk, w, conv_w, conv_b, e):
    t, d = x.shape
    seq_len = trunk[1]
    tm = _cond_row_tile(trunk, 1024)
    tc = _tile(e, 256, 128)
    nb = e // tc
    cidx = _cond_index(trunk, tm)
    nh = t // HALO
    r = tm // HALO
    wspec = lambda part: pl.BlockSpec((d, tc), lambda i, j: (0, part * nb + j))
    cw = conv_w.reshape(3, 3, e).transpose(1, 0, 2)
    cb = conv_b.reshape(3, 1, e)
    return pl.pallas_call(
        functools.partial(_hy_inproj_kernel, seq_len),
        grid=(t // tm, nb),
        in_specs=[pl.BlockSpec((tm, d), lambda i, j: (i, 0)),
                  pl.BlockSpec((HALO, d), lambda i, j: (jnp.maximum(i * r - 1, 0), 0)),
                  pl.BlockSpec((HALO, d), lambda i, j: (jnp.minimum((i + 1) * r, nh - 1), 0)),
                  pl.BlockSpec((1, d), lambda i, j: (0, 0)),
                  pl.BlockSpec((1, 1, d), lambda i, j: (cidx(i), 0, 0)),
                  pl.BlockSpec((1, 1, d), lambda i, j: (cidx(i), 0, 1)),
                  wspec(0), wspec(1), wspec(2), wspec(3),
                  pl.BlockSpec((3, 3, tc), lambda i, j: (0, 0, j)),
                  pl.BlockSpec((3, 1, tc), lambda i, j: (0, 0, j))],
        out_specs=[pl.BlockSpec((tm, tc), lambda i, j: (i, j))] * 2,
        out_shape=[jax.ShapeDtypeStruct((t, e), BF16), jax.ShapeDtypeStruct((t, e), BF16)],
        scratch_shapes=[pltpu.VMEM((tm + 2 * HALO, d), BF16)],
        compiler_params=_cparams(("arbitrary", "arbitrary")),
        name="hy_in_proj_gate",
    )(x, x, x, norm_g.reshape(1, d), mod, mod, w, w, w, w, cw, cb)


def _filter_tables(seq_len):
    l = seq_len
    t = np.linspace(0.0, 1.0, l)
    w = 2.0 * math.pi * np.arange(l) / l
    f = np.linspace(1e-4, N_BANDS - 1, N_BANDS)
    z = np.concatenate([t[:, None], np.cos(f[None] * w[:, None]), -np.sin(f[None] * w[:, None])], axis=-1)
    pos = np.concatenate([np.arange(l), [0], np.arange(l - 1, 0, -1)])
    z2 = np.zeros((2 * l, 128), np.float32)
    z2[:, :z.shape[1]] = z[pos]
    aux = np.zeros((2 * l, 128), np.float32)
    aux[:, 0] = t[pos]
    aux[:, 1] = 1.0
    aux[l, 1] = 0.0
    return jnp.asarray(z2), jnp.asarray(aux)


def _filter_ffn_kernel(z_ref, w1, b1, w2, b2, w3, b3, fr, o_ref):
    f = fr[...]
    h = jnp.sin(f * (jnp.dot(z_ref[...], w1[...], precision=HI, preferred_element_type=F32) + b1[...]))
    h = jnp.sin(f * (jnp.dot(h, w2[...], precision=HI, preferred_element_type=F32) + b2[...]))
    h = jnp.sin(f * (jnp.dot(h, w3[...], precision=HI, preferred_element_type=F32) + b3[...]))
    o_ref[...] = h.astype(o_ref.dtype)


def _filter_expand_kernel(h_ref, aux_ref, w4, dec, o_ref):
    k = jnp.dot(h_ref[...], w4[...].astype(BF16), preferred_element_type=F32)
    t = aux_ref[:, 0:1]
    keep = aux_ref[:, 1:2]
    o_ref[...] = k * jnp.exp(-t * jnp.abs(dec[0])) * keep


def _filters(seq_len, w1, b1, w2, b2, w3, b3, w4, freq, decay):
    e = decay.shape[-1]
    fo = w2.shape[0]
    z2, aux = _filter_tables(seq_len)
    w1p = jnp.zeros((128, fo), F32).at[:w1.shape[0]].set(w1)
    tr = _tile(seq_len, 1024, 8)
    tc = _tile(e, 2048, 128)
    nb = e // tc
    nr = seq_len // tr
    small = lambda a: pl.BlockSpec(a.shape, lambda *_: (0,) * a.ndim)
    b1r, b2r, b3r, frr = (a.reshape(1, fo) for a in (b1, b2, b3, freq))
    h3 = pl.pallas_call(
        _filter_ffn_kernel,
        grid=(2 * nr,),
        in_specs=[pl.BlockSpec((tr, 128), lambda i: (i, 0)),
                  small(w1p), small(b1r), small(w2), small(b2r), small(w3), small(b3r), small(frr)],
        out_specs=pl.BlockSpec((tr, fo), lambda i: (i, 0)),
        out_shape=jax.ShapeDtypeStruct((2 * seq_len, fo), BF16),
        compiler_params=_cparams(("arbitrary",)),
        name="hy_filter_ffn",
    )(z2, w1p, b1r, w2, b2r, w3, b3r, frr)
    return pl.pallas_call(
        _filter_expand_kernel,
        grid=(2 * nr, nb),
        in_specs=[pl.BlockSpec((tr, fo), lambda i, j: (i, 0)),
                  pl.BlockSpec((tr, 128), lambda i, j: (i, 0)),
                  pl.BlockSpec((fo, tc), lambda i, j: (0, (i // nr) * nb + j)),
                  pl.BlockSpec((1, 1, tc), lambda i, j: (i // nr, 0, j))],
        out_specs=pl.BlockSpec((tr, tc), lambda i, j: (i, j)),
        out_shape=jax.ShapeDtypeStruct((2 * seq_len, e), F32),
        compiler_params=_cparams(("arbitrary", "arbitrary")),
        name="hy_filter_expand",
    )(h3, aux, w4, decay.reshape(2, 1, e))


def _cmm_kernel(w_ref, x_ref, o_ref):
    o_ref[0] = jnp.dot(w_ref[...], x_ref[0], preferred_element_type=F32).astype(o_ref.dtype)


def _cmm(w, x, out_dtype):
    m, k = w.shape
    b, _, n = x.shape
    tn = _tile(n, 4096, 128)
    return pl.pallas_call(
        _cmm_kernel,
        grid=(b, n // tn),
        in_specs=[pl.BlockSpec((m, k), lambda bi, j: (0, 0)),
                  pl.BlockSpec((1, k, tn), lambda bi, j: (bi, 0, j))],
        out_specs=pl.BlockSpec((1, m, tn), lambda bi, j: (bi, 0, j)),
        out_shape=jax.ShapeDtypeStruct((b, m, n), out_dtype),
        compiler_params=_cparams(("arbitrary", "arbitrary")),
        name="const_lhs_matmul",
    )(w, x)


def _direct_mats(seq_len):
    l = seq_len
    n = 2 * l
    f = np.arange(l)[:, None]
    t = np.arange(n)[None, :]
    ang = 2.0 * math.pi * ((f * t) % n) / n
    fwd = np.concatenate([np.cos(ang), -np.sin(ang)], axis=0)
    fwd[l] = np.cos(math.pi * np.arange(n))
    tt = np.arange(l)[:, None]
    ff = np.arange(l)[None, :]
    ang2 = 2.0 * math.pi * ((tt * ff) % n) / n
    wgt = np.where(ff == 0, 1.0, 2.0) / n
    inv = np.concatenate([wgt * np.cos(ang2), -wgt * np.sin(ang2)], axis=1)
    inv[:, l] = np.cos(math.pi * np.arange(l)) / n
    return fwd, inv


def _direct_conv_kernel(seq_len, f_ref, v_ref, x_ref, k_ref, g0_ref, bias_ref, o_ref):
    l = seq_len
    x = x_ref[0]
    s = jnp.dot(f_ref[...], x, preferred_element_type=F32)
    sre, sim = s[:l], s[l:]
    kre, kim = k_ref[:l, :], k_ref[l:, :]
    row0 = lax.broadcasted_iota(jnp.int32, sre.shape, 0) == 0
    yre = sre * kre - jnp.where(row0, 0.0, sim * kim)
    yim = jnp.where(row0, sim * kim, sre * kim + sim * kre)
    y = jnp.concatenate([yre, yim], axis=0).astype(BF16)
    out = jnp.dot(v_ref[...], y, preferred_element_type=F32)
    out = out + x.astype(F32) * bias_ref[...]
    o_ref[0] = (out * g0_ref[0].astype(F32)).astype(o_ref.dtype)


def _longconv_direct(vx, g0, kfilt, bias, batch, seq_len):
    e = vx.shape[-1]
    l = seq_len
    fwd, inv = _direct_mats(l)
    f_full = jnp.asarray(fwd, BF16)
    f_data = jnp.asarray(fwd[:, :l], BF16)
    v_mat = jnp.asarray(inv, BF16)
    kspec = _cmm(f_full, kfilt.astype(BF16)[None], F32)[0]
    tc = _tile(e, 1024, 128)
    out = pl.pallas_call(
        functools.partial(_direct_conv_kernel, l),
        grid=(e // tc, batch),
        in_specs=[pl.BlockSpec((2 * l, l), lambda j, b: (0, 0)),
                  pl.BlockSpec((l, 2 * l), lambda j, b: (0, 0)),
                  pl.BlockSpec((1, l, tc), lambda j, b: (b, 0, j)),
                  pl.BlockSpec((2 * l, tc), lambda j, b: (0, j)),
                  pl.BlockSpec((1, l, tc), lambda j, b: (b, 0, j)),
                  pl.BlockSpec((1, tc), lambda j, b: (0, j))],
        out_specs=pl.BlockSpec((1, l, tc), lambda j, b: (b, 0, j)),
        out_shape=jax.ShapeDtypeStruct((batch, l, e), BF16),
        compiler_params=_cparams(("arbitrary", "arbitrary")),
        name="longconv_direct",
    )(f_data, v_mat, vx.reshape(batch, l, e), kspec, g0.reshape(batch, l, e), bias.reshape(1, e))
    return out.reshape(batch * l, e)


def _two_level_mats(seq_len):
    n = 2 * seq_len
    n2 = FFT_N2
    n1 = n // n2
    h2 = n2 // 2
    ns = 8 * ((h2 + 1 + 7) // 8)
    f2 = np.arange(h2 + 1)[:, None]
    t2 = np.arange(n2)[None, :]
    ang = 2.0 * math.pi * ((f2 * t2) % n2) / n2
    f1m = np.zeros((2 * ns, n2))
    f1m[0:2 * (h2 + 1):2] = np.cos(ang)
    f1m[1:2 * (h2 + 1):2] = -np.sin(ang)
    wgt = np.where((f2 == 0) | (f2 == h2), 1.0, 2.0) / n
    g1m = np.zeros((h2, 2 * ns))
    g1m[:, 0:2 * (h2 + 1):2] = (wgt * np.cos(ang[:, :h2])).T
    g1m[:, 1:2 * (h2 + 1):2] = -(wgt * np.sin(ang[:, :h2])).T
    t1 = np.arange(n1)[None, :]
    f1 = np.arange(n1)[:, None]
    rm = np.zeros((ns, 2 * n1, 2 * n1))
    pm = np.zeros((ns, 2 * n1, 2 * n1))
    for s in range(h2 + 1):
        a = 2.0 * math.pi * ((t1 * (n2 * f1 + s)) % n) / n
        mr, mi = np.cos(a), -np.sin(a)
        rm[s] = np.block([[mr, -mi], [mi, mr]])
        pr, pi = mr.T, -mi.T
        pm[s] = np.block([[pr, -pi], [pi, pr]])
    return n1, n2, ns, f1m, g1m, rm, pm


def _slab_spec_kernel(r_ref, a_ref, o_ref):
    n1 = a_ref.shape[3]
    a = jnp.concatenate([a_ref[0, 0, 0], a_ref[0, 0, 1]], axis=0)
    s = jnp.dot(r_ref[0], a, preferred_element_type=F32)
    o_ref[0, 0] = s[:n1]
    o_ref[0, 1] = s[n1:]


def _slab_conv_kernel(r_ref, p_ref, a_ref, k_ref, o_ref):
    n1 = a_ref.shape[3]
    a = jnp.concatenate([a_ref[0, 0, 0], a_ref[0, 0, 1]], axis=0)
    s = jnp.dot(r_ref[0], a, preferred_element_type=F32)
    sre, sim = s[:n1], s[n1:]
    kre, kim = k_ref[0, 0], k_ref[0, 1]
    y = jnp.concatenate([sre * kre - sim * kim, sre * kim + sim * kre], axis=0).astype(BF16)
    o = jnp.dot(p_ref[0], y, preferred_element_type=F32)
    o_ref[0, 0, 0] = o[:n1].astype(o_ref.dtype)
    o_ref[0, 0, 1] = o[n1:].astype(o_ref.dtype)


def _dft_l1_kernel(k_ref, x_ref, o_ref):
    x = x_ref[0].astype(F32)
    nt2, _, tc = x.shape
    halves = []
    for g in range(SUB_TILES):
        xg = x[:, SUBLANES * g:SUBLANES * (g + 1), :].reshape(nt2 * SUBLANES, tc).astype(BF16)
        a = jnp.dot(k_ref[...], xg, preferred_element_type=F32)
        halves.append(a.reshape(-1, SUBLANES, tc))
    o_ref[0] = jnp.concatenate(halves, axis=1).astype(o_ref.dtype)


def _dft_l1_inv_kernel(g_ref, a_ref, vx_ref, g0_ref, bias_ref, o_ref):
    a = a_ref[0].astype(F32)
    nr, _, tc = a.shape
    halves = []
    for g in range(SUB_TILES):
        ag = a[:, SUBLANES * g:SUBLANES * (g + 1), :].reshape(nr * SUBLANES, tc).astype(BF16)
        y = jnp.dot(g_ref[...], ag, preferred_element_type=F32)
        halves.append(y.reshape(-1, SUBLANES, tc))
    y = jnp.concatenate(halves, axis=1)
    y = y + vx_ref[0].astype(F32) * bias_ref[...]
    o_ref[0] = (y * g0_ref[0].astype(F32)).astype(o_ref.dtype)


def _dft_l1(kmat, x4, out_dtype):
    b, nt2, n1, e = x4.shape
    m = kmat.shape[0] // SUBLANES
    tc = _tile(e, 1024, 128)
    rows = SUBLANES * SUB_TILES
    return pl.pallas_call(
        _dft_l1_kernel,
        grid=(b, n1 // rows, e // tc),
        in_specs=[pl.BlockSpec(kmat.shape, lambda bi, i, j: (0, 0)),
                  pl.BlockSpec((1, nt2, rows, tc), lambda bi, i, j: (bi, 0, i, j))],
        out_specs=pl.BlockSpec((1, m, rows, tc), lambda bi, i, j: (bi, 0, i, j)),
        out_shape=jax.ShapeDtypeStruct((b, m, n1, e), out_dtype),
        compiler_params=_cparams(("arbitrary", "arbitrary", "arbitrary")),
        name="dft_level1",
    )(kmat, x4)


def _longconv_two_level(vx, g0, kfilt, bias, batch, seq_len):
    e = vx.shape[-1]
    l = seq_len
    n1, n2, ns, f1m, g1m, rm, pm = _two_level_mats(l)
    h2 = n2 // 2
    eye = np.eye(SUBLANES)
    k_full = jnp.asarray(np.kron(f1m, eye), BF16)
    k_data = jnp.asarray(np.kron(f1m[:, :h2], eye), BF16)
    g8 = jnp.asarray(np.kron(g1m, eye), BF16)
    rmat = jnp.asarray(rm, BF16)
    pmat = jnp.asarray(pm, BF16)
    tc = _tile(e, 4096, 128)
    nc = e // tc

    ka = _dft_l1(k_full, kfilt.reshape(1, n2, n1, e), BF16).reshape(1, ns, 2, n1, e)
    kspec = pl.pallas_call(
        _slab_spec_kernel,
        grid=(ns, nc),
        in_specs=[pl.BlockSpec((1, 2 * n1, 2 * n1), lambda s, j: (s, 0, 0)),
                  pl.BlockSpec((1, 1, 2, n1, tc), lambda s, j: (0, s, 0, 0, j))],
        out_specs=pl.BlockSpec((1, 2, n1, tc), lambda s, j: (s, 0, 0, j)),
        out_shape=jax.ShapeDtypeStruct((ns, 2, n1, e), F32),
        compiler_params=_cparams(("arbitrary", "arbitrary")),
        name="filter_slab_dft",
    )(rmat, ka)

    vx4 = vx.reshape(batch, h2, n1, e)
    a = _dft_l1(k_data, vx4, BF16).reshape(batch, ns, 2, n1, e)
    a2 = pl.pallas_call(
        _slab_conv_kernel,
        grid=(ns, batch, nc),
        in_specs=[pl.BlockSpec((1, 2 * n1, 2 * n1), lambda s, b, j: (s, 0, 0)),
                  pl.BlockSpec((1, 2 * n1, 2 * n1), lambda s, b, j: (s, 0, 0)),
                  pl.BlockSpec((1, 1, 2, n1, tc), lambda s, b, j: (b, s, 0, 0, j)),
                  pl.BlockSpec((1, 2, n1, tc), lambda s, b, j: (s, 0, 0, j))],
        out_specs=pl.BlockSpec((1, 1, 2, n1, tc), lambda s, b, j: (b, s, 0, 0, j)),
        out_shape=jax.ShapeDtypeStruct((batch, ns, 2, n1, e), BF16),
        compiler_params=_cparams(("arbitrary", "arbitrary", "arbitrary")),
        name="slab_conv",
    )(rmat, pmat, a, kspec)
    a2 = a2.reshape(batch, 2 * ns, n1, e)
    rows = SUBLANES * SUB_TILES
    tci = _tile(e, 1024, 128)
    blk = lambda r: pl.BlockSpec((1, r, rows, tci), lambda bi, i, j: (bi, 0, i, j))
    out = pl.pallas_call(
        _dft_l1_inv_kernel,
        grid=(batch, n1 // rows, e // tci),
        in_specs=[pl.BlockSpec(g8.shape, lambda bi, i, j: (0, 0)),
                  blk(2 * ns), blk(h2), blk(h2),
                  pl.BlockSpec((1, 1, tci), lambda bi, i, j: (0, 0, j))],
        out_specs=blk(h2),
        out_shape=jax.ShapeDtypeStruct((batch, h2, n1, e), BF16),
        compiler_params=_cparams(("arbitrary", "arbitrary", "arbitrary")),
        name="dft_level1_inverse",
    )(g8, a2, vx4, g0.reshape(batch, h2, n1, e), bias.reshape(1, 1, e))
    return out.reshape(batch * l, e)


def _longconv(vx, g0, kfilt, bias, batch, seq_len):
    if seq_len <= DIRECT_CONV_MAX_L:
        return _longconv_direct(vx, g0, kfilt, bias, batch, seq_len)
    return _longconv_two_level(vx, g0, kfilt, bias, batch, seq_len)


def _ml_qkv_kernel(seq_len, n_heads, dk, x_ref, xp_ref, xn_ref, cw_ref, cb_ref, wq_ref, wk_ref, wv_ref,
                   wgq_ref, wgk_ref, wgv_ref, bg_ref, q_ref, k_ref, kt_ref, v_ref, xc_ref, g_ref, gt_ref, gacc):
    h = pl.program_id(1)
    tm = x_ref.shape[0]
    q_len = gt_ref.shape[2]
    x = x_ref[...].astype(F32)
    conv = _conv3(x, xp_ref[HALO - 1:HALO, :].astype(F32), xn_ref[0:1, :].astype(F32),
                  cw_ref[...], cb_ref[...], pl.program_id(0) * tm, seq_len)
    xc = _silu(conv)
    xc_ref[...] = xc.astype(xc_ref.dtype)
    xcb = xc.astype(BF16)
    q = jnp.dot(xcb, wq_ref[0], preferred_element_type=F32).astype(BF16)
    kf = jnp.dot(xcb, wk_ref[0], preferred_element_type=F32) * (dk ** -0.5)
    k = kf.astype(BF16)
    v = jnp.dot(x.astype(BF16), wv_ref[0], preferred_element_type=F32).astype(BF16)
    q_ref[...] = q
    k_ref[...] = k
    v_ref[...] = v
    for c in range(tm // q_len):
        kt_ref[c] = kf[c * q_len:(c + 1) * q_len, :].T.astype(BF16)
    part = (jnp.dot(q, wgq_ref[0], preferred_element_type=F32)
            + jnp.dot(k, wgk_ref[0], preferred_element_type=F32)
            + jnp.dot(v, wgv_ref[0], preferred_element_type=F32))

    @pl.when(h == 0)
    def _():
        gacc[...] = part + bg_ref[...]

    @pl.when(h > 0)
    def _():
        gacc[...] += part

    @pl.when(h == n_heads - 1)
    def _():
        g = gacc[...]
        col = lax.broadcasted_iota(jnp.int32, g.shape, 1)
        is_forget = (col % (2 * n_heads)) >= n_heads
        g = jnp.where(is_forget, jax.nn.log_sigmoid(g), g)
        col_q = lax.broadcasted_iota(jnp.int32, (q_len, g.shape[1]), 1)
        forget_q = (col_q % (2 * n_heads)) >= n_heads
        bwd_q = (col_q // (2 * n_heads)) == 1
        r_i = lax.broadcasted_iota(jnp.int32, (q_len, q_len), 0)
        c_i = lax.broadcasted_iota(jnp.int32, (q_len, q_len), 1)
        tril = (c_i <= r_i).astype(F32)
        for c in range(tm // q_len):
            gc = g[c * q_len:(c + 1) * q_len, :]
            pre = jnp.dot(tril, gc, precision=HI, preferred_element_type=F32)
            tot = jnp.sum(gc, axis=0, keepdims=True)
            suf = tot - pre + gc
            gc = jnp.where(forget_q, jnp.where(bwd_q, suf, pre), gc)
            g_ref[c * q_len:(c + 1) * q_len, :] = gc
            gt_ref[c] = gc.T


def _ml_qkv(p, conv_w, conv_b, wq, wk, wv, w_gate, b_gate, batch, seq_len, e):
    t = p.shape[0]
    n_heads, dh, dk = wq.shape
    dv = wv.shape[2]
    q_len = min(SCAN_CHUNK, seq_len)
    tm = _row_tile(batch, seq_len, 1024, q_len)
    ng = w_gate.shape[1]
    gpad = 128
    wgq = jnp.zeros((n_heads, dk, gpad), BF16).at[:, :, :ng].set(
        w_gate[:n_heads * dk].reshape(n_heads, dk, ng).astype(BF16))
    wgk = jnp.zeros((n_heads, dk, gpad), BF16).at[:, :, :ng].set(
        w_gate[n_heads * dk:2 * n_heads * dk].reshape(n_heads, dk, ng).astype(BF16))
    wgv = jnp.zeros((n_heads, dv, gpad), BF16).at[:, :, :ng].set(
        w_gate[2 * n_heads * dk:].reshape(n_heads, dv, ng).astype(BF16))
    bg = jnp.zeros((1, gpad), F32).at[0, :ng].set(b_gate)
    specs = _halo_specs(tm, dh, t, lambda j: j)
    specs += [pl.BlockSpec((3, dh), lambda i, j: (0, j)),
              pl.BlockSpec((1, dh), lambda i, j: (0, j)),
              pl.BlockSpec((1, dh, dk), lambda i, j: (j, 0, 0)),
              pl.BlockSpec((1, dh, dk), lambda i, j: (j, 0, 0)),
              pl.BlockSpec((1, dh, dv), lambda i, j: (j, 0, 0)),
              pl.BlockSpec((1, dk, gpad), lambda i, j: (j, 0, 0)),
              pl.BlockSpec((1, dk, gpad), lambda i, j: (j, 0, 0)),
              pl.BlockSpec((1, dv, gpad), lambda i, j: (j, 0, 0)),
              pl.BlockSpec((1, gpad), lambda i, j: (0, 0))]
    return pl.pallas_call(
        functools.partial(_ml_qkv_kernel, seq_len, n_heads, dk),
        grid=(t // tm, n_heads),
        in_specs=specs,
        out_specs=[pl.BlockSpec((tm, dk), lambda i, j: (i, j)),
                   pl.BlockSpec((tm, dk), lambda i, j: (i, j)),
                   pl.BlockSpec((tm // q_len, dk, q_len), lambda i, j: (i, j, 0)),
                   pl.BlockSpec((tm, dv), lambda i, j: (i, j)),
                   pl.BlockSpec((tm, dh), lambda i, j: (i, j)),
                   pl.BlockSpec((tm, gpad), lambda i, j: (i, 0)),
                   pl.BlockSpec((tm // q_len, gpad, q_len), lambda i, j: (i, 0, 0))],
        out_shape=[jax.ShapeDtypeStruct((t, n_heads * dk), BF16),
                   jax.ShapeDtypeStruct((t, n_heads * dk), BF16),
                   jax.ShapeDtypeStruct((t // q_len, n_heads * dk, q_len), BF16),
                   jax.ShapeDtypeStruct((t, n_heads * dv), BF16),
                   jax.ShapeDtypeStruct((t, e), BF16),
                   jax.ShapeDtypeStruct((t, gpad), F32),
                   jax.ShapeDtypeStruct((t // q_len, gpad, q_len), F32)],
        scratch_shapes=[pltpu.VMEM((tm, gpad), F32)],
        compiler_params=_cparams(("arbitrary", "arbitrary")),
        name="ml_qkv_gates",
    )(p, p, p, conv_w, conv_b.reshape(1, e), wq.astype(BF16), wk.astype(BF16), wv.astype(BF16),
      wgq, wgk, wgv, bg)


def _ml_scan_kernel(n_heads, has_init, *refs):
    if has_init:
        (q_ref, k_ref, kt_ref, v_ref, g_ref, gt_ref, c0_ref, n0_ref, m0_ref,
         h_ref, cout_ref, nout_ref, mout_ref, c_scr, n_scr, m_scr) = refs
    else:
        (q_ref, k_ref, kt_ref, v_ref, g_ref, gt_ref,
         h_ref, cout_ref, nout_ref, mout_ref, c_scr, n_scr, m_scr) = refs
    head = pl.program_id(1)
    seq_len = q_ref.shape[0]
    q_len = gt_ref.shape[2]
    n_chunks = seq_len // q_len

    for d in range(2):
        if has_init:
            c_scr[d] = c0_ref[0, 0, d, 0]
            n_scr[d] = n0_ref[0, 0, d]
            m_scr[d] = m0_ref[0, 0, d]
        else:
            c_scr[d] = jnp.zeros(c_scr.shape[1:], F32)
            n_scr[d] = jnp.zeros(n_scr.shape[1:], F32)
            m_scr[d] = jnp.zeros(m_scr.shape[1:], F32)
    h_ref[...] = jnp.zeros_like(h_ref)

    row_i = lax.broadcasted_iota(jnp.int32, (q_len, q_len), 0)
    col_i = lax.broadcasted_iota(jnp.int32, (q_len, q_len), 1)
    lane_g = lax.broadcasted_iota(jnp.int32, (q_len, g_ref.shape[1]), 1)

    def direction(d, c):
        icol = d * 2 * n_heads + head
        fcol = icol + n_heads
        rows = pl.ds(pl.multiple_of(c * q_len, q_len), q_len)
        g = g_ref[rows, :]
        i_col = jnp.sum(jnp.where(lane_g == icol, g, 0.0), axis=1, keepdims=True)
        b_col = jnp.sum(jnp.where(lane_g == fcol, g, 0.0), axis=1, keepdims=True)
        i_row = gt_ref[c, pl.ds(icol, 1), :]
        b_row = gt_ref[c, pl.ds(fcol, 1), :]
        mask = (col_i <= row_i) if d == 0 else (col_i >= row_i)
        last = q_len - 1 if d == 0 else 0
        g_tot = jnp.sum(jnp.where(col_i[0:1, :] == last, b_row, 0.0), axis=1, keepdims=True)
        m_prev = m_scr[d][:, 0:1]
        n_prev = n_scr[d]
        c_prev = c_scr[d]

        qc = q_ref[rows, :]
        kc = k_ref[rows, :]
        vc = v_ref[rows, :]
        dmat = jnp.where(mask, b_col - b_row + i_row, -jnp.inf)
        inter = b_col + m_prev
        m_t = jnp.maximum(inter, jnp.max(dmat, axis=1, keepdims=True))
        qk = lax.dot_general(qc, kc, (((1,), (1,)), ((), ())), preferred_element_type=F32)
        s = qk * jnp.exp(dmat - m_t)
        w_inter = jnp.exp(inter - m_t)
        num = (w_inter * jnp.dot(qc, c_prev.astype(BF16), preferred_element_type=F32)
               + jnp.dot(s.astype(BF16), vc, preferred_element_type=F32))
        qn = (w_inter * jnp.sum(qc.astype(F32) * n_prev, axis=1, keepdims=True)
              + jnp.sum(s, axis=1, keepdims=True))
        den = jnp.maximum(jnp.abs(qn), jnp.exp(-m_t))
        h_ref[rows, :] += num / den

        a_col = g_tot - b_col + i_col
        a_row = g_tot - b_row + i_row
        m_new = jnp.maximum(g_tot + m_prev, jnp.max(a_row, axis=1, keepdims=True))
        dec = jnp.exp(g_tot + m_prev - m_new)
        kwt = (kt_ref[c].astype(F32) * jnp.exp(a_row - m_new)).astype(BF16)
        c_scr[d] = dec * c_prev + jnp.dot(kwt, vc, preferred_element_type=F32)
        n_scr[d] = dec * n_prev + jnp.sum(kc.astype(F32) * jnp.exp(a_col - m_new), axis=0, keepdims=True)
        m_scr[d] = jnp.broadcast_to(m_new, m_scr.shape[1:])

    def body(j, carry):
        direction(0, j)
        direction(1, n_chunks - 1 - j)
        return carry

    lax.fori_loop(0, n_chunks, body, 0)
    for d in range(2):
        cout_ref[0, 0, d, 0] = c_scr[d]
        nout_ref[0, 0, d] = n_scr[d]
        mout_ref[0, 0, d] = m_scr[d]


def _ml_scan(q, k, kt, v, g, gt, batch, seq_len, n_heads, init=None):
    dk = q.shape[1] // n_heads
    dv = v.shape[1] // n_heads
    q_len = gt.shape[2]
    gpad = g.shape[1]
    nq = seq_len // q_len
    in_specs = [pl.BlockSpec((seq_len, dk), lambda b, h: (b, h)),
                pl.BlockSpec((seq_len, dk), lambda b, h: (b, h)),
                pl.BlockSpec((nq, dk, q_len), lambda b, h: (b, h, 0)),
                pl.BlockSpec((seq_len, dv), lambda b, h: (b, h)),
                pl.BlockSpec((seq_len, gpad), lambda b, h: (b, 0)),
                pl.BlockSpec((nq, gpad, q_len), lambda b, h: (b, 0, 0))]
    args = [q, k, kt, v, g, gt]
    if init is not None:
        c0, n0, m0 = init
        in_specs += [pl.BlockSpec((1, 1, 2, 1, dk, dv), lambda b, h: (b, 0, 0, h, 0, 0)),
                     pl.BlockSpec((1, 1, 2, 1, dk), lambda b, h: (b, h, 0, 0, 0)),
                     pl.BlockSpec((1, 1, 2, 1, 128), lambda b, h: (b, h, 0, 0, 0))]
        args += [c0, n0, m0]
    out = pl.pallas_call(
        functools.partial(_ml_scan_kernel, n_heads, init is not None),
        grid=(batch, n_heads),
        in_specs=in_specs,
        out_specs=[pl.BlockSpec((seq_len, dv), lambda b, h: (b, h)),
                   pl.BlockSpec((1, 1, 2, 1, dk, dv), lambda b, h: (b, 0, 0, h, 0, 0)),
                   pl.BlockSpec((1, 1, 2, 1, dk), lambda b, h: (b, h, 0, 0, 0)),
                   pl.BlockSpec((1, 1, 2, 1, 128), lambda b, h: (b, h, 0, 0, 0))],
        out_shape=[jax.ShapeDtypeStruct((batch * seq_len, n_heads * dv), F32),
                   jax.ShapeDtypeStruct((batch, 1, 2, n_heads, dk, dv), F32),
                   jax.ShapeDtypeStruct((batch, n_heads, 2, 1, dk), F32),
                   jax.ShapeDtypeStruct((batch, n_heads, 2, 1, 128), F32)],
        scratch_shapes=[pltpu.VMEM((2, dk, dv), F32), pltpu.VMEM((2, 1, dk), F32),
                        pltpu.VMEM((2, 1, 128), F32)],
        compiler_params=_cparams(("arbitrary", "arbitrary")),
        name="ml_scan",
    )(*args)
    return out


def _ml_out_kernel(h_ref, o_ref_in, xc_ref, z_ref, nw_ref, sk_ref, w_ref, x_ref, g_ref, gate_ref, out_ref, acc):
    k = pl.program_id(1)

    @pl.when(k == 0)
    def _():
        acc[...] = jnp.zeros_like(acc)

    hh = jax.nn.sigmoid(o_ref_in[...].astype(F32)) * h_ref[...]
    mu = jnp.mean(hh, axis=-1, keepdims=True)
    var = jnp.mean(jnp.square(hh - mu), axis=-1, keepdims=True)
    hn = (hh - mu) * lax.rsqrt(var + EPS) * nw_ref[...]
    hn = hn + sk_ref[...] * xc_ref[...].astype(F32)
    y = (hn * _silu(z_ref[...].astype(F32))).astype(BF16)
    acc[...] += jnp.dot(y, w_ref[...].astype(BF16), preferred_element_type=F32)

    @pl.when(k == pl.num_programs(1) - 1)
    def _():
        _outproj_epilogue(acc[...], x_ref, g_ref, gate_ref, out_ref)


def _ml_out(h, p, xc, norm_w, skip, w, x, norm_g, mod, trunk, n_heads):
    t, e = h.shape
    d = w.shape[1]
    dv = e // n_heads
    tm = _cond_row_tile(trunk, 1024)
    cidx = _cond_index(trunk, tm)
    return pl.pallas_call(
        _ml_out_kernel,
        grid=(t // tm, n_heads),
        in_specs=[pl.BlockSpec((tm, dv), lambda i, k: (i, k)),
                  pl.BlockSpec((tm, dv), lambda i, k: (i, 2 * n_heads + k)),
                  pl.BlockSpec((tm, dv), lambda i, k: (i, k)),
                  pl.BlockSpec((tm, dv), lambda i, k: (i, n_heads + k)),
                  pl.BlockSpec((1, dv), lambda i, k: (0, k)),
                  pl.BlockSpec((1, dv), lambda i, k: (0, k)),
                  pl.BlockSpec((dv, d), lambda i, k: (k, 0)),
                  pl.BlockSpec((tm, d), lambda i, k: (i, 0), pipeline_mode=pl.Buffered(1)),
                  pl.BlockSpec((1, d), lambda i, k: (0, 0)),
                  pl.BlockSpec((1, 1, d), lambda i, k: (cidx(i), 0, 2))],
        out_specs=pl.BlockSpec((tm, d), lambda i, k: (i, 0)),
        out_shape=jax.ShapeDtypeStruct((t, d), F32),
        scratch_shapes=[pltpu.VMEM((tm, d), F32)],
        compiler_params=_cparams(("arbitrary", "arbitrary")),
        name="ml_out_proj",
    )(h, p, xc, p, norm_w.reshape(1, e), skip.reshape(1, e), w, x, norm_g.reshape(1, d), mod)


def _hyena_layer(x, mod, trunk, norm_pre, norm_post, hw):
    (w_in, conv_w, conv_b, w1, b1, w2, b2, w3, b3, w4, freq, decay, bias, w_out) = hw
    batch, seq_len = trunk[0], trunk[1]
    e = w_out.shape[0]
    vx, g0 = _hy_inproj(x, norm_pre, mod, trunk, w_in, conv_w, conv_b, e)
    kfilt = _filters(seq_len, w1, b1, w2, b2, w3, b3, w4, freq, decay)
    y = _longconv(vx, g0, kfilt, bias, batch, seq_len)
    return _outproj(y, w_out, x, norm_post, mod, trunk)


def _mlstm_layer(x, mod, trunk, norm_pre, norm_post, mw, init):
    (w_in, conv_w, conv_b, wq, wk, wv, w_gate, b_gate, norm_w, skip, w_out) = mw
    batch, seq_len = trunk[0], trunk[1]
    e = w_out.shape[0]
    n_heads = wq.shape[0]
    p = _inproj(x, norm_pre, mod, trunk, w_in, BF16)
    q, k, kt, v, xc, g, gt = _ml_qkv(p, conv_w, conv_b, wq, wk, wv, w_gate, b_gate, batch, seq_len, e)
    h, c_new, n_new, m_new = _ml_scan(q, k, kt, v, g, gt, batch, seq_len, n_heads, init)
    x_new = _ml_out(h, p, xc, norm_w, skip, w_out.astype(BF16), x, norm_post, mod, trunk, n_heads)
    return x_new, c_new, n_new, m_new


def kernel(x_prompt, x_sample, state_C, state_n, state_m, c, c_ctx, norm_pre, norm_post, ada_w, ada_b,
           hy_w_in, hy_conv_w, hy_conv_b, hy_ffn_w1, hy_ffn_b1, hy_ffn_w2, hy_ffn_b2, hy_ffn_w3, hy_ffn_b3,
           hy_ffn_w4, hy_sin_freq, hy_decay, hy_bias, hy_w_out, ml_w_in, ml_conv_w, ml_conv_b, ml_wq, ml_wk,
           ml_wv, ml_w_gate, ml_b_gate, ml_norm, ml_skip, ml_w_out):
    depth = norm_pre.shape[0]
    bp, lp, d = x_prompt.shape
    bs, ls, _ = x_sample.shape
    n_heads = ml_wq.shape[1]

    nrow = 8 * ((1 + bs + 7) // 8)
    conds = jnp.zeros((nrow, d), F32).at[0].set(c_ctx).at[1:1 + bs].set(c)
    mods = _ada(conds, ada_w, ada_b)

    def run_trunk(x3, cond0, per_seq, init_states):
        batch, seq_len, _ = x3.shape
        trunk = (batch, seq_len, cond0, per_seq)
        x = x3.reshape(batch * seq_len, d)
        new_c, new_n, new_m = [], [], []
        for i in range(depth):
            mod = mods[i].reshape(nrow, 1, 3 * d)
            j = i // 2
            if i % 2 == 0:
                hw = (hy_w_in[j], hy_conv_w[j], hy_conv_b[j], hy_ffn_w1[j], hy_ffn_b1[j], hy_ffn_w2[j],
                      hy_ffn_b2[j], hy_ffn_w3[j], hy_ffn_b3[j], hy_ffn_w4[j], hy_sin_freq[j], hy_decay[j],
                      hy_bias[j], hy_w_out[j])
                x = _hyena_layer(x, mod, trunk, norm_pre[i], norm_post[i], hw)
            else:
                mw = (ml_w_in[j], ml_conv_w[j], ml_conv_b[j], ml_wq[j], ml_wk[j], ml_wv[j], ml_w_gate[j],
                      ml_b_gate[j], ml_norm[j], ml_skip[j], ml_w_out[j])
                init = None if init_states is None else init_states(j)
                x, cj, nj, mj = _mlstm_layer(x, mod, trunk, norm_pre[i], norm_post[i], mw, init)
                new_c.append(cj)
                new_n.append(nj)
                new_m.append(mj)
        return x.reshape(batch, seq_len, d), new_c, new_n, new_m

    y_prompt, cs, ns, ms = run_trunk(x_prompt, 0, False, None)
    dk = ml_wq.shape[3]
    new_state_c = jnp.concatenate(cs, axis=1)
    new_state_n = jnp.stack([jnp.swapaxes(n[:, :, :, 0, :], 1, 2) for n in ns], axis=1)
    new_state_m = jnp.stack([jnp.swapaxes(m[:, :, :, 0, 0], 1, 2) for m in ms], axis=1)

    def lat_init(j):
        c0 = state_C[:, j:j + 1]
        n0 = jnp.swapaxes(state_n[:, j], 1, 2)[:, :, :, None, :]
        m0 = jnp.broadcast_to(jnp.swapaxes(state_m[:, j], 1, 2)[:, :, :, None, None],
                              (bs, n_heads, 2, 1, 128))
        return c0, n0, m0

    y_sample, _, _, _ = run_trunk(x_sample, 1, True, lat_init)
    return (y_prompt, y_sample, new_state_c, new_state_n, new_state_m)
```

```python
import functools
import math

import numpy as np
import jax
import jax.numpy as jnp
from jax import lax
from jax.experimental import pallas as pl
from jax.experimental.pallas import tpu as pltpu

F32 = jnp.float32
BF16 = jnp.bfloat16
EPS = 1e-6
N_BANDS = 16
SUBLANES = 8
SUB_TILES = 2
HALO = 16
SCAN_CHUNK = 256
FFT_N2 = 128
DIRECT_CONV_MAX_L = 512
VMEM_LIMIT = 56 * 1024 * 1024
HI = lax.Precision.HIGHEST


def _cparams(sem):
    return pltpu.CompilerParams(dimension_semantics=sem, vmem_limit_bytes=VMEM_LIMIT)


def _tile(n, pref, mult):
    if n <= pref:
        return n
    t = (pref // mult) * mult
    while t >= mult:
        if n % t == 0:
            return t
        t -= mult
    return n


def _row_tile(batch, seq_len, pref, mult):
    if seq_len >= pref:
        return _tile(seq_len, pref, mult)
    k = max(1, pref // seq_len)
    while batch % k:
        k -= 1
    return seq_len * k


def _silu(x):
    return x * jax.nn.sigmoid(x)


def _ada_kernel(c_ref, w_ref, b_ref, o_ref):
    a = _silu(c_ref[...]).astype(BF16)
    o_ref[0] = jnp.dot(a, w_ref[0].astype(BF16), preferred_element_type=F32) + b_ref[0]


def _ada(conds, ada_w, ada_b):
    depth, d, n = ada_w.shape
    rows = conds.shape[0]
    tn = _tile(n, 512, 128)
    return pl.pallas_call(
        _ada_kernel,
        grid=(depth, n // tn),
        in_specs=[pl.BlockSpec((rows, d), lambda l, j: (0, 0)),
                  pl.BlockSpec((1, d, tn), lambda l, j: (l, 0, j)),
                  pl.BlockSpec((1, 1, tn), lambda l, j: (l, 0, j))],
        out_specs=pl.BlockSpec((1, rows, tn), lambda l, j: (l, 0, j)),
        out_shape=jax.ShapeDtypeStruct((depth, rows, n), F32),
        compiler_params=_cparams(("arbitrary", "arbitrary")),
        name="ada_mod",
    )(conds, ada_w, ada_b.reshape(depth, 1, n))


def _inproj_kernel(x_ref, g_ref, shift_ref, scale_ref, w_ref, o_ref, u_scr):
    @pl.when(pl.program_id(1) == 0)
    def _():
        x = x_ref[...]
        r = lax.rsqrt(jnp.mean(x * x, axis=-1, keepdims=True) + EPS)
        u = x * r * g_ref[...]
        u_scr[...] = (u * (1.0 + scale_ref[0]) + shift_ref[0]).astype(BF16)

    o_ref[...] = jnp.dot(u_scr[...], w_ref[...].astype(BF16), preferred_element_type=F32).astype(o_ref.dtype)


def _cond_index(trunk, tm):
    batch, seq_len, cond0, per_seq = trunk
    if per_seq:
        return lambda i: cond0 + (i * tm) // seq_len
    return lambda i: cond0


def _cond_row_tile(trunk, pref):
    batch, seq_len, _, per_seq = trunk
    if per_seq:
        return _tile(seq_len, pref, 16)
    return _tile(batch * seq_len, pref, 16)


def _inproj(x, norm_g, mod, trunk, w, out_dtype):
    t, d = x.shape
    n = w.shape[1]
    tm = _cond_row_tile(trunk, 1024)
    tn = _tile(n, 1024, 128)
    cidx = _cond_index(trunk, tm)
    return pl.pallas_call(
        _inproj_kernel,
        grid=(t // tm, n // tn),
        in_specs=[pl.BlockSpec((tm, d), lambda i, j: (i, 0)),
                  pl.BlockSpec((1, d), lambda i, j: (0, 0)),
                  pl.BlockSpec((1, 1, d), lambda i, j: (cidx(i), 0, 0)),
                  pl.BlockSpec((1, 1, d), lambda i, j: (cidx(i), 0, 1)),
                  pl.BlockSpec((d, tn), lambda i, j: (0, j))],
        out_specs=pl.BlockSpec((tm, tn), lambda i, j: (i, j)),
        out_shape=jax.ShapeDtypeStruct((t, n), out_dtype),
        scratch_shapes=[pltpu.VMEM((tm, d), BF16)],
        compiler_params=_cparams(("arbitrary", "arbitrary")),
        name="in_proj",
    )(x, norm_g.reshape(1, d), mod, mod, w)


def _outproj_epilogue(acc, x_ref, g_ref, gate_ref, o_ref):
    y = acc
    r = lax.rsqrt(jnp.mean(y * y, axis=-1, keepdims=True) + EPS)
    o_ref[...] = x_ref[...] + gate_ref[0] * (y * r * g_ref[...])


def _outproj_kernel(y_ref, w_ref, x_ref, g_ref, gate_ref, o_ref, acc):
    k = pl.program_id(1)

    @pl.when(k == 0)
    def _():
        acc[...] = jnp.zeros_like(acc)

    acc[...] += jnp.dot(y_ref[...], w_ref[...].astype(BF16), preferred_element_type=F32)

    @pl.when(k == pl.num_programs(1) - 1)
    def _():
        _outproj_epilogue(acc[...], x_ref, g_ref, gate_ref, o_ref)


def _outproj(y, w, x, norm_g, mod, trunk):
    t, e = y.shape
    d = w.shape[1]
    tm = _cond_row_tile(trunk, 1024)
    tk = _tile(e, 1024, 128)
    cidx = _cond_index(trunk, tm)
    return pl.pallas_call(
        _outproj_kernel,
        grid=(t // tm, e // tk),
        in_specs=[pl.BlockSpec((tm, tk), lambda i, k: (i, k)),
                  pl.BlockSpec((tk, d), lambda i, k: (k, 0)),
                  pl.BlockSpec((tm, d), lambda i, k: (i, 0), pipeline_mode=pl.Buffered(1)),
                  pl.BlockSpec((1, d), lambda i, k: (0, 0)),
                  pl.BlockSpec((1, 1, d), lambda i, k: (cidx(i), 0, 2))],
        out_specs=pl.BlockSpec((tm, d), lambda i, k: (i, 0)),
        out_shape=jax.ShapeDtypeStruct((t, d), F32),
        scratch_shapes=[pltpu.VMEM((tm, d), F32)],
        compiler_params=_cparams(("arbitrary", "arbitrary")),
        name="out_proj",
    )(y, w, x, norm_g.reshape(1, d), mod)


def _conv3(x, prev_row, next_row, w, b, row0, seq_len):
    tm = x.shape[0]
    rows = lax.broadcasted_iota(jnp.int32, x.shape, 0)
    pos = (rows + row0) % seq_len
    prev = jnp.where(rows == 0, prev_row, pltpu.roll(x, 1, 0))
    nxt = jnp.where(rows == tm - 1, next_row, pltpu.roll(x, tm - 1, 0))
    prev = jnp.where(pos == 0, 0.0, prev)
    nxt = jnp.where(pos == seq_len - 1, 0.0, nxt)
    return w[0:1] * prev + w[1:2] * x + w[2:3] * nxt + b


def _halo_specs(tm, tc, nrows, colblk):
    nh = nrows // HALO
    r = tm // HALO
    return [pl.BlockSpec((tm, tc), lambda i, j: (i, colblk(j))),
            pl.BlockSpec((HALO, tc), lambda i, j: (jnp.maximum(i * r - 1, 0), colblk(j))),
            pl.BlockSpec((HALO, tc), lambda i, j: (jnp.minimum((i + 1) * r, nh - 1), colblk(j)))]


def _hy_inproj_kernel(seq_len, x_ref, xp_ref, xn_ref, g_ref, shift_ref, scale_ref,
                      w0_ref, w1_ref, w2_ref, wz_ref, cw, cb, vx_ref, g0_ref, u_scr):
    tm = x_ref.shape[0]
    ext = tm + 2 * HALO

    @pl.when(pl.program_id(1) == 0)
    def _():
        def modulated(x):
            r = lax.rsqrt(jnp.mean(x * x, axis=-1, keepdims=True) + EPS)
            return ((x * r * g_ref[...]) * (1.0 + scale_ref[0]) + shift_ref[0]).astype(BF16)

        u_scr[0:HALO] = modulated(xp_ref[...])
        u_scr[HALO:HALO + tm] = modulated(x_ref[...])
        u_scr[HALO + tm:ext] = modulated(xn_ref[...])

    u = u_scr[...]
    rows = lax.broadcasted_iota(jnp.int32, (tm, 1), 0)
    pos = (rows + pl.program_id(0) * tm) % seq_len
    first = pos == 0
    last = pos == seq_len - 1

    def conv(w_ref, k):
        r = jnp.dot(u, w_ref[...].astype(BF16), preferred_element_type=F32)
        prev = jnp.where(first, 0.0, pltpu.roll(r, 1, 0)[HALO:HALO + tm])
        nxt = jnp.where(last, 0.0, pltpu.roll(r, ext - 1, 0)[HALO:HALO + tm])
        w = cw[k]
        return w[0:1] * prev + w[1:2] * r[HALO:HALO + tm] + w[2:3] * nxt + cb[k]

    x1c = conv(w1_ref, 1)
    vc = conv(w2_ref, 2)
    vx_ref[...] = (vc * x1c).astype(vx_ref.dtype)
    x0c = conv(w0_ref, 0)
    z = jnp.dot(u_scr[HALO:HALO + tm], wz_ref[...].astype(BF16), preferred_element_type=F32)
    g0_ref[...] = (x0c * _silu(z)).astype(g0_ref.dtype)


def _hy_inproj(x, norm_g, mod, trunk, w, conv_w, conv_b, e):
    t, d = x.shape
    seq_len = trunk[1]
    tm = _cond_row_tile(trunk, 1024)
    tc = _tile(e, 256, 128)
    nb = e // tc
    cidx = _cond_index(trunk, tm)
    nh = t // HALO
    r = tm // HALO
    wspec = lambda part: pl.BlockSpec((d, tc), lambda i, j: (0, part * nb + j))
    cw = conv_w.reshape(3, 3, e).transpose(1, 0, 2)
    cb = conv_b.reshape(3, 1, e)
    return pl.pallas_call(
        functools.partial(_hy_inproj_kernel, seq_len),
        grid=(t // tm, nb),
        in_specs=[pl.BlockSpec((tm, d), lambda i, j: (i, 0)),
                  pl.BlockSpec((HALO, d), lambda i, j: (jnp.maximum(i * r - 1, 0), 0)),
                  pl.BlockSpec((HALO, d), lambda i, j: (jnp.minimum((i + 1) * r, nh - 1), 0)),
                  pl.BlockSpec((1, d), lambda i, j: (0, 0)),
                  pl.BlockSpec((1, 1, d), lambda i, j: (cidx(i), 0, 0)),
                  pl.BlockSpec((1, 1, d), lambda i, j: (cidx(i), 0, 1)),
                  wspec(0), wspec(1), wspec(2), wspec(3),
                  pl.BlockSpec((3, 3, tc), lambda i, j: (0, 0, j)),
                  pl.BlockSpec((3, 1, tc), lambda i, j: (0, 0, j))],
        out_specs=[pl.BlockSpec((tm, tc), lambda i, j: (i, j))] * 2,
        out_shape=[jax.ShapeDtypeStruct((t, e), BF16), jax.ShapeDtypeStruct((t, e), BF16)],
        scratch_shapes=[pltpu.VMEM((tm + 2 * HALO, d), BF16)],
        compiler_params=_cparams(("arbitrary", "arbitrary")),
        name="hy_in_proj_gate",
    )(x, x, x, norm_g.reshape(1, d), mod, mod, w, w, w, w, cw, cb)


def _filter_tables(seq_len):
    l = seq_len
    t = np.linspace(0.0, 1.0, l)
    w = 2.0 * math.pi * np.arange(l) / l
    f = np.linspace(1e-4, N_BANDS - 1, N_BANDS)
    z = np.concatenate([t[:, None], np.cos(f[None] * w[:, None]), -np.sin(f[None] * w[:, None])], axis=-1)
    pos = np.concatenate([np.arange(l), [0], np.arange(l - 1, 0, -1)])
    z2 = np.zeros((2 * l, 128), np.float32)
    z2[:, :z.shape[1]] = z[pos]
    aux = np.zeros((2 * l, 128), np.float32)
    aux[:, 0] = t[pos]
    aux[:, 1] = 1.0
    aux[l, 1] = 0.0
    return jnp.asarray(z2), jnp.asarray(aux)


def _filter_ffn_kernel(z_ref, w1, b1, w2, b2, w3, b3, fr, o_ref):
    f = fr[...]
    h = jnp.sin(f * (jnp.dot(z_ref[...], w1[...], precision=HI, preferred_element_type=F32) + b1[...]))
    h = jnp.sin(f * (jnp.dot(h, w2[...], precision=HI, preferred_element_type=F32) + b2[...]))
    h = jnp.sin(f * (jnp.dot(h, w3[...], precision=HI, preferred_element_type=F32) + b3[...]))
    o_ref[...] = h.astype(o_ref.dtype)


def _filter_expand_kernel(h_ref, aux_ref, w4, dec, o_ref):
    k = jnp.dot(h_ref[...], w4[...].astype(BF16), preferred_element_type=F32)
    t = aux_ref[:, 0:1]
    keep = aux_ref[:, 1:2]
    o_ref[...] = (k * jnp.exp(-t * jnp.abs(dec[0])) * keep).astype(o_ref.dtype)


def _filters(seq_len, w1, b1, w2, b2, w3, b3, w4, freq, decay):
    e = decay.shape[-1]
    fo = w2.shape[0]
    z2, aux = _filter_tables(seq_len)
    w1p = jnp.zeros((128, fo), F32).at[:w1.shape[0]].set(w1)
    tr = _tile(seq_len, 1024, 8)
    tc = _tile(e, 2048, 128)
    nb = e // tc
    nr = seq_len // tr
    small = lambda a: pl.BlockSpec(a.shape, lambda *_: (0,) * a.ndim)
    b1r, b2r, b3r, frr = (a.reshape(1, fo) for a in (b1, b2, b3, freq))
    h3 = pl.pallas_call(
        _filter_ffn_kernel,
        grid=(2 * nr,),
        in_specs=[pl.BlockSpec((tr, 128), lambda i: (i, 0)),
                  small(w1p), small(b1r), small(w2), small(b2r), small(w3), small(b3r), small(frr)],
        out_specs=pl.BlockSpec((tr, fo), lambda i: (i, 0)),
        out_shape=jax.ShapeDtypeStruct((2 * seq_len, fo), BF16),
        compiler_params=_cparams(("arbitrary",)),
        name="hy_filter_ffn",
    )(z2, w1p, b1r, w2, b2r, w3, b3r, frr)
    return pl.pallas_call(
        _filter_expand_kernel,
        grid=(2 * nr, nb),
        in_specs=[pl.BlockSpec((tr, fo), lambda i, j: (i, 0)),
                  pl.BlockSpec((tr, 128), lambda i, j: (i, 0)),
                  pl.BlockSpec((fo, tc), lambda i, j: (0, (i // nr) * nb + j)),
                  pl.BlockSpec((1, 1, tc), lambda i, j: (i // nr, 0, j))],
        out_specs=pl.BlockSpec((tr, tc), lambda i, j: (i, j)),
        out_shape=jax.ShapeDtypeStruct((2 * seq_len, e), BF16),
        compiler_params=_cparams(("arbitrary", "arbitrary")),
        name="hy_filter_expand",
    )(h3, aux, w4, decay.reshape(2, 1, e))


def _cmm_kernel(w_ref, x_ref, o_ref):
    o_ref[0] = jnp.dot(w_ref[...], x_ref[0], preferred_element_type=F32).astype(o_ref.dtype)


def _cmm(w, x, out_dtype):
    m, k = w.shape
    b, _, n = x.shape
    tn = _tile(n, 4096, 128)
    return pl.pallas_call(
        _cmm_kernel,
        grid=(b, n // tn),
        in_specs=[pl.BlockSpec((m, k), lambda bi, j: (0, 0)),
                  pl.BlockSpec((1, k, tn), lambda bi, j: (bi, 0, j))],
        out_specs=pl.BlockSpec((1, m, tn), lambda bi, j: (bi, 0, j)),
        out_shape=jax.ShapeDtypeStruct((b, m, n), out_dtype),
        compiler_params=_cparams(("arbitrary", "arbitrary")),
        name="const_lhs_matmul",
    )(w, x)


def _direct_mats(seq_len):
    l = seq_len
    n = 2 * l
    f = np.arange(l)[:, None]
    t = np.arange(n)[None, :]
    ang = 2.0 * math.pi * ((f * t) % n) / n
    fwd = np.concatenate([np.cos(ang), -np.sin(ang)], axis=0)
    fwd[l] = np.cos(math.pi * np.arange(n))
    tt = np.arange(l)[:, None]
    ff = np.arange(l)[None, :]
    ang2 = 2.0 * math.pi * ((tt * ff) % n) / n
    wgt = np.where(ff == 0, 1.0, 2.0) / n
    inv = np.concatenate([wgt * np.cos(ang2), -wgt * np.sin(ang2)], axis=1)
    inv[:, l] = np.cos(math.pi * np.arange(l)) / n
    return fwd, inv


def _direct_conv_kernel(seq_len, f_ref, v_ref, x_ref, k_ref, g0_ref, bias_ref, o_ref):
    l = seq_len
    x = x_ref[0]
    s = jnp.dot(f_ref[...], x, preferred_element_type=F32)
    sre, sim = s[:l], s[l:]
    kre, kim = k_ref[:l, :], k_ref[l:, :]
    row0 = lax.broadcasted_iota(jnp.int32, sre.shape, 0) == 0
    yre = sre * kre - jnp.where(row0, 0.0, sim * kim)
    yim = jnp.where(row0, sim * kim, sre * kim + sim * kre)
    y = jnp.concatenate([yre, yim], axis=0).astype(BF16)
    out = jnp.dot(v_ref[...], y, preferred_element_type=F32)
    out = out + x.astype(F32) * bias_ref[...]
    o_ref[0] = (out * g0_ref[0].astype(F32)).astype(o_ref.dtype)


def _longconv_direct(vx, g0, kfilt, bias, batch, seq_len):
    e = vx.shape[-1]
    l = seq_len
    fwd, inv = _direct_mats(l)
    f_full = jnp.asarray(fwd, BF16)
    f_data = jnp.asarray(fwd[:, :l], BF16)
    v_mat = jnp.asarray(inv, BF16)
    kspec = _cmm(f_full, kfilt.astype(BF16)[None], F32)[0]
    tc = _tile(e, 1024, 128)
    out = pl.pallas_call(
        functools.partial(_direct_conv_kernel, l),
        grid=(e // tc, batch),
        in_specs=[pl.BlockSpec((2 * l, l), lambda j, b: (0, 0)),
                  pl.BlockSpec((l, 2 * l), lambda j, b: (0, 0)),
                  pl.BlockSpec((1, l, tc), lambda j, b: (b, 0, j)),
                  pl.BlockSpec((2 * l, tc), lambda j, b: (0, j)),
                  pl.BlockSpec((1, l, tc), lambda j, b: (b, 0, j)),
                  pl.BlockSpec((1, tc), lambda j, b: (0, j))],
        out_specs=pl.BlockSpec((1, l, tc), lambda j, b: (b, 0, j)),
        out_shape=jax.ShapeDtypeStruct((batch, l, e), BF16),
        compiler_params=_cparams(("arbitrary", "arbitrary")),
        name="longconv_direct",
    )(f_data, v_mat, vx.reshape(batch, l, e), kspec, g0.reshape(batch, l, e), bias.reshape(1, e))
    return out.reshape(batch * l, e)


def _two_level_mats(seq_len):
    n = 2 * seq_len
    n2 = FFT_N2
    n1 = n // n2
    h2 = n2 // 2
    ns = 8 * ((h2 + 1 + 7) // 8)
    f2 = np.arange(h2 + 1)[:, None]
    t2 = np.arange(n2)[None, :]
    ang = 2.0 * math.pi * ((f2 * t2) % n2) / n2
    f1m = np.zeros((2 * ns, n2))
    f1m[0:2 * (h2 + 1):2] = np.cos(ang)
    f1m[1:2 * (h2 + 1):2] = -np.sin(ang)
    wgt = np.where((f2 == 0) | (f2 == h2), 1.0, 2.0) / n
    g1m = np.zeros((h2, 2 * ns))
    g1m[:, 0:2 * (h2 + 1):2] = (wgt * np.cos(ang[:, :h2])).T
    g1m[:, 1:2 * (h2 + 1):2] = -(wgt * np.sin(ang[:, :h2])).T
    t1 = np.arange(n1)[None, :]
    f1 = np.arange(n1)[:, None]
    rm = np.zeros((ns, 2 * n1, 2 * n1))
    pm = np.zeros((ns, 2 * n1, 2 * n1))
    for s in range(h2 + 1):
        a = 2.0 * math.pi * ((t1 * (n2 * f1 + s)) % n) / n
        mr, mi = np.cos(a), -np.sin(a)
        rm[s] = np.block([[mr, -mi], [mi, mr]])
        pr, pi = mr.T, -mi.T
        pm[s] = np.block([[pr, -pi], [pi, pr]])
    return n1, n2, ns, f1m, g1m, rm, pm


def _slab_spec_kernel(r_ref, a_ref, o_ref):
    n1 = a_ref.shape[3]
    a = jnp.concatenate([a_ref[0, 0, 0], a_ref[0, 0, 1]], axis=0)
    s = jnp.dot(r_ref[0], a, preferred_element_type=F32)
    o_ref[0, 0] = s[:n1].astype(o_ref.dtype)
    o_ref[0, 1] = s[n1:].astype(o_ref.dtype)


def _slab_conv_kernel(r_ref, p_ref, a_ref, k_ref, o_ref):
    n1 = a_ref.shape[3]
    kre, kim = k_ref[0, 0].astype(F32), k_ref[0, 1].astype(F32)
    for b in range(a_ref.shape[0]):
        a = jnp.concatenate([a_ref[b, 0, 0], a_ref[b, 0, 1]], axis=0)
        s = jnp.dot(r_ref[0], a, preferred_element_type=F32)
        sre, sim = s[:n1], s[n1:]
        y = jnp.concatenate([sre * kre - sim * kim, sre * kim + sim * kre], axis=0).astype(BF16)
        o = jnp.dot(p_ref[0], y, preferred_element_type=F32)
        o_ref[b, 0, 0] = o[:n1].astype(o_ref.dtype)
        o_ref[b, 0, 1] = o[n1:].astype(o_ref.dtype)


def _dft_l1_kernel(k_ref, x_ref, o_ref):
    x = x_ref[0].astype(F32)
    nt2, _, tc = x.shape
    halves = []
    for g in range(SUB_TILES):
        xg = x[:, SUBLANES * g:SUBLANES * (g + 1), :].reshape(nt2 * SUBLANES, tc).astype(BF16)
        a = jnp.dot(k_ref[...], xg, preferred_element_type=F32)
        halves.append(a.reshape(-1, SUBLANES, tc))
    o_ref[0] = jnp.concatenate(halves, axis=1).astype(o_ref.dtype)


def _dft_l1_inv_kernel(g_ref, a_ref, vx_ref, g0_ref, bias_ref, o_ref):
    a = a_ref[0].astype(F32)
    nr, _, tc = a.shape
    halves = []
    for g in range(SUB_TILES):
        ag = a[:, SUBLANES * g:SUBLANES * (g + 1), :].reshape(nr * SUBLANES, tc).astype(BF16)
        y = jnp.dot(g_ref[...], ag, preferred_element_type=F32)
        halves.append(y.reshape(-1, SUBLANES, tc))
    y = jnp.concatenate(halves, axis=1)
    y = y + vx_ref[0].astype(F32) * bias_ref[...]
    o_ref[0] = (y * g0_ref[0].astype(F32)).astype(o_ref.dtype)


def _dft_l1(kmat, x4, out_dtype):
    b, nt2, n1, e = x4.shape
    m = kmat.shape[0] // SUBLANES
    tc = _tile(e, 1024, 128)
    rows = SUBLANES * SUB_TILES
    return pl.pallas_call(
        _dft_l1_kernel,
        grid=(b, n1 // rows, e // tc),
        in_specs=[pl.BlockSpec(kmat.shape, lambda bi, i, j: (0, 0)),
                  pl.BlockSpec((1, nt2, rows, tc), lambda bi, i, j: (bi, 0, i, j))],
        out_specs=pl.BlockSpec((1, m, rows, tc), lambda bi, i, j: (bi, 0, i, j)),
        out_shape=jax.ShapeDtypeStruct((b, m, n1, e), out_dtype),
        compiler_params=_cparams(("arbitrary", "arbitrary", "arbitrary")),
        name="dft_level1",
    )(kmat, x4)


def _longconv_two_level(vx, g0, kfilt, bias, batch, seq_len):
    e = vx.shape[-1]
    l = seq_len
    n1, n2, ns, f1m, g1m, rm, pm = _two_level_mats(l)
    h2 = n2 // 2
    eye = np.eye(SUBLANES)
    k_full = jnp.asarray(np.kron(f1m, eye), BF16)
    k_data = jnp.asarray(np.kron(f1m[:, :h2], eye), BF16)
    g8 = jnp.asarray(np.kron(g1m, eye), BF16)
    rmat = jnp.asarray(rm, BF16)
    pmat = jnp.asarray(pm, BF16)
    tc = _tile(e, 4096, 128)
    nc = e // tc

    ka = _dft_l1(k_full, kfilt.reshape(1, n2, n1, e), BF16).reshape(1, ns, 2, n1, e)
    kspec = pl.pallas_call(
        _slab_spec_kernel,
        grid=(ns, nc),
        in_specs=[pl.BlockSpec((1, 2 * n1, 2 * n1), lambda s, j: (s, 0, 0)),
                  pl.BlockSpec((1, 1, 2, n1, tc), lambda s, j: (0, s, 0, 0, j))],
        out_specs=pl.BlockSpec((1, 2, n1, tc), lambda s, j: (s, 0, 0, j)),
        out_shape=jax.ShapeDtypeStruct((ns, 2, n1, e), BF16),
        compiler_params=_cparams(("arbitrary", "arbitrary")),
        name="filter_slab_dft",
    )(rmat, ka)

    vx4 = vx.reshape(batch, h2, n1, e)
    a = _dft_l1(k_data, vx4, BF16).reshape(batch, ns, 2, n1, e)
    a2 = pl.pallas_call(
        _slab_conv_kernel,
        grid=(ns, nc),
        in_specs=[pl.BlockSpec((1, 2 * n1, 2 * n1), lambda s, j: (s, 0, 0)),
                  pl.BlockSpec((1, 2 * n1, 2 * n1), lambda s, j: (s, 0, 0)),
                  pl.BlockSpec((batch, 1, 2, n1, tc), lambda s, j: (0, s, 0, 0, j)),
                  pl.BlockSpec((1, 2, n1, tc), lambda s, j: (s, 0, 0, j))],
        out_specs=pl.BlockSpec((batch, 1, 2, n1, tc), lambda s, j: (0, s, 0, 0, j)),
        out_shape=jax.ShapeDtypeStruct((batch, ns, 2, n1, e), BF16),
        compiler_params=_cparams(("arbitrary", "arbitrary")),
        name="slab_conv",
    )(rmat, pmat, a, kspec)
    a2 = a2.reshape(batch, 2 * ns, n1, e)
    rows = SUBLANES * SUB_TILES
    tci = _tile(e, 1024, 128)
    blk = lambda r: pl.BlockSpec((1, r, rows, tci), lambda bi, i, j: (bi, 0, i, j))
    out = pl.pallas_call(
        _dft_l1_inv_kernel,
        grid=(batch, n1 // rows, e // tci),
        in_specs=[pl.BlockSpec(g8.shape, lambda bi, i, j: (0, 0)),
                  blk(2 * ns), blk(h2), blk(h2),
                  pl.BlockSpec((1, 1, tci), lambda bi, i, j: (0, 0, j))],
        out_specs=blk(h2),
        out_shape=jax.ShapeDtypeStruct((batch, h2, n1, e), BF16),
        compiler_params=_cparams(("arbitrary", "arbitrary", "arbitrary")),
        name="dft_level1_inverse",
    )(g8, a2, vx4, g0.reshape(batch, h2, n1, e), bias.reshape(1, 1, e))
    return out.reshape(batch * l, e)


def _longconv(vx, g0, kfilt, bias, batch, seq_len):
    if seq_len <= DIRECT_CONV_MAX_L:
        return _longconv_direct(vx, g0, kfilt, bias, batch, seq_len)
    return _longconv_two_level(vx, g0, kfilt, bias, batch, seq_len)


def _ml_qkv_kernel(seq_len, n_heads, dk, x_ref, xp_ref, xn_ref, cw_ref, cb_ref, wq_ref, wk_ref, wv_ref,
                   wgq_ref, wgk_ref, wgv_ref, bg_ref, q_ref, k_ref, kt_ref, v_ref, xc_ref, g_ref, gt_ref, gacc):
    h = pl.program_id(1)
    tm = x_ref.shape[0]
    q_len = gt_ref.shape[2]
    x = x_ref[...].astype(F32)
    conv = _conv3(x, xp_ref[HALO - 1:HALO, :].astype(F32), xn_ref[0:1, :].astype(F32),
                  cw_ref[...], cb_ref[...], pl.program_id(0) * tm, seq_len)
    xc = _silu(conv)
    xc_ref[...] = xc.astype(xc_ref.dtype)
    xcb = xc.astype(BF16)
    q = jnp.dot(xcb, wq_ref[0], preferred_element_type=F32).astype(BF16)
    kf = jnp.dot(xcb, wk_ref[0], preferred_element_type=F32) * (dk ** -0.5)
    k = kf.astype(BF16)
    v = jnp.dot(x.astype(BF16), wv_ref[0], preferred_element_type=F32).astype(BF16)
    q_ref[...] = q
    k_ref[...] = k
    v_ref[...] = v
    for c in range(tm // q_len):
        kt_ref[c] = kf[c * q_len:(c + 1) * q_len, :].T.astype(BF16)
    part = (jnp.dot(q, wgq_ref[0], preferred_element_type=F32)
            + jnp.dot(k, wgk_ref[0], preferred_element_type=F32)
            + jnp.dot(v, wgv_ref[0], preferred_element_type=F32))

    @pl.when(h == 0)
    def _():
        gacc[...] = part + bg_ref[...]

    @pl.when(h > 0)
    def _():
        gacc[...] += part

    @pl.when(h == n_heads - 1)
    def _():
        g = gacc[...]
        col = lax.broadcasted_iota(jnp.int32, g.shape, 1)
        is_forget = (col % (2 * n_heads)) >= n_heads
        g = jnp.where(is_forget, jax.nn.log_sigmoid(g), g)
        col_q = lax.broadcasted_iota(jnp.int32, (q_len, g.shape[1]), 1)
        forget_q = (col_q % (2 * n_heads)) >= n_heads
        bwd_q = (col_q // (2 * n_heads)) == 1
        r_i = lax.broadcasted_iota(jnp.int32, (q_len, q_len), 0)
        c_i = lax.broadcasted_iota(jnp.int32, (q_len, q_len), 1)
        tril = (c_i <= r_i).astype(F32)
        for c in range(tm // q_len):
            gc = g[c * q_len:(c + 1) * q_len, :]
            pre = jnp.dot(tril, gc, precision=HI, preferred_element_type=F32)
            tot = jnp.sum(gc, axis=0, keepdims=True)
            suf = tot - pre + gc
            gc = jnp.where(forget_q, jnp.where(bwd_q, suf, pre), gc)
            g_ref[c * q_len:(c + 1) * q_len, :] = gc
            gt_ref[c] = gc.T


def _ml_qkv(p, conv_w, conv_b, wq, wk, wv, w_gate, b_gate, batch, seq_len, e):
    t = p.shape[0]
    n_heads, dh, dk = wq.shape
    dv = wv.shape[2]
    q_len = min(SCAN_CHUNK, seq_len)
    tm = _row_tile(batch, seq_len, 1024, q_len)
    ng = w_gate.shape[1]
    gpad = 128
    wgq = jnp.zeros((n_heads, dk, gpad), BF16).at[:, :, :ng].set(
        w_gate[:n_heads * dk].reshape(n_heads, dk, ng).astype(BF16))
    wgk = jnp.zeros((n_heads, dk, gpad), BF16).at[:, :, :ng].set(
        w_gate[n_heads * dk:2 * n_heads * dk].reshape(n_heads, dk, ng).astype(BF16))
    wgv = jnp.zeros((n_heads, dv, gpad), BF16).at[:, :, :ng].set(
        w_gate[2 * n_heads * dk:].reshape(n_heads, dv, ng).astype(BF16))
    bg = jnp.zeros((1, gpad), F32).at[0, :ng].set(b_gate)
    specs = _halo_specs(tm, dh, t, lambda j: j)
    specs += [pl.BlockSpec((3, dh), lambda i, j: (0, j)),
              pl.BlockSpec((1, dh), lambda i, j: (0, j)),
              pl.BlockSpec((1, dh, dk), lambda i, j: (j, 0, 0)),
              pl.BlockSpec((1, dh, dk), lambda i, j: (j, 0, 0)),
              pl.BlockSpec((1, dh, dv), lambda i, j: (j, 0, 0)),
              pl.BlockSpec((1, dk, gpad), lambda i, j: (j, 0, 0)),
              pl.BlockSpec((1, dk, gpad), lambda i, j: (j, 0, 0)),
              pl.BlockSpec((1, dv, gpad), lambda i, j: (j, 0, 0)),
              pl.BlockSpec((1, gpad), lambda i, j: (0, 0))]
    return pl.pallas_call(
        functools.partial(_ml_qkv_kernel, seq_len, n_heads, dk),
        grid=(t // tm, n_heads),
        in_specs=specs,
        out_specs=[pl.BlockSpec((tm, dk), lambda i, j: (i, j)),
                   pl.BlockSpec((tm, dk), lambda i, j: (i, j)),
                   pl.BlockSpec((tm // q_len, dk, q_len), lambda i, j: (i, j, 0)),
                   pl.BlockSpec((tm, dv), lambda i, j: (i, j)),
                   pl.BlockSpec((tm, dh), lambda i, j: (i, j)),
                   pl.BlockSpec((tm, gpad), lambda i, j: (i, 0)),
                   pl.BlockSpec((tm // q_len, gpad, q_len), lambda i, j: (i, 0, 0))],
        out_shape=[jax.ShapeDtypeStruct((t, n_heads * dk), BF16),
                   jax.ShapeDtypeStruct((t, n_heads * dk), BF16),
                   jax.ShapeDtypeStruct((t // q_len, n_heads * dk, q_len), BF16),
                   jax.ShapeDtypeStruct((t, n_heads * dv), BF16),
                   jax.ShapeDtypeStruct((t, e), BF16),
                   jax.ShapeDtypeStruct((t, gpad), F32),
                   jax.ShapeDtypeStruct((t // q_len, gpad, q_len), F32)],
        scratch_shapes=[pltpu.VMEM((tm, gpad), F32)],
        compiler_params=_cparams(("arbitrary", "arbitrary")),
        name="ml_qkv_gates",
    )(p, p, p, conv_w, conv_b.reshape(1, e), wq.astype(BF16), wk.astype(BF16), wv.astype(BF16),
      wgq, wgk, wgv, bg)


def _ml_scan_kernel(n_heads, has_init, *refs):
    if has_init:
        (q_ref, k_ref, kt_ref, v_ref, g_ref, gt_ref, c0_ref, n0_ref, m0_ref,
         h_ref, cout_ref, nout_ref, mout_ref, c_scr, n_scr, m_scr) = refs
    else:
        (q_ref, k_ref, kt_ref, v_ref, g_ref, gt_ref,
         h_ref, cout_ref, nout_ref, mout_ref, c_scr, n_scr, m_scr) = refs
    head = pl.program_id(1)
    seq_len = q_ref.shape[0]
    q_len = gt_ref.shape[2]
    n_chunks = seq_len // q_len

    for d in range(2):
        if has_init:
            c_scr[d] = c0_ref[0, 0, d, 0]
            n_scr[d] = n0_ref[0, 0, d]
            m_scr[d] = m0_ref[0, 0, d]
        else:
            c_scr[d] = jnp.zeros(c_scr.shape[1:], F32)
            n_scr[d] = jnp.zeros(n_scr.shape[1:], F32)
            m_scr[d] = jnp.zeros(m_scr.shape[1:], F32)
    h_ref[...] = jnp.zeros_like(h_ref)

    row_i = lax.broadcasted_iota(jnp.int32, (q_len, q_len), 0)
    col_i = lax.broadcasted_iota(jnp.int32, (q_len, q_len), 1)
    lane_g = lax.broadcasted_iota(jnp.int32, (q_len, g_ref.shape[1]), 1)

    def direction(d, c):
        icol = d * 2 * n_heads + head
        fcol = icol + n_heads
        rows = pl.ds(pl.multiple_of(c * q_len, q_len), q_len)
        g = g_ref[rows, :]
        i_col = jnp.sum(jnp.where(lane_g == icol, g, 0.0), axis=1, keepdims=True)
        b_col = jnp.sum(jnp.where(lane_g == fcol, g, 0.0), axis=1, keepdims=True)
        i_row = gt_ref[c, pl.ds(icol, 1), :]
        b_row = gt_ref[c, pl.ds(fcol, 1), :]
        mask = (col_i <= row_i) if d == 0 else (col_i >= row_i)
        last = q_len - 1 if d == 0 else 0
        g_tot = jnp.sum(jnp.where(col_i[0:1, :] == last, b_row, 0.0), axis=1, keepdims=True)
        m_prev = m_scr[d][:, 0:1]
        n_prev = n_scr[d]
        c_prev = c_scr[d]

        qc = q_ref[rows, :]
        kc = k_ref[rows, :]
        vc = v_ref[rows, :]
        dmat = jnp.where(mask, b_col - b_row + i_row, -jnp.inf)
        inter = b_col + m_prev
        m_t = jnp.maximum(inter, jnp.max(dmat, axis=1, keepdims=True))
        qk = lax.dot_general(qc, kc, (((1,), (1,)), ((), ())), preferred_element_type=F32)
        s = qk * jnp.exp(dmat - m_t)
        w_inter = jnp.exp(inter - m_t)
        num = (w_inter * jnp.dot(qc, c_prev.astype(BF16), preferred_element_type=F32)
               + jnp.dot(s.astype(BF16), vc, preferred_element_type=F32))
        qn = (w_inter * jnp.sum(qc.astype(F32) * n_prev, axis=1, keepdims=True)
              + jnp.sum(s, axis=1, keepdims=True))
        den = jnp.maximum(jnp.abs(qn), jnp.exp(-m_t))
        h_ref[rows, :] += num / den

        a_col = g_tot - b_col + i_col
        a_row = g_tot - b_row + i_row
        m_new = jnp.maximum(g_tot + m_prev, jnp.max(a_row, axis=1, keepdims=True))
        dec = jnp.exp(g_tot + m_prev - m_new)
        kwt = (kt_ref[c].astype(F32) * jnp.exp(a_row - m_new)).astype(BF16)
        c_scr[d] = dec * c_prev + jnp.dot(kwt, vc, preferred_element_type=F32)
        n_scr[d] = dec * n_prev + jnp.sum(kc.astype(F32) * jnp.exp(a_col - m_new), axis=0, keepdims=True)
        m_scr[d] = jnp.broadcast_to(m_new, m_scr.shape[1:])

    def body(j, carry):
        direction(0, j)
        direction(1, n_chunks - 1 - j)
        return carry

    lax.fori_loop(0, n_chunks, body, 0)
    for d in range(2):
        cout_ref[0, 0, d, 0] = c_scr[d]
        nout_ref[0, 0, d] = n_scr[d]
        mout_ref[0, 0, d] = m_scr[d]


def _ml_scan(q, k, kt, v, g, gt, batch, seq_len, n_heads, init=None):
    dk = q.shape[1] // n_heads
    dv = v.shape[1] // n_heads
    q_len = gt.shape[2]
    gpad = g.shape[1]
    nq = seq_len // q_len
    in_specs = [pl.BlockSpec((seq_len, dk), lambda b, h: (b, h)),
                pl.BlockSpec((seq_len, dk), lambda b, h: (b, h)),
                pl.BlockSpec((nq, dk, q_len), lambda b, h: (b, h, 0)),
                pl.BlockSpec((seq_len, dv), lambda b, h: (b, h)),
                pl.BlockSpec((seq_len, gpad), lambda b, h: (b, 0)),
                pl.BlockSpec((nq, gpad, q_len), lambda b, h: (b, 0, 0))]
    args = [q, k, kt, v, g, gt]
    if init is not None:
        c0, n0, m0 = init
        in_specs += [pl.BlockSpec((1, 1, 2, 1, dk, dv), lambda b, h: (b, 0, 0, h, 0, 0)),
                     pl.BlockSpec((1, 1, 2, 1, dk), lambda b, h: (b, h, 0, 0, 0)),
                     pl.BlockSpec((1, 1, 2, 1, 128), lambda b, h: (b, h, 0, 0, 0))]
        args += [c0, n0, m0]
    out = pl.pallas_call(
        functools.partial(_ml_scan_kernel, n_heads, init is not None),
        grid=(batch, n_heads),
        in_specs=in_specs,
        out_specs=[pl.BlockSpec((seq_len, dv), lambda b, h: (b, h)),
                   pl.BlockSpec((1, 1, 2, 1, dk, dv), lambda b, h: (b, 0, 0, h, 0, 0)),
                   pl.BlockSpec((1, 1, 2, 1, dk), lambda b, h: (b, h, 0, 0, 0)),
                   pl.BlockSpec((1, 1, 2, 1, 128), lambda b, h: (b, h, 0, 0, 0))],
        out_shape=[jax.ShapeDtypeStruct((batch * seq_len, n_heads * dv), F32),
                   jax.ShapeDtypeStruct((batch, 1, 2, n_heads, dk, dv), F32),
                   jax.ShapeDtypeStruct((batch, n_heads, 2, 1, dk), F32),
                   jax.ShapeDtypeStruct((batch, n_heads, 2, 1, 128), F32)],
        scratch_shapes=[pltpu.VMEM((2, dk, dv), F32), pltpu.VMEM((2, 1, dk), F32),
                        pltpu.VMEM((2, 1, 128), F32)],
        compiler_params=_cparams(("arbitrary", "arbitrary")),
        name="ml_scan",
    )(*args)
    return out


def _ml_out_kernel(h_ref, o_ref_in, xc_ref, z_ref, nw_ref, sk_ref, w_ref, x_ref, g_ref, gate_ref, out_ref, acc):
    k = pl.program_id(1)

    @pl.when(k == 0)
    def _():
        acc[...] = jnp.zeros_like(acc)

    hh = jax.nn.sigmoid(o_ref_in[...].astype(F32)) * h_ref[...]
    mu = jnp.mean(hh, axis=-1, keepdims=True)
    var = jnp.mean(jnp.square(hh - mu), axis=-1, keepdims=True)
    hn = (hh - mu) * lax.rsqrt(var + EPS) * nw_ref[...]
    hn = hn + sk_ref[...] * xc_ref[...].astype(F32)
    y = (hn * _silu(z_ref[...].astype(F32))).astype(BF16)
    acc[...] += jnp.dot(y, w_ref[...].astype(BF16), preferred_element_type=F32)

    @pl.when(k == pl.num_programs(1) - 1)
    def _():
        _outproj_epilogue(acc[...], x_ref, g_ref, gate_ref, out_ref)


def _ml_out(h, p, xc, norm_w, skip, w, x, norm_g, mod, trunk, n_heads):
    t, e = h.shape
    d = w.shape[1]
    dv = e // n_heads
    tm = _cond_row_tile(trunk, 1024)
    cidx = _cond_index(trunk, tm)
    return pl.pallas_call(
        _ml_out_kernel,
        grid=(t // tm, n_heads),
        in_specs=[pl.BlockSpec((tm, dv), lambda i, k: (i, k)),
                  pl.BlockSpec((tm, dv), lambda i, k: (i, 2 * n_heads + k)),
                  pl.BlockSpec((tm, dv), lambda i, k: (i, k)),
                  pl.BlockSpec((tm, dv), lambda i, k: (i, n_heads + k)),
                  pl.BlockSpec((1, dv), lambda i, k: (0, k)),
                  pl.BlockSpec((1, dv), lambda i, k: (0, k)),
                  pl.BlockSpec((dv, d), lambda i, k: (k, 0)),
                  pl.BlockSpec((tm, d), lambda i, k: (i, 0), pipeline_mode=pl.Buffered(1)),
                  pl.BlockSpec((1, d), lambda i, k: (0, 0)),
                  pl.BlockSpec((1, 1, d), lambda i, k: (cidx(i), 0, 2))],
        out_specs=pl.BlockSpec((tm, d), lambda i, k: (i, 0)),
        out_shape=jax.ShapeDtypeStruct((t, d), F32),
        scratch_shapes=[pltpu.VMEM((tm, d), F32)],
        compiler_params=_cparams(("arbitrary", "arbitrary")),
        name="ml_out_proj",
    )(h, p, xc, p, norm_w.reshape(1, e), skip.reshape(1, e), w, x, norm_g.reshape(1, d), mod)


def _hyena_layer(x, mod, trunk, norm_pre, norm_post, hw):
    (w_in, conv_w, conv_b, w1, b1, w2, b2, w3, b3, w4, freq, decay, bias, w_out) = hw
    batch, seq_len = trunk[0], trunk[1]
    e = w_out.shape[0]
    vx, g0 = _hy_inproj(x, norm_pre, mod, trunk, w_in, conv_w, conv_b, e)
    kfilt = _filters(seq_len, w1, b1, w2, b2, w3, b3, w4, freq, decay)
    y = _longconv(vx, g0, kfilt, bias, batch, seq_len)
    return _outproj(y, w_out.astype(BF16), x, norm_post, mod, trunk)


def _mlstm_layer(x, mod, trunk, norm_pre, norm_post, mw, init):
    (w_in, conv_w, conv_b, wq, wk, wv, w_gate, b_gate, norm_w, skip, w_out) = mw
    batch, seq_len = trunk[0], trunk[1]
    e = w_out.shape[0]
    n_heads = wq.shape[0]
    p = _inproj(x, norm_pre, mod, trunk, w_in, BF16)
    q, k, kt, v, xc, g, gt = _ml_qkv(p, conv_w, conv_b, wq, wk, wv, w_gate, b_gate, batch, seq_len, e)
    h, c_new, n_new, m_new = _ml_scan(q, k, kt, v, g, gt, batch, seq_len, n_heads, init)
    x_new = _ml_out(h, p, xc, norm_w, skip, w_out.astype(BF16), x, norm_post, mod, trunk, n_heads)
    return x_new, c_new, n_new, m_new


def kernel(x_prompt, x_sample, state_C, state_n, state_m, c, c_ctx, norm_pre, norm_post, ada_w, ada_b,
           hy_w_in, hy_conv_w, hy_conv_b, hy_ffn_w1, hy_ffn_b1, hy_ffn_w2, hy_ffn_b2, hy_ffn_w3, hy_ffn_b3,
           hy_ffn_w4, hy_sin_freq, hy_decay, hy_bias, hy_w_out, ml_w_in, ml_conv_w, ml_conv_b, ml_wq, ml_wk,
           ml_wv, ml_w_gate, ml_b_gate, ml_norm, ml_skip, ml_w_out):
    depth = norm_pre.shape[0]
    bp, lp, d = x_prompt.shape
    bs, ls, _ = x_sample.shape
    n_heads = ml_wq.shape[1]

    nrow = 8 * ((1 + bs + 7) // 8)
    conds = jnp.zeros((nrow, d), F32).at[0].set(c_ctx).at[1:1 + bs].set(c)
    mods = _ada(conds, ada_w, ada_b)

    def run_trunk(x3, cond0, per_seq, init_states):
        batch, seq_len, _ = x3.shape
        trunk = (batch, seq_len, cond0, per_seq)
        x = x3.reshape(batch * seq_len, d)
        new_c, new_n, new_m = [], [], []
        for i in range(depth):
            mod = mods[i].reshape(nrow, 1, 3 * d)
            j = i // 2
            if i % 2 == 0:
                hw = (hy_w_in[j], hy_conv_w[j], hy_conv_b[j], hy_ffn_w1[j], hy_ffn_b1[j], hy_ffn_w2[j],
                      hy_ffn_b2[j], hy_ffn_w3[j], hy_ffn_b3[j], hy_ffn_w4[j], hy_sin_freq[j], hy_decay[j],
                      hy_bias[j], hy_w_out[j])
                x = _hyena_layer(x, mod, trunk, norm_pre[i], norm_post[i], hw)
            else:
                mw = (ml_w_in[j], ml_conv_w[j], ml_conv_b[j], ml_wq[j], ml_wk[j], ml_wv[j], ml_w_gate[j],
                      ml_b_gate[j], ml_norm[j], ml_skip[j], ml_w_out[j])
                init = None if init_states is None else init_states(j)
                x, cj, nj, mj = _mlstm_layer(x, mod, trunk, norm_pre[i], norm_post[i], mw, init)
                new_c.append(cj)
                new_n.append(nj)
                new_m.append(mj)
        return x.reshape(batch, seq_len, d), new_c, new_n, new_m

    y_prompt, cs, ns, ms = run_trunk(x_prompt, 0, False, None)
    dk = ml_wq.shape[3]
    new_state_c = jnp.concatenate(cs, axis=1)
    new_state_n = jnp.stack([jnp.swapaxes(n[:, :, :, 0, :], 1, 2) for n in ns], axis=1)
    new_state_m = jnp.stack([jnp.swapaxes(m[:, :, :, 0, 0], 1, 2) for m in ms], axis=1)

    def lat_init(j):
        c0 = state_C[:, j:j + 1]
        n0 = jnp.swapaxes(state_n[:, j], 1, 2)[:, :, :, None, :]
        m0 = jnp.broadcast_to(jnp.swapaxes(state_m[:, j], 1, 2)[:, :, :, None, None],
                              (bs, n_heads, 2, 1, 128))
        return c0, n0, m0

    y_sample, _, _, _ = run_trunk(x_sample, 1, True, lat_init)
    return (y_prompt, y_sample, new_state_c, new_state_n, new_state_m)
```

```python
import functools
import math

import numpy as np
import jax
import jax.numpy as jnp
from jax import lax
from jax.experimental import pallas as pl
from jax.experimental.pallas import tpu as pltpu

F32 = jnp.float32
BF16 = jnp.bfloat16
EPS = 1e-6
N_BANDS = 16
SUBLANES = 8
SUB_TILES = 2
HALO = 16
SCAN_CHUNK = 256
FFT_N2 = 128
DIRECT_CONV_MAX_L = 512
VMEM_LIMIT = 56 * 1024 * 1024
HI = lax.Precision.HIGHEST


def _cparams(sem):
    return pltpu.CompilerParams(dimension_semantics=sem, vmem_limit_bytes=VMEM_LIMIT)


def _tile(n, pref, mult):
    if n <= pref:
        return n
    t = (pref // mult) * mult
    while t >= mult:
        if n % t == 0:
            return t
        t -= mult
    return n


def _row_tile(batch, seq_len, pref, mult):
    if seq_len >= pref:
        return _tile(seq_len, pref, mult)
    k = max(1, pref // seq_len)
    while batch % k:
        k -= 1
    return seq_len * k


def _silu(x):
    return x * jax.nn.sigmoid(x)


def _ada_kernel(c_ref, w_ref, b_ref, o_ref):
    a = _silu(c_ref[...]).astype(BF16)
    o_ref[0] = jnp.dot(a, w_ref[0].astype(BF16), preferred_element_type=F32) + b_ref[0]


def _ada(conds, ada_w, ada_b):
    depth, d, n = ada_w.shape
    rows = conds.shape[0]
    tn = _tile(n, 512, 128)
    return pl.pallas_call(
        _ada_kernel,
        grid=(depth, n // tn),
        in_specs=[pl.BlockSpec((rows, d), lambda l, j: (0, 0)),
                  pl.BlockSpec((1, d, tn), lambda l, j: (l, 0, j)),
                  pl.BlockSpec((1, 1, tn), lambda l, j: (l, 0, j))],
        out_specs=pl.BlockSpec((1, rows, tn), lambda l, j: (l, 0, j)),
        out_shape=jax.ShapeDtypeStruct((depth, rows, n), F32),
        compiler_params=_cparams(("arbitrary", "arbitrary")),
        name="ada_mod",
    )(conds, ada_w, ada_b.reshape(depth, 1, n))


def _inproj_kernel(x_ref, g_ref, shift_ref, scale_ref, w_ref, o_ref, u_scr):
    @pl.when(pl.program_id(1) == 0)
    def _():
        x = x_ref[...]
        r = lax.rsqrt(jnp.mean(x * x, axis=-1, keepdims=True) + EPS)
        u = x * r * g_ref[...]
        u_scr[...] = (u * (1.0 + scale_ref[0]) + shift_ref[0]).astype(BF16)

    o_ref[...] = jnp.dot(u_scr[...], w_ref[...].astype(BF16), preferred_element_type=F32).astype(o_ref.dtype)


def _cond_index(trunk, tm):
    batch, seq_len, cond0, per_seq = trunk
    if per_seq:
        return lambda i: cond0 + (i * tm) // seq_len
    return lambda i: cond0


def _cond_row_tile(trunk, pref):
    batch, seq_len, _, per_seq = trunk
    if per_seq:
        return _tile(seq_len, pref, 16)
    return _tile(batch * seq_len, pref, 16)


def _inproj(x, norm_g, mod, trunk, w, out_dtype):
    t, d = x.shape
    n = w.shape[1]
    tm = _cond_row_tile(trunk, 1024)
    tn = _tile(n, 1024, 128)
    cidx = _cond_index(trunk, tm)
    return pl.pallas_call(
        _inproj_kernel,
        grid=(t // tm, n // tn),
        in_specs=[pl.BlockSpec((tm, d), lambda i, j: (i, 0)),
                  pl.BlockSpec((1, d), lambda i, j: (0, 0)),
                  pl.BlockSpec((1, 1, d), lambda i, j: (cidx(i), 0, 0)),
                  pl.BlockSpec((1, 1, d), lambda i, j: (cidx(i), 0, 1)),
                  pl.BlockSpec((d, tn), lambda i, j: (0, j))],
        out_specs=pl.BlockSpec((tm, tn), lambda i, j: (i, j)),
        out_shape=jax.ShapeDtypeStruct((t, n), out_dtype),
        scratch_shapes=[pltpu.VMEM((tm, d), BF16)],
        compiler_params=_cparams(("arbitrary", "arbitrary")),
        name="in_proj",
    )(x, norm_g.reshape(1, d), mod, mod, w)


def _outproj_epilogue(acc, x_ref, g_ref, gate_ref, o_ref):
    y = acc
    r = lax.rsqrt(jnp.mean(y * y, axis=-1, keepdims=True) + EPS)
    o_ref[...] = x_ref[...] + gate_ref[0] * (y * r * g_ref[...])


def _outproj_kernel(y_ref, w_ref, x_ref, g_ref, gate_ref, o_ref, acc):
    k = pl.program_id(1)

    @pl.when(k == 0)
    def _():
        acc[...] = jnp.zeros_like(acc)

    acc[...] += jnp.dot(y_ref[...], w_ref[...].astype(BF16), preferred_element_type=F32)

    @pl.when(k == pl.num_programs(1) - 1)
    def _():
        _outproj_epilogue(acc[...], x_ref, g_ref, gate_ref, o_ref)


def _outproj(y, w, x, norm_g, mod, trunk):
    t, e = y.shape
    d = w.shape[1]
    tm = _cond_row_tile(trunk, 512)
    tk = _tile(e, 1024, 128)
    cidx = _cond_index(trunk, tm)
    return pl.pallas_call(
        _outproj_kernel,
        grid=(t // tm, e // tk),
        in_specs=[pl.BlockSpec((tm, tk), lambda i, k: (i, k)),
                  pl.BlockSpec((tk, d), lambda i, k: (k, 0)),
                  pl.BlockSpec((tm, d), lambda i, k: (i, 0)),
                  pl.BlockSpec((1, d), lambda i, k: (0, 0)),
                  pl.BlockSpec((1, 1, d), lambda i, k: (cidx(i), 0, 2))],
        out_specs=pl.BlockSpec((tm, d), lambda i, k: (i, 0)),
        out_shape=jax.ShapeDtypeStruct((t, d), F32),
        scratch_shapes=[pltpu.VMEM((tm, d), F32)],
        compiler_params=_cparams(("arbitrary", "arbitrary")),
        name="out_proj",
    )(y, w, x, norm_g.reshape(1, d), mod)


def _conv3(x, prev_row, next_row, w, b, row0, seq_len):
    tm = x.shape[0]
    rows = lax.broadcasted_iota(jnp.int32, x.shape, 0)
    pos = (rows + row0) % seq_len
    prev = jnp.where(rows == 0, prev_row, pltpu.roll(x, 1, 0))
    nxt = jnp.where(rows == tm - 1, next_row, pltpu.roll(x, tm - 1, 0))
    prev = jnp.where(pos == 0, 0.0, prev)
    nxt = jnp.where(pos == seq_len - 1, 0.0, nxt)
    return w[0:1] * prev + w[1:2] * x + w[2:3] * nxt + b


def _halo_specs(tm, tc, nrows, colblk):
    nh = nrows // HALO
    r = tm // HALO
    return [pl.BlockSpec((tm, tc), lambda i, j: (i, colblk(j))),
            pl.BlockSpec((HALO, tc), lambda i, j: (jnp.maximum(i * r - 1, 0), colblk(j))),
            pl.BlockSpec((HALO, tc), lambda i, j: (jnp.minimum((i + 1) * r, nh - 1), colblk(j)))]


def _hy_inproj_kernel(seq_len, x_ref, xp_ref, xn_ref, g_ref, shift_ref, scale_ref,
                      w0_ref, w1_ref, w2_ref, wz_ref, cw, cb, vx_ref, g0_ref, u_scr):
    tm = x_ref.shape[0]
    ext = tm + 2 * HALO

    @pl.when(pl.program_id(1) == 0)
    def _():
        def modulated(x):
            r = lax.rsqrt(jnp.mean(x * x, axis=-1, keepdims=True) + EPS)
            return ((x * r * g_ref[...]) * (1.0 + scale_ref[0]) + shift_ref[0]).astype(BF16)

        u_scr[0:HALO] = modulated(xp_ref[...])
        u_scr[HALO:HALO + tm] = modulated(x_ref[...])
        u_scr[HALO + tm:ext] = modulated(xn_ref[...])

    u = u_scr[...]
    rows = lax.broadcasted_iota(jnp.int32, (tm, 1), 0)
    pos = (rows + pl.program_id(0) * tm) % seq_len
    first = pos == 0
    last = pos == seq_len - 1

    def conv(w_ref, k):
        r = jnp.dot(u, w_ref[...].astype(BF16), preferred_element_type=F32)
        prev = jnp.where(first, 0.0, pltpu.roll(r, 1, 0)[HALO:HALO + tm])
        nxt = jnp.where(last, 0.0, pltpu.roll(r, ext - 1, 0)[HALO:HALO + tm])
        w = cw[k]
        return w[0:1] * prev + w[1:2] * r[HALO:HALO + tm] + w[2:3] * nxt + cb[k]

    x1c = conv(w1_ref, 1)
    vc = conv(w2_ref, 2)
    vx_ref[...] = (vc * x1c).astype(vx_ref.dtype)
    x0c = conv(w0_ref, 0)
    z = jnp.dot(u_scr[HALO:HALO + tm], wz_ref[...].astype(BF16), preferred_element_type=F32)
    g0_ref[...] = (x0c * _silu(z)).astype(g0_ref.dtype)


def _hy_inproj(x, norm_g, mod, trunk, w, conv_w, conv_b, e):
    t, d = x.shape
    seq_len = trunk[1]
    tm = _cond_row_tile(trunk, 1024)
    tc = _tile(e, 256, 128)
    nb = e // tc
    cidx = _cond_index(trunk, tm)
    nh = t // HALO
    r = tm // HALO
    wspec = lambda part: pl.BlockSpec((d, tc), lambda i, j: (0, part * nb + j))
    cw = conv_w.reshape(3, 3, e).transpose(1, 0, 2)
    cb = conv_b.reshape(3, 1, e)
    return pl.pallas_call(
        functools.partial(_hy_inproj_kernel, seq_len),
        grid=(t // tm, nb),
        in_specs=[pl.BlockSpec((tm, d), lambda i, j: (i, 0)),
                  pl.BlockSpec((HALO, d), lambda i, j: (jnp.maximum(i * r - 1, 0), 0)),
                  pl.BlockSpec((HALO, d), lambda i, j: (jnp.minimum((i + 1) * r, nh - 1), 0)),
                  pl.BlockSpec((1, d), lambda i, j: (0, 0)),
                  pl.BlockSpec((1, 1, d), lambda i, j: (cidx(i), 0, 0)),
                  pl.BlockSpec((1, 1, d), lambda i, j: (cidx(i), 0, 1)),
                  wspec(0), wspec(1), wspec(2), wspec(3),
                  pl.BlockSpec((3, 3, tc), lambda i, j: (0, 0, j)),
                  pl.BlockSpec((3, 1, tc), lambda i, j: (0, 0, j))],
        out_specs=[pl.BlockSpec((tm, tc), lambda i, j: (i, j))] * 2,
        out_shape=[jax.ShapeDtypeStruct((t, e), BF16), jax.ShapeDtypeStruct((t, e), BF16)],
        scratch_shapes=[pltpu.VMEM((tm + 2 * HALO, d), BF16)],
        compiler_params=_cparams(("arbitrary", "arbitrary")),
        name="hy_in_proj_gate",
    )(x, x, x, norm_g.reshape(1, d), mod, mod, w, w, w, w, cw, cb)


def _filter_tables(seq_len):
    l = seq_len
    t = np.linspace(0.0, 1.0, l)
    w = 2.0 * math.pi * np.arange(l) / l
    f = np.linspace(1e-4, N_BANDS - 1, N_BANDS)
    z = np.concatenate([t[:, None], np.cos(f[None] * w[:, None]), -np.sin(f[None] * w[:, None])], axis=-1)
    pos = np.concatenate([np.arange(l), [0], np.arange(l - 1, 0, -1)])
    z2 = np.zeros((2 * l, 128), np.float32)
    z2[:, :z.shape[1]] = z[pos]
    aux = np.zeros((2 * l, 128), np.float32)
    aux[:, 0] = t[pos]
    aux[:, 1] = 1.0
    aux[l, 1] = 0.0
    return jnp.asarray(z2), jnp.asarray(aux)


def _filter_ffn_kernel(z_ref, w1, b1, w2, b2, w3, b3, fr, o_ref):
    f = fr[...]
    h = jnp.sin(f * (jnp.dot(z_ref[...], w1[...], precision=HI, preferred_element_type=F32) + b1[...]))
    h = jnp.sin(f * (jnp.dot(h, w2[...], precision=HI, preferred_element_type=F32) + b2[...]))
    h = jnp.sin(f * (jnp.dot(h, w3[...], precision=HI, preferred_element_type=F32) + b3[...]))
    o_ref[...] = h.astype(o_ref.dtype)


def _filter_expand_kernel(h_ref, aux_ref, w4, dec, o_ref):
    k = jnp.dot(h_ref[...], w4[...].astype(BF16), preferred_element_type=F32)
    t = aux_ref[:, 0:1]
    keep = aux_ref[:, 1:2]
    o_ref[...] = (k * jnp.exp(-t * jnp.abs(dec[0])) * keep).astype(o_ref.dtype)


def _filters(seq_len, w1, b1, w2, b2, w3, b3, w4, freq, decay):
    e = decay.shape[-1]
    fo = w2.shape[0]
    z2, aux = _filter_tables(seq_len)
    w1p = jnp.zeros((128, fo), F32).at[:w1.shape[0]].set(w1)
    tr = _tile(seq_len, 1024, 8)
    tc = _tile(e, 2048, 128)
    nb = e // tc
    nr = seq_len // tr
    small = lambda a: pl.BlockSpec(a.shape, lambda *_: (0,) * a.ndim)
    b1r, b2r, b3r, frr = (a.reshape(1, fo) for a in (b1, b2, b3, freq))
    h3 = pl.pallas_call(
        _filter_ffn_kernel,
        grid=(2 * nr,),
        in_specs=[pl.BlockSpec((tr, 128), lambda i: (i, 0)),
                  small(w1p), small(b1r), small(w2), small(b2r), small(w3), small(b3r), small(frr)],
        out_specs=pl.BlockSpec((tr, fo), lambda i: (i, 0)),
        out_shape=jax.ShapeDtypeStruct((2 * seq_len, fo), BF16),
        compiler_params=_cparams(("arbitrary",)),
        name="hy_filter_ffn",
    )(z2, w1p, b1r, w2, b2r, w3, b3r, frr)
    return pl.pallas_call(
        _filter_expand_kernel,
        grid=(2 * nr, nb),
        in_specs=[pl.BlockSpec((tr, fo), lambda i, j: (i, 0)),
                  pl.BlockSpec((tr, 128), lambda i, j: (i, 0)),
                  pl.BlockSpec((fo, tc), lambda i, j: (0, (i // nr) * nb + j)),
                  pl.BlockSpec((1, 1, tc), lambda i, j: (i // nr, 0, j))],
        out_specs=pl.BlockSpec((tr, tc), lambda i, j: (i, j)),
        out_shape=jax.ShapeDtypeStruct((2 * seq_len, e), BF16),
        compiler_params=_cparams(("arbitrary", "arbitrary")),
        name="hy_filter_expand",
    )(h3, aux, w4, decay.reshape(2, 1, e))


def _cmm_kernel(w_ref, x_ref, o_ref):
    o_ref[0] = jnp.dot(w_ref[...], x_ref[0], preferred_element_type=F32).astype(o_ref.dtype)


def _cmm(w, x, out_dtype):
    m, k = w.shape
    b, _, n = x.shape
    tn = _tile(n, 4096, 128)
    return pl.pallas_call(
        _cmm_kernel,
        grid=(b, n // tn),
        in_specs=[pl.BlockSpec((m, k), lambda bi, j: (0, 0)),
                  pl.BlockSpec((1, k, tn), lambda bi, j: (bi, 0, j))],
        out_specs=pl.BlockSpec((1, m, tn), lambda bi, j: (bi, 0, j)),
        out_shape=jax.ShapeDtypeStruct((b, m, n), out_dtype),
        compiler_params=_cparams(("arbitrary", "arbitrary")),
        name="const_lhs_matmul",
    )(w, x)


def _direct_mats(seq_len):
    l = seq_len
    n = 2 * l
    f = np.arange(l)[:, None]
    t = np.arange(n)[None, :]
    ang = 2.0 * math.pi * ((f * t) % n) / n
    fwd = np.concatenate([np.cos(ang), -np.sin(ang)], axis=0)
    fwd[l] = np.cos(math.pi * np.arange(n))
    tt = np.arange(l)[:, None]
    ff = np.arange(l)[None, :]
    ang2 = 2.0 * math.pi * ((tt * ff) % n) / n
    wgt = np.where(ff == 0, 1.0, 2.0) / n
    inv = np.concatenate([wgt * np.cos(ang2), -wgt * np.sin(ang2)], axis=1)
    inv[:, l] = np.cos(math.pi * np.arange(l)) / n
    return fwd, inv


def _direct_conv_kernel(seq_len, f_ref, v_ref, x_ref, k_ref, g0_ref, bias_ref, o_ref):
    l = seq_len
    x = x_ref[0]
    s = jnp.dot(f_ref[...], x, preferred_element_type=F32)
    sre, sim = s[:l], s[l:]
    kre, kim = k_ref[:l, :], k_ref[l:, :]
    row0 = lax.broadcasted_iota(jnp.int32, sre.shape, 0) == 0
    yre = sre * kre - jnp.where(row0, 0.0, sim * kim)
    yim = jnp.where(row0, sim * kim, sre * kim + sim * kre)
    y = jnp.concatenate([yre, yim], axis=0).astype(BF16)
    out = jnp.dot(v_ref[...], y, preferred_element_type=F32)
    out = out + x.astype(F32) * bias_ref[...]
    o_ref[0] = (out * g0_ref[0].astype(F32)).astype(o_ref.dtype)


def _longconv_direct(vx, g0, kfilt, bias, batch, seq_len):
    e = vx.shape[-1]
    l = seq_len
    fwd, inv = _direct_mats(l)
    f_full = jnp.asarray(fwd, BF16)
    f_data = jnp.asarray(fwd[:, :l], BF16)
    v_mat = jnp.asarray(inv, BF16)
    kspec = _cmm(f_full, kfilt.astype(BF16)[None], F32)[0]
    tc = _tile(e, 1024, 128)
    out = pl.pallas_call(
        functools.partial(_direct_conv_kernel, l),
        grid=(e // tc, batch),
        in_specs=[pl.BlockSpec((2 * l, l), lambda j, b: (0, 0)),
                  pl.BlockSpec((l, 2 * l), lambda j, b: (0, 0)),
                  pl.BlockSpec((1, l, tc), lambda j, b: (b, 0, j)),
                  pl.BlockSpec((2 * l, tc), lambda j, b: (0, j)),
                  pl.BlockSpec((1, l, tc), lambda j, b: (b, 0, j)),
                  pl.BlockSpec((1, tc), lambda j, b: (0, j))],
        out_specs=pl.BlockSpec((1, l, tc), lambda j, b: (b, 0, j)),
        out_shape=jax.ShapeDtypeStruct((batch, l, e), BF16),
        compiler_params=_cparams(("arbitrary", "arbitrary")),
        name="longconv_direct",
    )(f_data, v_mat, vx.reshape(batch, l, e), kspec, g0.reshape(batch, l, e), bias.reshape(1, e))
    return out.reshape(batch * l, e)


def _two_level_mats(seq_len):
    n = 2 * seq_len
    n2 = FFT_N2
    n1 = n // n2
    h2 = n2 // 2
    ns = 8 * ((h2 + 1 + 7) // 8)
    f2 = np.arange(h2 + 1)[:, None]
    t2 = np.arange(n2)[None, :]
    ang = 2.0 * math.pi * ((f2 * t2) % n2) / n2
    f1m = np.zeros((2 * ns, n2))
    f1m[0:2 * (h2 + 1):2] = np.cos(ang)
    f1m[1:2 * (h2 + 1):2] = -np.sin(ang)
    wgt = np.where((f2 == 0) | (f2 == h2), 1.0, 2.0) / n
    g1m = np.zeros((h2, 2 * ns))
    g1m[:, 0:2 * (h2 + 1):2] = (wgt * np.cos(ang[:, :h2])).T
    g1m[:, 1:2 * (h2 + 1):2] = -(wgt * np.sin(ang[:, :h2])).T
    t1 = np.arange(n1)[None, :]
    f1 = np.arange(n1)[:, None]
    rm = np.zeros((ns, 2 * n1, 2 * n1))
    pm = np.zeros((ns, 2 * n1, 2 * n1))
    for s in range(h2 + 1):
        a = 2.0 * math.pi * ((t1 * (n2 * f1 + s)) % n) / n
        mr, mi = np.cos(a), -np.sin(a)
        rm[s] = np.block([[mr, -mi], [mi, mr]])
        pr, pi = mr.T, -mi.T
        pm[s] = np.block([[pr, -pi], [pi, pr]])
    return n1, n2, ns, f1m, g1m, rm, pm


def _slab_spec_kernel(r_ref, a_ref, o_ref):
    n1 = a_ref.shape[3]
    a = jnp.concatenate([a_ref[0, 0, 0], a_ref[0, 0, 1]], axis=0)
    s = jnp.dot(r_ref[0], a, preferred_element_type=F32)
    o_ref[0, 0] = s[:n1].astype(o_ref.dtype)
    o_ref[0, 1] = s[n1:].astype(o_ref.dtype)


def _slab_conv_kernel(r_ref, p_ref, a_ref, k_ref, o_ref):
    n1 = a_ref.shape[3]
    kre, kim = k_ref[0, 0].astype(F32), k_ref[0, 1].astype(F32)
    for b in range(a_ref.shape[0]):
        a = jnp.concatenate([a_ref[b, 0, 0], a_ref[b, 0, 1]], axis=0)
        s = jnp.dot(r_ref[0], a, preferred_element_type=F32)
        sre, sim = s[:n1], s[n1:]
        y = jnp.concatenate([sre * kre - sim * kim, sre * kim + sim * kre], axis=0).astype(BF16)
        o = jnp.dot(p_ref[0], y, preferred_element_type=F32)
        o_ref[b, 0, 0] = o[:n1].astype(o_ref.dtype)
        o_ref[b, 0, 1] = o[n1:].astype(o_ref.dtype)


def _dft_l1_kernel(k_ref, x_ref, o_ref):
    x = x_ref[0].astype(F32)
    nt2, _, tc = x.shape
    halves = []
    for g in range(SUB_TILES):
        xg = x[:, SUBLANES * g:SUBLANES * (g + 1), :].reshape(nt2 * SUBLANES, tc).astype(BF16)
        a = jnp.dot(k_ref[...], xg, preferred_element_type=F32)
        halves.append(a.reshape(-1, SUBLANES, tc))
    o_ref[0] = jnp.concatenate(halves, axis=1).astype(o_ref.dtype)


def _dft_l1_inv_kernel(g_ref, a_ref, vx_ref, g0_ref, bias_ref, o_ref):
    a = a_ref[0].astype(F32)
    nr, _, tc = a.shape
    halves = []
    for g in range(SUB_TILES):
        ag = a[:, SUBLANES * g:SUBLANES * (g + 1), :].reshape(nr * SUBLANES, tc).astype(BF16)
        y = jnp.dot(g_ref[...], ag, preferred_element_type=F32)
        halves.append(y.reshape(-1, SUBLANES, tc))
    y = jnp.concatenate(halves, axis=1)
    y = y + vx_ref[0].astype(F32) * bias_ref[...]
    o_ref[0] = (y * g0_ref[0].astype(F32)).astype(o_ref.dtype)


def _dft_l1(kmat, x4, out_dtype):
    b, nt2, n1, e = x4.shape
    m = kmat.shape[0] // SUBLANES
    tc = _tile(e, 1024, 128)
    rows = SUBLANES * SUB_TILES
    return pl.pallas_call(
        _dft_l1_kernel,
        grid=(b, n1 // rows, e // tc),
        in_specs=[pl.BlockSpec(kmat.shape, lambda bi, i, j: (0, 0)),
                  pl.BlockSpec((1, nt2, rows, tc), lambda bi, i, j: (bi, 0, i, j))],
        out_specs=pl.BlockSpec((1, m, rows, tc), lambda bi, i, j: (bi, 0, i, j)),
        out_shape=jax.ShapeDtypeStruct((b, m, n1, e), out_dtype),
        compiler_params=_cparams(("arbitrary", "arbitrary", "arbitrary")),
        name="dft_level1",
    )(kmat, x4)


def _longconv_two_level(vx, g0, kfilt, bias, batch, seq_len):
    e = vx.shape[-1]
    l = seq_len
    n1, n2, ns, f1m, g1m, rm, pm = _two_level_mats(l)
    h2 = n2 // 2
    eye = np.eye(SUBLANES)
    k_full = jnp.asarray(np.kron(f1m, eye), BF16)
    k_data = jnp.asarray(np.kron(f1m[:, :h2], eye), BF16)
    g8 = jnp.asarray(np.kron(g1m, eye), BF16)
    rmat = jnp.asarray(rm, BF16)
    pmat = jnp.asarray(pm, BF16)
    tc = _tile(e, 4096, 128)
    nc = e // tc

    ka = _dft_l1(k_full, kfilt.reshape(1, n2, n1, e), BF16).reshape(1, ns, 2, n1, e)
    kspec = pl.pallas_call(
        _slab_spec_kernel,
        grid=(ns, nc),
        in_specs=[pl.BlockSpec((1, 2 * n1, 2 * n1), lambda s, j: (s, 0, 0)),
                  pl.BlockSpec((1, 1, 2, n1, tc), lambda s, j: (0, s, 0, 0, j))],
        out_specs=pl.BlockSpec((1, 2, n1, tc), lambda s, j: (s, 0, 0, j)),
        out_shape=jax.ShapeDtypeStruct((ns, 2, n1, e), BF16),
        compiler_params=_cparams(("arbitrary", "arbitrary")),
        name="filter_slab_dft",
    )(rmat, ka)

    vx4 = vx.reshape(batch, h2, n1, e)
    a = _dft_l1(k_data, vx4, BF16).reshape(batch, ns, 2, n1, e)
    a2 = pl.pallas_call(
        _slab_conv_kernel,
        grid=(ns, nc),
        in_specs=[pl.BlockSpec((1, 2 * n1, 2 * n1), lambda s, j: (s, 0, 0)),
                  pl.BlockSpec((1, 2 * n1, 2 * n1), lambda s, j: (s, 0, 0)),
                  pl.BlockSpec((batch, 1, 2, n1, tc), lambda s, j: (0, s, 0, 0, j)),
                  pl.BlockSpec((1, 2, n1, tc), lambda s, j: (s, 0, 0, j))],
        out_specs=pl.BlockSpec((batch, 1, 2, n1, tc), lambda s, j: (0, s, 0, 0, j)),
        out_shape=jax.ShapeDtypeStruct((batch, ns, 2, n1, e), BF16),
        compiler_params=_cparams(("arbitrary", "arbitrary")),
        name="slab_conv",
    )(rmat, pmat, a, kspec)
    a2 = a2.reshape(batch, 2 * ns, n1, e)
    rows = SUBLANES * SUB_TILES
    tci = _tile(e, 1024, 128)
    blk = lambda r: pl.BlockSpec((1, r, rows, tci), lambda bi, i, j: (bi, 0, i, j))
    out = pl.pallas_call(
        _dft_l1_inv_kernel,
        grid=(batch, n1 // rows, e // tci),
        in_specs=[pl.BlockSpec(g8.shape, lambda bi, i, j: (0, 0)),
                  blk(2 * ns), blk(h2), blk(h2),
                  pl.BlockSpec((1, 1, tci), lambda bi, i, j: (0, 0, j))],
        out_specs=blk(h2),
        out_shape=jax.ShapeDtypeStruct((batch, h2, n1, e), BF16),
        compiler_params=_cparams(("arbitrary", "arbitrary", "arbitrary")),
        name="dft_level1_inverse",
    )(g8, a2, vx4, g0.reshape(batch, h2, n1, e), bias.reshape(1, 1, e))
    return out.reshape(batch * l, e)


def _longconv(vx, g0, kfilt, bias, batch, seq_len):
    if seq_len <= DIRECT_CONV_MAX_L:
        return _longconv_direct(vx, g0, kfilt, bias, batch, seq_len)
    return _longconv_two_level(vx, g0, kfilt, bias, batch, seq_len)


def _ml_qkv_kernel(seq_len, n_heads, dk, x_ref, xp_ref, xn_ref, cw_ref, cb_ref, wq_ref, wk_ref, wv_ref,
                   wgq_ref, wgk_ref, wgv_ref, bg_ref, q_ref, k_ref, kt_ref, v_ref, xc_ref, g_ref, gt_ref, gacc):
    h = pl.program_id(1)
    tm = x_ref.shape[0]
    q_len = gt_ref.shape[2]
    x = x_ref[...].astype(F32)
    conv = _conv3(x, xp_ref[HALO - 1:HALO, :].astype(F32), xn_ref[0:1, :].astype(F32),
                  cw_ref[...], cb_ref[...], pl.program_id(0) * tm, seq_len)
    xc = _silu(conv)
    xc_ref[...] = xc.astype(xc_ref.dtype)
    xcb = xc.astype(BF16)
    q = jnp.dot(xcb, wq_ref[0], preferred_element_type=F32).astype(BF16)
    kf = jnp.dot(xcb, wk_ref[0], preferred_element_type=F32) * (dk ** -0.5)
    k = kf.astype(BF16)
    v = jnp.dot(x.astype(BF16), wv_ref[0], preferred_element_type=F32).astype(BF16)
    q_ref[...] = q
    k_ref[...] = k
    v_ref[...] = v
    for c in range(tm // q_len):
        kt_ref[c] = kf[c * q_len:(c + 1) * q_len, :].T.astype(BF16)
    part = (jnp.dot(q, wgq_ref[0], preferred_element_type=F32)
            + jnp.dot(k, wgk_ref[0], preferred_element_type=F32)
            + jnp.dot(v, wgv_ref[0], preferred_element_type=F32))

    @pl.when(h == 0)
    def _():
        gacc[...] = part + bg_ref[...]

    @pl.when(h > 0)
    def _():
        gacc[...] += part

    @pl.when(h == n_heads - 1)
    def _():
        g = gacc[...]
        col = lax.broadcasted_iota(jnp.int32, g.shape, 1)
        is_forget = (col % (2 * n_heads)) >= n_heads
        g = jnp.where(is_forget, jax.nn.log_sigmoid(g), g)
        col_q = lax.broadcasted_iota(jnp.int32, (q_len, g.shape[1]), 1)
        forget_q = (col_q % (2 * n_heads)) >= n_heads
        bwd_q = (col_q // (2 * n_heads)) == 1
        r_i = lax.broadcasted_iota(jnp.int32, (q_len, q_len), 0)
        c_i = lax.broadcasted_iota(jnp.int32, (q_len, q_len), 1)
        tril = (c_i <= r_i).astype(F32)
        for c in range(tm // q_len):
            gc = g[c * q_len:(c + 1) * q_len, :]
            pre = jnp.dot(tril, gc, precision=HI, preferred_element_type=F32)
            tot = jnp.sum(gc, axis=0, keepdims=True)
            suf = tot - pre + gc
            gc = jnp.where(forget_q, jnp.where(bwd_q, suf, pre), gc)
            g_ref[c * q_len:(c + 1) * q_len, :] = gc
            gt_ref[c] = gc.T


def _ml_qkv(p, conv_w, conv_b, wq, wk, wv, w_gate, b_gate, batch, seq_len, e):
    t = p.shape[0]
    n_heads, dh, dk = wq.shape
    dv = wv.shape[2]
    q_len = min(SCAN_CHUNK, seq_len)
    tm = _row_tile(batch, seq_len, 1024, q_len)
    ng = w_gate.shape[1]
    gpad = 128
    wgq = jnp.zeros((n_heads, dk, gpad), BF16).at[:, :, :ng].set(
        w_gate[:n_heads * dk].reshape(n_heads, dk, ng).astype(BF16))
    wgk = jnp.zeros((n_heads, dk, gpad), BF16).at[:, :, :ng].set(
        w_gate[n_heads * dk:2 * n_heads * dk].reshape(n_heads, dk, ng).astype(BF16))
    wgv = jnp.zeros((n_heads, dv, gpad), BF16).at[:, :, :ng].set(
        w_gate[2 * n_heads * dk:].reshape(n_heads, dv, ng).astype(BF16))
    bg = jnp.zeros((1, gpad), F32).at[0, :ng].set(b_gate)
    specs = _halo_specs(tm, dh, t, lambda j: j)
    specs += [pl.BlockSpec((3, dh), lambda i, j: (0, j)),
              pl.BlockSpec((1, dh), lambda i, j: (0, j)),
              pl.BlockSpec((1, dh, dk), lambda i, j: (j, 0, 0)),
              pl.BlockSpec((1, dh, dk), lambda i, j: (j, 0, 0)),
              pl.BlockSpec((1, dh, dv), lambda i, j: (j, 0, 0)),
              pl.BlockSpec((1, dk, gpad), lambda i, j: (j, 0, 0)),
              pl.BlockSpec((1, dk, gpad), lambda i, j: (j, 0, 0)),
              pl.BlockSpec((1, dv, gpad), lambda i, j: (j, 0, 0)),
              pl.BlockSpec((1, gpad), lambda i, j: (0, 0))]
    return pl.pallas_call(
        functools.partial(_ml_qkv_kernel, seq_len, n_heads, dk),
        grid=(t // tm, n_heads),
        in_specs=specs,
        out_specs=[pl.BlockSpec((tm, dk), lambda i, j: (i, j)),
                   pl.BlockSpec((tm, dk), lambda i, j: (i, j)),
                   pl.BlockSpec((tm // q_len, dk, q_len), lambda i, j: (i, j, 0)),
                   pl.BlockSpec((tm, dv), lambda i, j: (i, j)),
                   pl.BlockSpec((tm, dh), lambda i, j: (i, j)),
                   pl.BlockSpec((tm, gpad), lambda i, j: (i, 0)),
                   pl.BlockSpec((tm // q_len, gpad, q_len), lambda i, j: (i, 0, 0))],
        out_shape=[jax.ShapeDtypeStruct((t, n_heads * dk), BF16),
                   jax.ShapeDtypeStruct((t, n_heads * dk), BF16),
                   jax.ShapeDtypeStruct((t // q_len, n_heads * dk, q_len), BF16),
                   jax.ShapeDtypeStruct((t, n_heads * dv), BF16),
                   jax.ShapeDtypeStruct((t, e), BF16),
                   jax.ShapeDtypeStruct((t, gpad), F32),
                   jax.ShapeDtypeStruct((t // q_len, gpad, q_len), F32)],
        scratch_shapes=[pltpu.VMEM((tm, gpad), F32)],
        compiler_params=_cparams(("arbitrary", "arbitrary")),
        name="ml_qkv_gates",
    )(p, p, p, conv_w, conv_b.reshape(1, e), wq.astype(BF16), wk.astype(BF16), wv.astype(BF16),
      wgq, wgk, wgv, bg)


def _ml_scan_kernel(n_heads, has_init, *refs):
    if has_init:
        (q_ref, k_ref, kt_ref, v_ref, g_ref, gt_ref, c0_ref, n0_ref, m0_ref,
         h_ref, cout_ref, nout_ref, mout_ref, c_scr, n_scr, m_scr) = refs
    else:
        (q_ref, k_ref, kt_ref, v_ref, g_ref, gt_ref,
         h_ref, cout_ref, nout_ref, mout_ref, c_scr, n_scr, m_scr) = refs
    head = pl.program_id(1)
    seq_len = q_ref.shape[0]
    q_len = gt_ref.shape[2]
    n_chunks = seq_len // q_len

    if has_init:
        for d in range(2):
            c_scr[d] = c0_ref[0, 0, d, 0]
            n_scr[d] = n0_ref[0, 0, d]
            m_scr[d] = m0_ref[0, 0, d]
    h_ref[...] = jnp.zeros_like(h_ref)

    row_i = lax.broadcasted_iota(jnp.int32, (q_len, q_len), 0)
    col_i = lax.broadcasted_iota(jnp.int32, (q_len, q_len), 1)
    lane_g = lax.broadcasted_iota(jnp.int32, (q_len, g_ref.shape[1]), 1)

    def direction(d, c, from_zero=False):
        icol = d * 2 * n_heads + head
        fcol = icol + n_heads
        start = c * q_len if isinstance(c, int) else pl.multiple_of(c * q_len, q_len)
        rows = pl.ds(start, q_len)
        g = g_ref[rows, :]
        i_col = jnp.sum(jnp.where(lane_g == icol, g, 0.0), axis=1, keepdims=True)
        b_col = jnp.sum(jnp.where(lane_g == fcol, g, 0.0), axis=1, keepdims=True)
        i_row = gt_ref[c, pl.ds(icol, 1), :]
        b_row = gt_ref[c, pl.ds(fcol, 1), :]
        mask = (col_i <= row_i) if d == 0 else (col_i >= row_i)
        last = q_len - 1 if d == 0 else 0
        g_tot = jnp.sum(jnp.where(col_i[0:1, :] == last, b_row, 0.0), axis=1, keepdims=True)
        m_prev = 0.0 if from_zero else m_scr[d][:, 0:1]
        if not from_zero:
            n_prev = n_scr[d]
            c_prev = c_scr[d]

        qc = q_ref[rows, :]
        kc = k_ref[rows, :]
        vc = v_ref[rows, :]
        dmat = jnp.where(mask, b_col - b_row + i_row, -jnp.inf)
        inter = b_col + m_prev
        m_t = jnp.maximum(inter, jnp.max(dmat, axis=1, keepdims=True))
        qk = lax.dot_general(qc, kc, (((1,), (1,)), ((), ())), preferred_element_type=F32)
        s = qk * jnp.exp(dmat - m_t)
        if from_zero:
            num = jnp.dot(s.astype(BF16), vc, preferred_element_type=F32)
            qn = jnp.sum(s, axis=1, keepdims=True)
        else:
            w_inter = jnp.exp(inter - m_t)
            num = (w_inter * jnp.dot(qc, c_prev.astype(BF16), preferred_element_type=F32)
                   + jnp.dot(s.astype(BF16), vc, preferred_element_type=F32))
            qn = (w_inter * jnp.sum(qc.astype(F32) * n_prev, axis=1, keepdims=True)
                  + jnp.sum(s, axis=1, keepdims=True))
        den = jnp.maximum(jnp.abs(qn), jnp.exp(-m_t))
        h_ref[rows, :] += num / den

        a_col = g_tot - b_col + i_col
        a_row = g_tot - b_row + i_row
        m_new = jnp.maximum(g_tot + m_prev, jnp.max(a_row, axis=1, keepdims=True))
        kwt = (kt_ref[c].astype(F32) * jnp.exp(a_row - m_new)).astype(BF16)
        kv = jnp.dot(kwt, vc, preferred_element_type=F32)
        ksum = jnp.sum(kc.astype(F32) * jnp.exp(a_col - m_new), axis=0, keepdims=True)
        if from_zero:
            c_scr[d] = kv
            n_scr[d] = ksum
        else:
            dec = jnp.exp(g_tot + m_prev - m_new)
            c_scr[d] = dec * c_prev + kv
            n_scr[d] = dec * n_prev + ksum
        m_scr[d] = jnp.broadcast_to(m_new, m_scr.shape[1:])

    def body(j, carry):
        direction(0, j)
        direction(1, n_chunks - 1 - j)
        return carry

    first = 0
    if not has_init:
        direction(0, 0, from_zero=True)
        direction(1, n_chunks - 1, from_zero=True)
        first = 1
    lax.fori_loop(first, n_chunks, body, 0)
    for d in range(2):
        cout_ref[0, 0, d, 0] = c_scr[d]
        nout_ref[0, 0, d] = n_scr[d]
        mout_ref[0, 0, d] = m_scr[d]


def _ml_scan(q, k, kt, v, g, gt, batch, seq_len, n_heads, init=None):
    dk = q.shape[1] // n_heads
    dv = v.shape[1] // n_heads
    q_len = gt.shape[2]
    gpad = g.shape[1]
    nq = seq_len // q_len
    in_specs = [pl.BlockSpec((seq_len, dk), lambda b, h: (b, h)),
                pl.BlockSpec((seq_len, dk), lambda b, h: (b, h)),
                pl.BlockSpec((nq, dk, q_len), lambda b, h: (b, h, 0)),
                pl.BlockSpec((seq_len, dv), lambda b, h: (b, h)),
                pl.BlockSpec((seq_len, gpad), lambda b, h: (b, 0)),
                pl.BlockSpec((nq, gpad, q_len), lambda b, h: (b, 0, 0))]
    args = [q, k, kt, v, g, gt]
    if init is not None:
        c0, n0, m0 = init
        in_specs += [pl.BlockSpec((1, 1, 2, 1, dk, dv), lambda b, h: (b, 0, 0, h, 0, 0)),
                     pl.BlockSpec((1, 1, 2, 1, dk), lambda b, h: (b, h, 0, 0, 0)),
                     pl.BlockSpec((1, 1, 2, 1, 128), lambda b, h: (b, h, 0, 0, 0))]
        args += [c0, n0, m0]
    out = pl.pallas_call(
        functools.partial(_ml_scan_kernel, n_heads, init is not None),
        grid=(batch, n_heads),
        in_specs=in_specs,
        out_specs=[pl.BlockSpec((seq_len, dv), lambda b, h: (b, h)),
                   pl.BlockSpec((1, 1, 2, 1, dk, dv), lambda b, h: (b, 0, 0, h, 0, 0)),
                   pl.BlockSpec((1, 1, 2, 1, dk), lambda b, h: (b, h, 0, 0, 0)),
                   pl.BlockSpec((1, 1, 2, 1, 128), lambda b, h: (b, h, 0, 0, 0))],
        out_shape=[jax.ShapeDtypeStruct((batch * seq_len, n_heads * dv), F32),
                   jax.ShapeDtypeStruct((batch, 1, 2, n_heads, dk, dv), F32),
                   jax.ShapeDtypeStruct((batch, n_heads, 2, 1, dk), F32),
                   jax.ShapeDtypeStruct((batch, n_heads, 2, 1, 128), F32)],
        scratch_shapes=[pltpu.VMEM((2, dk, dv), F32), pltpu.VMEM((2, 1, dk), F32),
                        pltpu.VMEM((2, 1, 128), F32)],
        compiler_params=_cparams(("arbitrary", "arbitrary")),
        name="ml_scan",
    )(*args)
    return out


def _ml_out_kernel(h_ref, o_ref_in, xc_ref, z_ref, nw_ref, sk_ref, w_ref, x_ref, g_ref, gate_ref, out_ref, acc):
    k = pl.program_id(1)

    @pl.when(k == 0)
    def _():
        acc[...] = jnp.zeros_like(acc)

    hh = jax.nn.sigmoid(o_ref_in[...].astype(F32)) * h_ref[...]
    mu = jnp.mean(hh, axis=-1, keepdims=True)
    var = jnp.mean(jnp.square(hh - mu), axis=-1, keepdims=True)
    hn = (hh - mu) * lax.rsqrt(var + EPS) * nw_ref[...]
    hn = hn + sk_ref[...] * xc_ref[...].astype(F32)
    y = (hn * _silu(z_ref[...].astype(F32))).astype(BF16)
    acc[...] += jnp.dot(y, w_ref[...].astype(BF16), preferred_element_type=F32)

    @pl.when(k == pl.num_programs(1) - 1)
    def _():
        _outproj_epilogue(acc[...], x_ref, g_ref, gate_ref, out_ref)


def _ml_out(h, p, xc, norm_w, skip, w, x, norm_g, mod, trunk, n_heads):
    t, e = h.shape
    d = w.shape[1]
    dv = e // n_heads
    tm = _cond_row_tile(trunk, 1024)
    cidx = _cond_index(trunk, tm)
    return pl.pallas_call(
        _ml_out_kernel,
        grid=(t // tm, n_heads),
        in_specs=[pl.BlockSpec((tm, dv), lambda i, k: (i, k)),
                  pl.BlockSpec((tm, dv), lambda i, k: (i, 2 * n_heads + k)),
                  pl.BlockSpec((tm, dv), lambda i, k: (i, k)),
                  pl.BlockSpec((tm, dv), lambda i, k: (i, n_heads + k)),
                  pl.BlockSpec((1, dv), lambda i, k: (0, k)),
                  pl.BlockSpec((1, dv), lambda i, k: (0, k)),
                  pl.BlockSpec((dv, d), lambda i, k: (k, 0)),
                  pl.BlockSpec((tm, d), lambda i, k: (i, 0), pipeline_mode=pl.Buffered(1)),
                  pl.BlockSpec((1, d), lambda i, k: (0, 0)),
                  pl.BlockSpec((1, 1, d), lambda i, k: (cidx(i), 0, 2))],
        out_specs=pl.BlockSpec((tm, d), lambda i, k: (i, 0)),
        out_shape=jax.ShapeDtypeStruct((t, d), F32),
        scratch_shapes=[pltpu.VMEM((tm, d), F32)],
        compiler_params=_cparams(("arbitrary", "arbitrary")),
        name="ml_out_proj",
    )(h, p, xc, p, norm_w.reshape(1, e), skip.reshape(1, e), w, x, norm_g.reshape(1, d), mod)


def _hyena_layer(x, mod, trunk, norm_pre, norm_post, hw):
    (w_in, conv_w, conv_b, w1, b1, w2, b2, w3, b3, w4, freq, decay, bias, w_out) = hw
    batch, seq_len = trunk[0], trunk[1]
    e = w_out.shape[0]
    vx, g0 = _hy_inproj(x, norm_pre, mod, trunk, w_in, conv_w, conv_b, e)
    kfilt = _filters(seq_len, w1, b1, w2, b2, w3, b3, w4, freq, decay)
    y = _longconv(vx, g0, kfilt, bias, batch, seq_len)
    return _outproj(y, w_out.astype(BF16), x, norm_post, mod, trunk)


def _mlstm_layer(x, mod, trunk, norm_pre, norm_post, mw, init):
    (w_in, conv_w, conv_b, wq, wk, wv, w_gate, b_gate, norm_w, skip, w_out) = mw
    batch, seq_len = trunk[0], trunk[1]
    e = w_out.shape[0]
    n_heads = wq.shape[0]
    p = _inproj(x, norm_pre, mod, trunk, w_in, BF16)
    q, k, kt, v, xc, g, gt = _ml_qkv(p, conv_w, conv_b, wq, wk, wv, w_gate, b_gate, batch, seq_len, e)
    h, c_new, n_new, m_new = _ml_scan(q, k, kt, v, g, gt, batch, seq_len, n_heads, init)
    x_new = _ml_out(h, p, xc, norm_w, skip, w_out.astype(BF16), x, norm_post, mod, trunk, n_heads)
    return x_new, c_new, n_new, m_new


def kernel(x_prompt, x_sample, state_C, state_n, state_m, c, c_ctx, norm_pre, norm_post, ada_w, ada_b,
           hy_w_in, hy_conv_w, hy_conv_b, hy_ffn_w1, hy_ffn_b1, hy_ffn_w2, hy_ffn_b2, hy_ffn_w3, hy_ffn_b3,
           hy_ffn_w4, hy_sin_freq, hy_decay, hy_bias, hy_w_out, ml_w_in, ml_conv_w, ml_conv_b, ml_wq, ml_wk,
           ml_wv, ml_w_gate, ml_b_gate, ml_norm, ml_skip, ml_w_out):
    depth = norm_pre.shape[0]
    bp, lp, d = x_prompt.shape
    bs, ls, _ = x_sample.shape
    n_heads = ml_wq.shape[1]

    nrow = 8 * ((1 + bs + 7) // 8)
    conds = jnp.zeros((nrow, d), F32).at[0].set(c_ctx).at[1:1 + bs].set(c)
    mods = _ada(conds, ada_w, ada_b)

    def run_trunk(x3, cond0, per_seq, init_states):
        batch, seq_len, _ = x3.shape
        trunk = (batch, seq_len, cond0, per_seq)
        x = x3.reshape(batch * seq_len, d)
        new_c, new_n, new_m = [], [], []
        for i in range(depth):
            mod = mods[i].reshape(nrow, 1, 3 * d)
            j = i // 2
            if i % 2 == 0:
                hw = (hy_w_in[j], hy_conv_w[j], hy_conv_b[j], hy_ffn_w1[j], hy_ffn_b1[j], hy_ffn_w2[j],
                      hy_ffn_b2[j], hy_ffn_w3[j], hy_ffn_b3[j], hy_ffn_w4[j], hy_sin_freq[j], hy_decay[j],
                      hy_bias[j], hy_w_out[j])
                x = _hyena_layer(x, mod, trunk, norm_pre[i], norm_post[i], hw)
            else:
                mw = (ml_w_in[j], ml_conv_w[j], ml_conv_b[j], ml_wq[j], ml_wk[j], ml_wv[j], ml_w_gate[j],
                      ml_b_gate[j], ml_norm[j], ml_skip[j], ml_w_out[j])
                init = None if init_states is None else init_states(j)
                x, cj, nj, mj = _mlstm_layer(x, mod, trunk, norm_pre[i], norm_post[i], mw, init)
                new_c.append(cj)
                new_n.append(nj)
                new_m.append(mj)
        return x.reshape(batch, seq_len, d), new_c, new_n, new_m

    y_prompt, cs, ns, ms = run_trunk(x_prompt, 0, False, None)
    dk = ml_wq.shape[3]
    new_state_c = jnp.concatenate(cs, axis=1)
    new_state_n = jnp.stack([jnp.swapaxes(n[:, :, :, 0, :], 1, 2) for n in ns], axis=1)
    new_state_m = jnp.stack([jnp.swapaxes(m[:, :, :, 0, 0], 1, 2) for m in ms], axis=1)

    def lat_init(j):
        c0 = state_C[:, j:j + 1]
        n0 = jnp.swapaxes(state_n[:, j], 1, 2)[:, :, :, None, :]
        m0 = jnp.broadcast_to(jnp.swapaxes(state_m[:, j], 1, 2)[:, :, :, None, None],
                              (bs, n_heads, 2, 1, 128))
        return c0, n0, m0

    y_sample, _, _, _ = run_trunk(x_sample, 1, True, lat_init)
    return (y_prompt, y_sample, new_state_c, new_state_n, new_state_m)
```

```python
import functools
import math

import numpy as np
import jax
import jax.numpy as jnp
from jax import lax
from jax.experimental import pallas as pl
from jax.experimental.pallas import tpu as pltpu

F32 = jnp.float32
BF16 = jnp.bfloat16
EPS = 1e-6
N_BANDS = 16
SUBLANES = 8
SUB_TILES = 2
HALO = 16
SCAN_CHUNK = 256
HEADS_PER_STEP = 2
FFT_N2 = 128
DIRECT_CONV_MAX_L = 512
VMEM_LIMIT = 56 * 1024 * 1024
HI = lax.Precision.HIGHEST


def _cparams(sem):
    return pltpu.CompilerParams(dimension_semantics=sem, vmem_limit_bytes=VMEM_LIMIT)


def _tile(n, pref, mult):
    if n <= pref:
        return n
    t = (pref // mult) * mult
    while t >= mult:
        if n % t == 0:
            return t
        t -= mult
    return n


def _row_tile(batch, seq_len, pref, mult):
    if seq_len >= pref:
        return _tile(seq_len, pref, mult)
    k = max(1, pref // seq_len)
    while batch % k:
        k -= 1
    return seq_len * k


def _silu(x):
    return x * jax.nn.sigmoid(x)


def _ada_kernel(c_ref, w_ref, b_ref, o_ref):
    a = _silu(c_ref[...]).astype(BF16)
    o_ref[0] = jnp.dot(a, w_ref[0].astype(BF16), preferred_element_type=F32) + b_ref[0]


def _ada(conds, ada_w, ada_b):
    depth, d, n = ada_w.shape
    rows = conds.shape[0]
    tn = _tile(n, 512, 128)
    return pl.pallas_call(
        _ada_kernel,
        grid=(depth, n // tn),
        in_specs=[pl.BlockSpec((rows, d), lambda l, j: (0, 0)),
                  pl.BlockSpec((1, d, tn), lambda l, j: (l, 0, j)),
                  pl.BlockSpec((1, 1, tn), lambda l, j: (l, 0, j))],
        out_specs=pl.BlockSpec((1, rows, tn), lambda l, j: (l, 0, j)),
        out_shape=jax.ShapeDtypeStruct((depth, rows, n), F32),
        compiler_params=_cparams(("arbitrary", "arbitrary")),
        name="ada_mod",
    )(conds, ada_w, ada_b.reshape(depth, 1, n))


def _inproj_kernel(x_ref, g_ref, shift_ref, scale_ref, w_ref, o_ref, u_scr):
    @pl.when(pl.program_id(1) == 0)
    def _():
        x = x_ref[...]
        r = lax.rsqrt(jnp.mean(x * x, axis=-1, keepdims=True) + EPS)
        u = x * r * g_ref[...]
        u_scr[...] = (u * (1.0 + scale_ref[0]) + shift_ref[0]).astype(BF16)

    o_ref[...] = jnp.dot(u_scr[...], w_ref[...].astype(BF16), preferred_element_type=F32).astype(o_ref.dtype)


def _cond_index(trunk, tm):
    batch, seq_len, cond0, per_seq = trunk
    if per_seq:
        return lambda i: cond0 + (i * tm) // seq_len
    return lambda i: cond0


def _cond_row_tile(trunk, pref):
    batch, seq_len, _, per_seq = trunk
    if per_seq:
        return _tile(seq_len, pref, 16)
    return _tile(batch * seq_len, pref, 16)


def _inproj(x, norm_g, mod, trunk, w, out_dtype):
    t, d = x.shape
    n = w.shape[1]
    tm = _cond_row_tile(trunk, 1024)
    tn = _tile(n, 1024, 128)
    cidx = _cond_index(trunk, tm)
    return pl.pallas_call(
        _inproj_kernel,
        grid=(t // tm, n // tn),
        in_specs=[pl.BlockSpec((tm, d), lambda i, j: (i, 0)),
                  pl.BlockSpec((1, d), lambda i, j: (0, 0)),
                  pl.BlockSpec((1, 1, d), lambda i, j: (cidx(i), 0, 0)),
                  pl.BlockSpec((1, 1, d), lambda i, j: (cidx(i), 0, 1)),
                  pl.BlockSpec((d, tn), lambda i, j: (0, j))],
        out_specs=pl.BlockSpec((tm, tn), lambda i, j: (i, j)),
        out_shape=jax.ShapeDtypeStruct((t, n), out_dtype),
        scratch_shapes=[pltpu.VMEM((tm, d), BF16)],
        compiler_params=_cparams(("arbitrary", "arbitrary")),
        name="in_proj",
    )(x, norm_g.reshape(1, d), mod, mod, w)


def _outproj_epilogue(acc, x_ref, g_ref, gate_ref, o_ref):
    y = acc
    r = lax.rsqrt(jnp.mean(y * y, axis=-1, keepdims=True) + EPS)
    o_ref[...] = x_ref[...] + gate_ref[0] * (y * r * g_ref[...])


def _outproj_kernel(y_ref, w_ref, x_ref, g_ref, gate_ref, o_ref, acc):
    k = pl.program_id(1)

    @pl.when(k == 0)
    def _():
        acc[...] = jnp.zeros_like(acc)

    acc[...] += jnp.dot(y_ref[...], w_ref[...].astype(BF16), preferred_element_type=F32)

    @pl.when(k == pl.num_programs(1) - 1)
    def _():
        _outproj_epilogue(acc[...], x_ref, g_ref, gate_ref, o_ref)


def _outproj(y, w, x, norm_g, mod, trunk):
    t, e = y.shape
    d = w.shape[1]
    tm = _cond_row_tile(trunk, 512)
    tk = _tile(e, 1024, 128)
    cidx = _cond_index(trunk, tm)
    return pl.pallas_call(
        _outproj_kernel,
        grid=(t // tm, e // tk),
        in_specs=[pl.BlockSpec((tm, tk), lambda i, k: (i, k)),
                  pl.BlockSpec((tk, d), lambda i, k: (k, 0)),
                  pl.BlockSpec((tm, d), lambda i, k: (i, 0)),
                  pl.BlockSpec((1, d), lambda i, k: (0, 0)),
                  pl.BlockSpec((1, 1, d), lambda i, k: (cidx(i), 0, 2))],
        out_specs=pl.BlockSpec((tm, d), lambda i, k: (i, 0)),
        out_shape=jax.ShapeDtypeStruct((t, d), F32),
        scratch_shapes=[pltpu.VMEM((tm, d), F32)],
        compiler_params=_cparams(("arbitrary", "arbitrary")),
        name="out_proj",
    )(y, w, x, norm_g.reshape(1, d), mod)


def _conv3(x, prev_row, next_row, w, b, row0, seq_len):
    tm = x.shape[0]
    rows = lax.broadcasted_iota(jnp.int32, x.shape, 0)
    pos = (rows + row0) % seq_len
    prev = jnp.where(rows == 0, prev_row, pltpu.roll(x, 1, 0))
    nxt = jnp.where(rows == tm - 1, next_row, pltpu.roll(x, tm - 1, 0))
    prev = jnp.where(pos == 0, 0.0, prev)
    nxt = jnp.where(pos == seq_len - 1, 0.0, nxt)
    return w[0:1] * prev + w[1:2] * x + w[2:3] * nxt + b


def _halo_specs(tm, tc, nrows, colblk):
    nh = nrows // HALO
    r = tm // HALO
    return [pl.BlockSpec((tm, tc), lambda i, j: (i, colblk(j))),
            pl.BlockSpec((HALO, tc), lambda i, j: (jnp.maximum(i * r - 1, 0), colblk(j))),
            pl.BlockSpec((HALO, tc), lambda i, j: (jnp.minimum((i + 1) * r, nh - 1), colblk(j)))]


def _hy_inproj_kernel(seq_len, x_ref, xp_ref, xn_ref, g_ref, shift_ref, scale_ref,
                      w0_ref, w1_ref, w2_ref, wz_ref, cw, cb, vx_ref, g0_ref, u_scr):
    tm = x_ref.shape[0]
    ext = tm + 2 * HALO

    @pl.when(pl.program_id(1) == 0)
    def _():
        def modulated(x):
            r = lax.rsqrt(jnp.mean(x * x, axis=-1, keepdims=True) + EPS)
            return ((x * r * g_ref[...]) * (1.0 + scale_ref[0]) + shift_ref[0]).astype(BF16)

        u_scr[0:HALO] = modulated(xp_ref[...])
        u_scr[HALO:HALO + tm] = modulated(x_ref[...])
        u_scr[HALO + tm:ext] = modulated(xn_ref[...])

    u = u_scr[...]
    rows = lax.broadcasted_iota(jnp.int32, (tm, 1), 0)
    pos = (rows + pl.program_id(0) * tm) % seq_len
    first = pos == 0
    last = pos == seq_len - 1

    def conv(w_ref, k):
        r = jnp.dot(u, w_ref[...].astype(BF16), preferred_element_type=F32)
        prev = jnp.where(first, 0.0, pltpu.roll(r, 1, 0)[HALO:HALO + tm])
        nxt = jnp.where(last, 0.0, pltpu.roll(r, ext - 1, 0)[HALO:HALO + tm])
        w = cw[k]
        return w[0:1] * prev + w[1:2] * r[HALO:HALO + tm] + w[2:3] * nxt + cb[k]

    x1c = conv(w1_ref, 1)
    vc = conv(w2_ref, 2)
    vx_ref[...] = (vc * x1c).astype(vx_ref.dtype)
    x0c = conv(w0_ref, 0)
    z = jnp.dot(u_scr[HALO:HALO + tm], wz_ref[...].astype(BF16), preferred_element_type=F32)
    g0_ref[...] = (x0c * _silu(z)).astype(g0_ref.dtype)


def _hy_inproj(x, norm_g, mod, trunk, w, conv_w, conv_b, e):
    t, d = x.shape
    seq_len = trunk[1]
    tm = _cond_row_tile(trunk, 1024)
    tc = _tile(e, 256, 128)
    nb = e // tc
    cidx = _cond_index(trunk, tm)
    nh = t // HALO
    r = tm // HALO
    wspec = lambda part: pl.BlockSpec((d, tc), lambda i, j: (0, part * nb + j))
    cw = conv_w.reshape(3, 3, e).transpose(1, 0, 2)
    cb = conv_b.reshape(3, 1, e)
    return pl.pallas_call(
        functools.partial(_hy_inproj_kernel, seq_len),
        grid=(t // tm, nb),
        in_specs=[pl.BlockSpec((tm, d), lambda i, j: (i, 0)),
                  pl.BlockSpec((HALO, d), lambda i, j: (jnp.maximum(i * r - 1, 0), 0)),
                  pl.BlockSpec((HALO, d), lambda i, j: (jnp.minimum((i + 1) * r, nh - 1), 0)),
                  pl.BlockSpec((1, d), lambda i, j: (0, 0)),
                  pl.BlockSpec((1, 1, d), lambda i, j: (cidx(i), 0, 0)),
                  pl.BlockSpec((1, 1, d), lambda i, j: (cidx(i), 0, 1)),
                  wspec(0), wspec(1), wspec(2), wspec(3),
                  pl.BlockSpec((3, 3, tc), lambda i, j: (0, 0, j)),
                  pl.BlockSpec((3, 1, tc), lambda i, j: (0, 0, j))],
        out_specs=[pl.BlockSpec((tm, tc), lambda i, j: (i, j))] * 2,
        out_shape=[jax.ShapeDtypeStruct((t, e), BF16), jax.ShapeDtypeStruct((t, e), BF16)],
        scratch_shapes=[pltpu.VMEM((tm + 2 * HALO, d), BF16)],
        compiler_params=_cparams(("arbitrary", "arbitrary")),
        name="hy_in_proj_gate",
    )(x, x, x, norm_g.reshape(1, d), mod, mod, w, w, w, w, cw, cb)


def _filter_tables(seq_len):
    l = seq_len
    t = np.linspace(0.0, 1.0, l)
    w = 2.0 * math.pi * np.arange(l) / l
    f = np.linspace(1e-4, N_BANDS - 1, N_BANDS)
    z = np.concatenate([t[:, None], np.cos(f[None] * w[:, None]), -np.sin(f[None] * w[:, None])], axis=-1)
    pos = np.concatenate([np.arange(l), [0], np.arange(l - 1, 0, -1)])
    z2 = np.zeros((2 * l, 128), np.float32)
    z2[:, :z.shape[1]] = z[pos]
    aux = np.zeros((2 * l, 128), np.float32)
    aux[:, 0] = t[pos]
    aux[:, 1] = 1.0
    aux[l, 1] = 0.0
    return jnp.asarray(z2.T.copy()), jnp.asarray(aux)


def _filter_ffn_kernel(z_ref, w1, b1, w2, b2, w3, b3, fr, o_ref):
    f = fr[...]
    h = jnp.sin(f * (jnp.dot(w1[...], z_ref[...], precision=HI, preferred_element_type=F32) + b1[...]))
    h = jnp.sin(f * (jnp.dot(w2[...], h, precision=HI, preferred_element_type=F32) + b2[...]))
    h = jnp.sin(f * (jnp.dot(w3[...], h, precision=HI, preferred_element_type=F32) + b3[...]))
    o_ref[...] = h.astype(o_ref.dtype)


def _filter_expand_kernel(h_ref, aux_ref, w4, dec, o_ref):
    k = lax.dot_general(h_ref[...], w4[...].astype(BF16), (((0,), (0,)), ((), ())),
                        preferred_element_type=F32)
    t = aux_ref[:, 0:1]
    keep = aux_ref[:, 1:2]
    o_ref[...] = (k * jnp.exp(-t * jnp.abs(dec[0])) * keep).astype(o_ref.dtype)


def _filters(seq_len, w1, b1, w2, b2, w3, b3, w4, freq, decay):
    e = decay.shape[-1]
    fo = w2.shape[0]
    z2t, aux = _filter_tables(seq_len)
    w1t = jnp.zeros((fo, 128), F32).at[:, :w1.shape[0]].set(w1.T)
    tr = _tile(seq_len, 1024, 128)
    tc = _tile(e, 2048, 128)
    nb = e // tc
    nr = seq_len // tr
    small = lambda a: pl.BlockSpec(a.shape, lambda *_: (0,) * a.ndim)
    b1c, b2c, b3c, frc = (a.reshape(fo, 1) for a in (b1, b2, b3, freq))
    w2t, w3t = w2.T, w3.T
    h3 = pl.pallas_call(
        _filter_ffn_kernel,
        grid=(2 * nr,),
        in_specs=[pl.BlockSpec((128, tr), lambda i: (0, i)),
                  small(w1t), small(b1c), small(w2t), small(b2c), small(w3t), small(b3c), small(frc)],
        out_specs=pl.BlockSpec((fo, tr), lambda i: (0, i)),
        out_shape=jax.ShapeDtypeStruct((fo, 2 * seq_len), BF16),
        compiler_params=_cparams(("arbitrary",)),
        name="hy_filter_ffn",
    )(z2t, w1t, b1c, w2t, b2c, w3t, b3c, frc)
    return pl.pallas_call(
        _filter_expand_kernel,
        grid=(2 * nr, nb),
        in_specs=[pl.BlockSpec((fo, tr), lambda i, j: (0, i)),
                  pl.BlockSpec((tr, 128), lambda i, j: (i, 0)),
                  pl.BlockSpec((fo, tc), lambda i, j: (0, (i // nr) * nb + j)),
                  pl.BlockSpec((1, 1, tc), lambda i, j: (i // nr, 0, j))],
        out_specs=pl.BlockSpec((tr, tc), lambda i, j: (i, j)),
        out_shape=jax.ShapeDtypeStruct((2 * seq_len, e), BF16),
        compiler_params=_cparams(("arbitrary", "arbitrary")),
        name="hy_filter_expand",
    )(h3, aux, w4, decay.reshape(2, 1, e))


def _cmm_kernel(w_ref, x_ref, o_ref):
    o_ref[0] = jnp.dot(w_ref[...], x_ref[0], preferred_element_type=F32).astype(o_ref.dtype)


def _cmm(w, x, out_dtype):
    m, k = w.shape
    b, _, n = x.shape
    tn = _tile(n, 4096, 128)
    return pl.pallas_call(
        _cmm_kernel,
        grid=(b, n // tn),
        in_specs=[pl.BlockSpec((m, k), lambda bi, j: (0, 0)),
                  pl.BlockSpec((1, k, tn), lambda bi, j: (bi, 0, j))],
        out_specs=pl.BlockSpec((1, m, tn), lambda bi, j: (bi, 0, j)),
        out_shape=jax.ShapeDtypeStruct((b, m, n), out_dtype),
        compiler_params=_cparams(("arbitrary", "arbitrary")),
        name="const_lhs_matmul",
    )(w, x)


def _direct_mats(seq_len):
    l = seq_len
    n = 2 * l
    f = np.arange(l)[:, None]
    t = np.arange(n)[None, :]
    ang = 2.0 * math.pi * ((f * t) % n) / n
    fwd = np.concatenate([np.cos(ang), -np.sin(ang)], axis=0)
    fwd[l] = np.cos(math.pi * np.arange(n))
    tt = np.arange(l)[:, None]
    ff = np.arange(l)[None, :]
    ang2 = 2.0 * math.pi * ((tt * ff) % n) / n
    wgt = np.where(ff == 0, 1.0, 2.0) / n
    inv = np.concatenate([wgt * np.cos(ang2), -wgt * np.sin(ang2)], axis=1)
    inv[:, l] = np.cos(math.pi * np.arange(l)) / n
    return fwd, inv


def _direct_conv_kernel(seq_len, f_ref, v_ref, x_ref, k_ref, g0_ref, bias_ref, o_ref):
    l = seq_len
    x = x_ref[0]
    s = jnp.dot(f_ref[...], x, preferred_element_type=F32)
    sre, sim = s[:l], s[l:]
    kre, kim = k_ref[:l, :], k_ref[l:, :]
    row0 = lax.broadcasted_iota(jnp.int32, sre.shape, 0) == 0
    yre = sre * kre - jnp.where(row0, 0.0, sim * kim)
    yim = jnp.where(row0, sim * kim, sre * kim + sim * kre)
    y = jnp.concatenate([yre, yim], axis=0).astype(BF16)
    out = jnp.dot(v_ref[...], y, preferred_element_type=F32)
    out = out + x.astype(F32) * bias_ref[...]
    o_ref[0] = (out * g0_ref[0].astype(F32)).astype(o_ref.dtype)


def _longconv_direct(vx, g0, kfilt, bias, batch, seq_len):
    e = vx.shape[-1]
    l = seq_len
    fwd, inv = _direct_mats(l)
    f_full = jnp.asarray(fwd, BF16)
    f_data = jnp.asarray(fwd[:, :l], BF16)
    v_mat = jnp.asarray(inv, BF16)
    kspec = _cmm(f_full, kfilt.astype(BF16)[None], F32)[0]
    tc = _tile(e, 1024, 128)
    out = pl.pallas_call(
        functools.partial(_direct_conv_kernel, l),
        grid=(e // tc, batch),
        in_specs=[pl.BlockSpec((2 * l, l), lambda j, b: (0, 0)),
                  pl.BlockSpec((l, 2 * l), lambda j, b: (0, 0)),
                  pl.BlockSpec((1, l, tc), lambda j, b: (b, 0, j)),
                  pl.BlockSpec((2 * l, tc), lambda j, b: (0, j)),
                  pl.BlockSpec((1, l, tc), lambda j, b: (b, 0, j)),
                  pl.BlockSpec((1, tc), lambda j, b: (0, j))],
        out_specs=pl.BlockSpec((1, l, tc), lambda j, b: (b, 0, j)),
        out_shape=jax.ShapeDtypeStruct((batch, l, e), BF16),
        compiler_params=_cparams(("arbitrary", "arbitrary")),
        name="longconv_direct",
    )(f_data, v_mat, vx.reshape(batch, l, e), kspec, g0.reshape(batch, l, e), bias.reshape(1, e))
    return out.reshape(batch * l, e)


def _two_level_mats(seq_len):
    n = 2 * seq_len
    n2 = FFT_N2
    n1 = n // n2
    h2 = n2 // 2
    ns = 8 * ((h2 + 1 + 7) // 8)
    f2 = np.arange(h2 + 1)[:, None]
    t2 = np.arange(n2)[None, :]
    ang = 2.0 * math.pi * ((f2 * t2) % n2) / n2
    f1m = np.zeros((2 * ns, n2))
    f1m[0:2 * (h2 + 1):2] = np.cos(ang)
    f1m[1:2 * (h2 + 1):2] = -np.sin(ang)
    wgt = np.where((f2 == 0) | (f2 == h2), 1.0, 2.0) / n
    g1m = np.zeros((h2, 2 * ns))
    g1m[:, 0:2 * (h2 + 1):2] = (wgt * np.cos(ang[:, :h2])).T
    g1m[:, 1:2 * (h2 + 1):2] = -(wgt * np.sin(ang[:, :h2])).T
    t1 = np.arange(n1)[None, :]
    f1 = np.arange(n1)[:, None]
    rm = np.zeros((ns, 2 * n1, 2 * n1))
    pm = np.zeros((ns, 2 * n1, 2 * n1))
    for s in range(h2 + 1):
        a = 2.0 * math.pi * ((t1 * (n2 * f1 + s)) % n) / n
        mr, mi = np.cos(a), -np.sin(a)
        rm[s] = np.block([[mr, -mi], [mi, mr]])
        pr, pi = mr.T, -mi.T
        pm[s] = np.block([[pr, -pi], [pi, pr]])
    return n1, n2, ns, f1m, g1m, rm, pm


def _slab_spec_kernel(r_ref, a_ref, o_ref):
    n1 = a_ref.shape[3]
    a = jnp.concatenate([a_ref[0, 0, 0], a_ref[0, 0, 1]], axis=0)
    s = jnp.dot(r_ref[0], a, preferred_element_type=F32)
    o_ref[0, 0] = s[:n1].astype(o_ref.dtype)
    o_ref[0, 1] = s[n1:].astype(o_ref.dtype)


def _slab_conv_kernel(r_ref, p_ref, a_ref, k_ref, o_ref):
    n1 = a_ref.shape[3]
    kre, kim = k_ref[0, 0].astype(F32), k_ref[0, 1].astype(F32)
    for b in range(a_ref.shape[0]):
        a = jnp.concatenate([a_ref[b, 0, 0], a_ref[b, 0, 1]], axis=0)
        s = jnp.dot(r_ref[0], a, preferred_element_type=F32)
        sre, sim = s[:n1], s[n1:]
        y = jnp.concatenate([sre * kre - sim * kim, sre * kim + sim * kre], axis=0).astype(BF16)
        o = jnp.dot(p_ref[0], y, preferred_element_type=F32)
        o_ref[b, 0, 0] = o[:n1].astype(o_ref.dtype)
        o_ref[b, 0, 1] = o[n1:].astype(o_ref.dtype)


def _dft_l1_kernel(k_ref, x_ref, o_ref):
    x = x_ref[0].astype(F32)
    nt2, _, tc = x.shape
    halves = []
    for g in range(SUB_TILES):
        xg = x[:, SUBLANES * g:SUBLANES * (g + 1), :].reshape(nt2 * SUBLANES, tc).astype(BF16)
        a = jnp.dot(k_ref[...], xg, preferred_element_type=F32)
        halves.append(a.reshape(-1, SUBLANES, tc))
    o_ref[0] = jnp.concatenate(halves, axis=1).astype(o_ref.dtype)


def _dft_l1_inv_kernel(g_ref, a_ref, vx_ref, g0_ref, bias_ref, o_ref):
    a = a_ref[0].astype(F32)
    nr, _, tc = a.shape
    halves = []
    for g in range(SUB_TILES):
        ag = a[:, SUBLANES * g:SUBLANES * (g + 1), :].reshape(nr * SUBLANES, tc).astype(BF16)
        y = jnp.dot(g_ref[...], ag, preferred_element_type=F32)
        halves.append(y.reshape(-1, SUBLANES, tc))
    y = jnp.concatenate(halves, axis=1)
    y = y + vx_ref[0].astype(F32) * bias_ref[...]
    o_ref[0] = (y * g0_ref[0].astype(F32)).astype(o_ref.dtype)


def _dft_l1(kmat, x4, out_dtype):
    b, nt2, n1, e = x4.shape
    m = kmat.shape[0] // SUBLANES
    tc = _tile(e, 1024, 128)
    rows = SUBLANES * SUB_TILES
    return pl.pallas_call(
        _dft_l1_kernel,
        grid=(b, n1 // rows, e // tc),
        in_specs=[pl.BlockSpec(kmat.shape, lambda bi, i, j: (0, 0)),
                  pl.BlockSpec((1, nt2, rows, tc), lambda bi, i, j: (bi, 0, i, j))],
        out_specs=pl.BlockSpec((1, m, rows, tc), lambda bi, i, j: (bi, 0, i, j)),
        out_shape=jax.ShapeDtypeStruct((b, m, n1, e), out_dtype),
        compiler_params=_cparams(("arbitrary", "arbitrary", "arbitrary")),
        name="dft_level1",
    )(kmat, x4)


def _longconv_two_level(vx, g0, kfilt, bias, batch, seq_len):
    e = vx.shape[-1]
    l = seq_len
    n1, n2, ns, f1m, g1m, rm, pm = _two_level_mats(l)
    h2 = n2 // 2
    eye = np.eye(SUBLANES)
    k_full = jnp.asarray(np.kron(f1m, eye), BF16)
    k_data = jnp.asarray(np.kron(f1m[:, :h2], eye), BF16)
    g8 = jnp.asarray(np.kron(g1m, eye), BF16)
    rmat = jnp.asarray(rm, BF16)
    pmat = jnp.asarray(pm, BF16)
    tc = _tile(e, 4096, 128)
    nc = e // tc

    ka = _dft_l1(k_full, kfilt.reshape(1, n2, n1, e), BF16).reshape(1, ns, 2, n1, e)
    kspec = pl.pallas_call(
        _slab_spec_kernel,
        grid=(ns, nc),
        in_specs=[pl.BlockSpec((1, 2 * n1, 2 * n1), lambda s, j: (s, 0, 0)),
                  pl.BlockSpec((1, 1, 2, n1, tc), lambda s, j: (0, s, 0, 0, j))],
        out_specs=pl.BlockSpec((1, 2, n1, tc), lambda s, j: (s, 0, 0, j)),
        out_shape=jax.ShapeDtypeStruct((ns, 2, n1, e), BF16),
        compiler_params=_cparams(("arbitrary", "arbitrary")),
        name="filter_slab_dft",
    )(rmat, ka)

    vx4 = vx.reshape(batch, h2, n1, e)
    a = _dft_l1(k_data, vx4, BF16).reshape(batch, ns, 2, n1, e)
    a2 = pl.pallas_call(
        _slab_conv_kernel,
        grid=(ns, nc),
        in_specs=[pl.BlockSpec((1, 2 * n1, 2 * n1), lambda s, j: (s, 0, 0)),
                  pl.BlockSpec((1, 2 * n1, 2 * n1), lambda s, j: (s, 0, 0)),
                  pl.BlockSpec((batch, 1, 2, n1, tc), lambda s, j: (0, s, 0, 0, j)),
                  pl.BlockSpec((1, 2, n1, tc), lambda s, j: (s, 0, 0, j))],
        out_specs=pl.BlockSpec((batch, 1, 2, n1, tc), lambda s, j: (0, s, 0, 0, j)),
        out_shape=jax.ShapeDtypeStruct((batch, ns, 2, n1, e), BF16),
        compiler_params=_cparams(("arbitrary", "arbitrary")),
        name="slab_conv",
    )(rmat, pmat, a, kspec)
    a2 = a2.reshape(batch, 2 * ns, n1, e)
    rows = SUBLANES * SUB_TILES
    tci = _tile(e, 1024, 128)
    blk = lambda r: pl.BlockSpec((1, r, rows, tci), lambda bi, i, j: (bi, 0, i, j))
    out = pl.pallas_call(
        _dft_l1_inv_kernel,
        grid=(batch, n1 // rows, e // tci),
        in_specs=[pl.BlockSpec(g8.shape, lambda bi, i, j: (0, 0)),
                  blk(2 * ns), blk(h2), blk(h2),
                  pl.BlockSpec((1, 1, tci), lambda bi, i, j: (0, 0, j))],
        out_specs=blk(h2),
        out_shape=jax.ShapeDtypeStruct((batch, h2, n1, e), BF16),
        compiler_params=_cparams(("arbitrary", "arbitrary", "arbitrary")),
        name="dft_level1_inverse",
    )(g8, a2, vx4, g0.reshape(batch, h2, n1, e), bias.reshape(1, 1, e))
    return out.reshape(batch * l, e)


def _longconv(vx, g0, kfilt, bias, batch, seq_len):
    if seq_len <= DIRECT_CONV_MAX_L:
        return _longconv_direct(vx, g0, kfilt, bias, batch, seq_len)
    return _longconv_two_level(vx, g0, kfilt, bias, batch, seq_len)


def _ml_qkv_kernel(seq_len, n_heads, dk, x_ref, xp_ref, xn_ref, cw_ref, cb_ref, wq_ref, wk_ref, wv_ref,
                   wgq_ref, wgk_ref, wgv_ref, bg_ref, q_ref, k_ref, kt_ref, v_ref, xc_ref, g_ref, gt_ref, gacc):
    h = pl.program_id(1)
    tm = x_ref.shape[0]
    q_len = gt_ref.shape[2]
    x = x_ref[...].astype(F32)
    conv = _conv3(x, xp_ref[HALO - 1:HALO, :].astype(F32), xn_ref[0:1, :].astype(F32),
                  cw_ref[...], cb_ref[...], pl.program_id(0) * tm, seq_len)
    xc = _silu(conv)
    xc_ref[...] = xc.astype(xc_ref.dtype)
    xcb = xc.astype(BF16)
    q = jnp.dot(xcb, wq_ref[0], preferred_element_type=F32).astype(BF16)
    kf = jnp.dot(xcb, wk_ref[0], preferred_element_type=F32) * (dk ** -0.5)
    k = kf.astype(BF16)
    v = jnp.dot(x.astype(BF16), wv_ref[0], preferred_element_type=F32).astype(BF16)
    q_ref[...] = q
    k_ref[...] = k
    v_ref[...] = v
    for c in range(tm // q_len):
        kt_ref[c] = kf[c * q_len:(c + 1) * q_len, :].T.astype(BF16)
    part = (jnp.dot(q, wgq_ref[0], preferred_element_type=F32)
            + jnp.dot(k, wgk_ref[0], preferred_element_type=F32)
            + jnp.dot(v, wgv_ref[0], preferred_element_type=F32))

    @pl.when(h == 0)
    def _():
        gacc[...] = part + bg_ref[...]

    @pl.when(h > 0)
    def _():
        gacc[...] += part

    @pl.when(h == n_heads - 1)
    def _():
        g = gacc[...]
        col = lax.broadcasted_iota(jnp.int32, g.shape, 1)
        is_forget = (col % (2 * n_heads)) >= n_heads
        g = jnp.where(is_forget, jax.nn.log_sigmoid(g), g)
        col_q = lax.broadcasted_iota(jnp.int32, (q_len, g.shape[1]), 1)
        forget_q = (col_q % (2 * n_heads)) >= n_heads
        bwd_q = (col_q // (2 * n_heads)) == 1
        r_i = lax.broadcasted_iota(jnp.int32, (q_len, q_len), 0)
        c_i = lax.broadcasted_iota(jnp.int32, (q_len, q_len), 1)
        tril = (c_i <= r_i).astype(F32)
        for c in range(tm // q_len):
            gc = g[c * q_len:(c + 1) * q_len, :]
            pre = jnp.dot(tril, gc, precision=HI, preferred_element_type=F32)
            tot = jnp.sum(gc, axis=0, keepdims=True)
            suf = tot - pre + gc
            gc = jnp.where(forget_q, jnp.where(bwd_q, suf, pre), gc)
            g_ref[c * q_len:(c + 1) * q_len, :] = gc
            gt_ref[c] = gc.T


def _ml_qkv(p, conv_w, conv_b, wq, wk, wv, w_gate, b_gate, batch, seq_len, e):
    t = p.shape[0]
    n_heads, dh, dk = wq.shape
    dv = wv.shape[2]
    q_len = min(SCAN_CHUNK, seq_len)
    tm = _row_tile(batch, seq_len, 1024, q_len)
    ng = w_gate.shape[1]
    gpad = 128
    wgq = jnp.zeros((n_heads, dk, gpad), BF16).at[:, :, :ng].set(
        w_gate[:n_heads * dk].reshape(n_heads, dk, ng).astype(BF16))
    wgk = jnp.zeros((n_heads, dk, gpad), BF16).at[:, :, :ng].set(
        w_gate[n_heads * dk:2 * n_heads * dk].reshape(n_heads, dk, ng).astype(BF16))
    wgv = jnp.zeros((n_heads, dv, gpad), BF16).at[:, :, :ng].set(
        w_gate[2 * n_heads * dk:].reshape(n_heads, dv, ng).astype(BF16))
    bg = jnp.zeros((1, gpad), F32).at[0, :ng].set(b_gate)
    specs = _halo_specs(tm, dh, t, lambda j: j)
    specs += [pl.BlockSpec((3, dh), lambda i, j: (0, j)),
              pl.BlockSpec((1, dh), lambda i, j: (0, j)),
              pl.BlockSpec((1, dh, dk), lambda i, j: (j, 0, 0)),
              pl.BlockSpec((1, dh, dk), lambda i, j: (j, 0, 0)),
              pl.BlockSpec((1, dh, dv), lambda i, j: (j, 0, 0)),
              pl.BlockSpec((1, dk, gpad), lambda i, j: (j, 0, 0)),
              pl.BlockSpec((1, dk, gpad), lambda i, j: (j, 0, 0)),
              pl.BlockSpec((1, dv, gpad), lambda i, j: (j, 0, 0)),
              pl.BlockSpec((1, gpad), lambda i, j: (0, 0))]
    return pl.pallas_call(
        functools.partial(_ml_qkv_kernel, seq_len, n_heads, dk),
        grid=(t // tm, n_heads),
        in_specs=specs,
        out_specs=[pl.BlockSpec((tm, dk), lambda i, j: (i, j)),
                   pl.BlockSpec((tm, dk), lambda i, j: (i, j)),
                   pl.BlockSpec((tm // q_len, dk, q_len), lambda i, j: (i, j, 0)),
                   pl.BlockSpec((tm, dv), lambda i, j: (i, j)),
                   pl.BlockSpec((tm, dh), lambda i, j: (i, j)),
                   pl.BlockSpec((tm, gpad), lambda i, j: (i, 0)),
                   pl.BlockSpec((tm // q_len, gpad, q_len), lambda i, j: (i, 0, 0))],
        out_shape=[jax.ShapeDtypeStruct((t, n_heads * dk), BF16),
                   jax.ShapeDtypeStruct((t, n_heads * dk), BF16),
                   jax.ShapeDtypeStruct((t // q_len, n_heads * dk, q_len), BF16),
                   jax.ShapeDtypeStruct((t, n_heads * dv), BF16),
                   jax.ShapeDtypeStruct((t, e), BF16),
                   jax.ShapeDtypeStruct((t, gpad), F32),
                   jax.ShapeDtypeStruct((t // q_len, gpad, q_len), F32)],
        scratch_shapes=[pltpu.VMEM((tm, gpad), F32)],
        compiler_params=_cparams(("arbitrary", "arbitrary")),
        name="ml_qkv_gates",
    )(p, p, p, conv_w, conv_b.reshape(1, e), wq.astype(BF16), wk.astype(BF16), wv.astype(BF16),
      wgq, wgk, wgv, bg)


def _ml_scan_kernel(n_heads, has_init, *refs):
    if has_init:
        (q_ref, k_ref, kt_ref, v_ref, g_ref, gt_ref, c0_ref, n0_ref, m0_ref,
         h_ref, cout_ref, nout_ref, mout_ref, c_scr, n_scr, m_scr) = refs
    else:
        (q_ref, k_ref, kt_ref, v_ref, g_ref, gt_ref,
         h_ref, cout_ref, nout_ref, mout_ref, c_scr, n_scr, m_scr) = refs
    head = pl.program_id(1)
    seq_len = q_ref.shape[0]
    q_len = gt_ref.shape[2]
    n_chunks = seq_len // q_len

    if has_init:
        for d in range(2):
            c_scr[d] = c0_ref[0, 0, d, 0]
            n_scr[d] = n0_ref[0, 0, d]
            m_scr[d] = m0_ref[0, 0, d]
    h_ref[...] = jnp.zeros_like(h_ref)

    row_i = lax.broadcasted_iota(jnp.int32, (q_len, q_len), 0)
    col_i = lax.broadcasted_iota(jnp.int32, (q_len, q_len), 1)
    lane_g = lax.broadcasted_iota(jnp.int32, (q_len, g_ref.shape[1]), 1)

    def direction(d, c, from_zero=False):
        icol = d * 2 * n_heads + head
        fcol = icol + n_heads
        start = c * q_len if isinstance(c, int) else pl.multiple_of(c * q_len, q_len)
        rows = pl.ds(start, q_len)
        g = g_ref[rows, :]
        i_col = jnp.sum(jnp.where(lane_g == icol, g, 0.0), axis=1, keepdims=True)
        b_col = jnp.sum(jnp.where(lane_g == fcol, g, 0.0), axis=1, keepdims=True)
        i_row = gt_ref[c, pl.ds(icol, 1), :]
        b_row = gt_ref[c, pl.ds(fcol, 1), :]
        mask = (col_i <= row_i) if d == 0 else (col_i >= row_i)
        last = q_len - 1 if d == 0 else 0
        g_tot = jnp.sum(jnp.where(col_i[0:1, :] == last, b_row, 0.0), axis=1, keepdims=True)
        m_prev = 0.0 if from_zero else m_scr[d][:, 0:1]
        if not from_zero:
            n_prev = n_scr[d]
            c_prev = c_scr[d]

        qc = q_ref[rows, :]
        kc = k_ref[rows, :]
        vc = v_ref[rows, :]
        dmat = jnp.where(mask, b_col - b_row + i_row, -jnp.inf)
        inter = b_col + m_prev
        m_t = jnp.maximum(inter, jnp.max(dmat, axis=1, keepdims=True))
        qk = lax.dot_general(qc, kc, (((1,), (1,)), ((), ())), preferred_element_type=F32)
        s = qk * jnp.exp(dmat - m_t)
        if from_zero:
            num = jnp.dot(s.astype(BF16), vc, preferred_element_type=F32)
            qn = jnp.sum(s, axis=1, keepdims=True)
        else:
            w_inter = jnp.exp(inter - m_t)
            num = (w_inter * jnp.dot(qc, c_prev.astype(BF16), preferred_element_type=F32)
                   + jnp.dot(s.astype(BF16), vc, preferred_element_type=F32))
            qn = (w_inter * jnp.sum(qc.astype(F32) * n_prev, axis=1, keepdims=True)
                  + jnp.sum(s, axis=1, keepdims=True))
        den = jnp.maximum(jnp.abs(qn), jnp.exp(-m_t))
        h_ref[rows, :] += num / den

        a_col = g_tot - b_col + i_col
        a_row = g_tot - b_row + i_row
        m_new = jnp.maximum(g_tot + m_prev, jnp.max(a_row, axis=1, keepdims=True))
        kwt = (kt_ref[c].astype(F32) * jnp.exp(a_row - m_new)).astype(BF16)
        kv = jnp.dot(kwt, vc, preferred_element_type=F32)
        ksum = jnp.sum(kc.astype(F32) * jnp.exp(a_col - m_new), axis=0, keepdims=True)
        if from_zero:
            c_scr[d] = kv
            n_scr[d] = ksum
        else:
            dec = jnp.exp(g_tot + m_prev - m_new)
            c_scr[d] = dec * c_prev + kv
            n_scr[d] = dec * n_prev + ksum
        m_scr[d] = jnp.broadcast_to(m_new, m_scr.shape[1:])

    def body(j, carry):
        direction(0, j)
        direction(1, n_chunks - 1 - j)
        return carry

    first = 0
    if not has_init:
        direction(0, 0, from_zero=True)
        direction(1, n_chunks - 1, from_zero=True)
        first = 1
    lax.fori_loop(first, n_chunks, body, 0)
    for d in range(2):
        cout_ref[0, 0, d, 0] = c_scr[d]
        nout_ref[0, 0, d] = n_scr[d]
        mout_ref[0, 0, d] = m_scr[d]


def _ml_scan(q, k, kt, v, g, gt, batch, seq_len, n_heads, init=None):
    dk = q.shape[1] // n_heads
    dv = v.shape[1] // n_heads
    q_len = gt.shape[2]
    gpad = g.shape[1]
    nq = seq_len // q_len
    in_specs = [pl.BlockSpec((seq_len, dk), lambda b, h: (b, h)),
                pl.BlockSpec((seq_len, dk), lambda b, h: (b, h)),
                pl.BlockSpec((nq, dk, q_len), lambda b, h: (b, h, 0)),
                pl.BlockSpec((seq_len, dv), lambda b, h: (b, h)),
                pl.BlockSpec((seq_len, gpad), lambda b, h: (b, 0)),
                pl.BlockSpec((nq, gpad, q_len), lambda b, h: (b, 0, 0))]
    args = [q, k, kt, v, g, gt]
    if init is not None:
        c0, n0, m0 = init
        in_specs += [pl.BlockSpec((1, 1, 2, 1, dk, dv), lambda b, h: (b, 0, 0, h, 0, 0)),
                     pl.BlockSpec((1, 1, 2, 1, dk), lambda b, h: (b, h, 0, 0, 0)),
                     pl.BlockSpec((1, 1, 2, 1, 128), lambda b, h: (b, h, 0, 0, 0))]
        args += [c0, n0, m0]
    out = pl.pallas_call(
        functools.partial(_ml_scan_kernel, n_heads, init is not None),
        grid=(batch, n_heads),
        in_specs=in_specs,
        out_specs=[pl.BlockSpec((seq_len, dv), lambda b, h: (b, h)),
                   pl.BlockSpec((1, 1, 2, 1, dk, dv), lambda b, h: (b, 0, 0, h, 0, 0)),
                   pl.BlockSpec((1, 1, 2, 1, dk), lambda b, h: (b, h, 0, 0, 0)),
                   pl.BlockSpec((1, 1, 2, 1, 128), lambda b, h: (b, h, 0, 0, 0))],
        out_shape=[jax.ShapeDtypeStruct((batch * seq_len, n_heads * dv), F32),
                   jax.ShapeDtypeStruct((batch, 1, 2, n_heads, dk, dv), F32),
                   jax.ShapeDtypeStruct((batch, n_heads, 2, 1, dk), F32),
                   jax.ShapeDtypeStruct((batch, n_heads, 2, 1, 128), F32)],
        scratch_shapes=[pltpu.VMEM((2, dk, dv), F32), pltpu.VMEM((2, 1, dk), F32),
                        pltpu.VMEM((2, 1, 128), F32)],
        compiler_params=_cparams(("arbitrary", "arbitrary")),
        name="ml_scan",
    )(*args)
    return out


def _ml_out_kernel(h_ref, o_ref_in, xc_ref, z_ref, nw_ref, sk_ref, w_ref, x_ref, g_ref, gate_ref, out_ref, acc):
    k = pl.program_id(1)

    @pl.when(k == 0)
    def _():
        acc[...] = jnp.zeros_like(acc)

    dv = w_ref.shape[0] // HEADS_PER_STEP
    ys = []
    for hd in range(HEADS_PER_STEP):
        cols = slice(hd * dv, (hd + 1) * dv)
        hh = jax.nn.sigmoid(o_ref_in[:, cols].astype(F32)) * h_ref[:, cols]
        mu = jnp.mean(hh, axis=-1, keepdims=True)
        var = jnp.mean(jnp.square(hh - mu), axis=-1, keepdims=True)
        hn = (hh - mu) * lax.rsqrt(var + EPS) * nw_ref[:, cols]
        hn = hn + sk_ref[:, cols] * xc_ref[:, cols].astype(F32)
        ys.append((hn * _silu(z_ref[:, cols].astype(F32))).astype(BF16))
    y = jnp.concatenate(ys, axis=1)
    acc[...] += jnp.dot(y, w_ref[...].astype(BF16), preferred_element_type=F32)

    @pl.when(k == pl.num_programs(1) - 1)
    def _():
        _outproj_epilogue(acc[...], x_ref, g_ref, gate_ref, out_ref)


def _ml_out(h, p, xc, norm_w, skip, w, x, norm_g, mod, trunk, n_heads):
    t, e = h.shape
    d = w.shape[1]
    assert n_heads % HEADS_PER_STEP == 0
    nk = n_heads // HEADS_PER_STEP
    dv = HEADS_PER_STEP * (e // n_heads)
    tm = _cond_row_tile(trunk, 512)
    cidx = _cond_index(trunk, tm)
    return pl.pallas_call(
        _ml_out_kernel,
        grid=(t // tm, nk),
        in_specs=[pl.BlockSpec((tm, dv), lambda i, k: (i, k)),
                  pl.BlockSpec((tm, dv), lambda i, k: (i, 2 * nk + k)),
                  pl.BlockSpec((tm, dv), lambda i, k: (i, k)),
                  pl.BlockSpec((tm, dv), lambda i, k: (i, nk + k)),
                  pl.BlockSpec((1, dv), lambda i, k: (0, k)),
                  pl.BlockSpec((1, dv), lambda i, k: (0, k)),
                  pl.BlockSpec((dv, d), lambda i, k: (k, 0)),
                  pl.BlockSpec((tm, d), lambda i, k: (i, 0)),
                  pl.BlockSpec((1, d), lambda i, k: (0, 0)),
                  pl.BlockSpec((1, 1, d), lambda i, k: (cidx(i), 0, 2))],
        out_specs=pl.BlockSpec((tm, d), lambda i, k: (i, 0)),
        out_shape=jax.ShapeDtypeStruct((t, d), F32),
        scratch_shapes=[pltpu.VMEM((tm, d), F32)],
        compiler_params=_cparams(("arbitrary", "arbitrary")),
        name="ml_out_proj",
    )(h, p, xc, p, norm_w.reshape(1, e), skip.reshape(1, e), w, x, norm_g.reshape(1, d), mod)


def _hyena_layer(x, mod, trunk, norm_pre, norm_post, hw):
    (w_in, conv_w, conv_b, w1, b1, w2, b2, w3, b3, w4, freq, decay, bias, w_out) = hw
    batch, seq_len = trunk[0], trunk[1]
    e = w_out.shape[0]
    vx, g0 = _hy_inproj(x, norm_pre, mod, trunk, w_in, conv_w, conv_b, e)
    kfilt = _filters(seq_len, w1, b1, w2, b2, w3, b3, w4, freq, decay)
    y = _longconv(vx, g0, kfilt, bias, batch, seq_len)
    return _outproj(y, w_out.astype(BF16), x, norm_post, mod, trunk)


def _mlstm_layer(x, mod, trunk, norm_pre, norm_post, mw, init):
    (w_in, conv_w, conv_b, wq, wk, wv, w_gate, b_gate, norm_w, skip, w_out) = mw
    batch, seq_len = trunk[0], trunk[1]
    e = w_out.shape[0]
    n_heads = wq.shape[0]
    p = _inproj(x, norm_pre, mod, trunk, w_in, BF16)
    q, k, kt, v, xc, g, gt = _ml_qkv(p, conv_w, conv_b, wq, wk, wv, w_gate, b_gate, batch, seq_len, e)
    h, c_new, n_new, m_new = _ml_scan(q, k, kt, v, g, gt, batch, seq_len, n_heads, init)
    x_new = _ml_out(h, p, xc, norm_w, skip, w_out.astype(BF16), x, norm_post, mod, trunk, n_heads)
    return x_new, c_new, n_new, m_new


def kernel(x_prompt, x_sample, state_C, state_n, state_m, c, c_ctx, norm_pre, norm_post, ada_w, ada_b,
           hy_w_in, hy_conv_w, hy_conv_b, hy_ffn_w1, hy_ffn_b1, hy_ffn_w2, hy_ffn_b2, hy_ffn_w3, hy_ffn_b3,
           hy_ffn_w4, hy_sin_freq, hy_decay, hy_bias, hy_w_out, ml_w_in, ml_conv_w, ml_conv_b, ml_wq, ml_wk,
           ml_wv, ml_w_gate, ml_b_gate, ml_norm, ml_skip, ml_w_out):
    depth = norm_pre.shape[0]
    bp, lp, d = x_prompt.shape
    bs, ls, _ = x_sample.shape
    n_heads = ml_wq.shape[1]

    nrow = 8 * ((1 + bs + 7) // 8)
    conds = jnp.zeros((nrow, d), F32).at[0].set(c_ctx).at[1:1 + bs].set(c)
    mods = _ada(conds, ada_w, ada_b)

    def run_trunk(x3, cond0, per_seq, init_states):
        batch, seq_len, _ = x3.shape
        trunk = (batch, seq_len, cond0, per_seq)
        x = x3.reshape(batch * seq_len, d)
        new_c, new_n, new_m = [], [], []
        for i in range(depth):
            mod = mods[i].reshape(nrow, 1, 3 * d)
            j = i // 2
            if i % 2 == 0:
                hw = (hy_w_in[j], hy_conv_w[j], hy_conv_b[j], hy_ffn_w1[j], hy_ffn_b1[j], hy_ffn_w2[j],
                      hy_ffn_b2[j], hy_ffn_w3[j], hy_ffn_b3[j], hy_ffn_w4[j], hy_sin_freq[j], hy_decay[j],
                      hy_bias[j], hy_w_out[j])
                x = _hyena_layer(x, mod, trunk, norm_pre[i], norm_post[i], hw)
            else:
                mw = (ml_w_in[j], ml_conv_w[j], ml_conv_b[j], ml_wq[j], ml_wk[j], ml_wv[j], ml_w_gate[j],
                      ml_b_gate[j], ml_norm[j], ml_skip[j], ml_w_out[j])
                init = None if init_states is None else init_states(j)
                x, cj, nj, mj = _mlstm_layer(x, mod, trunk, norm_pre[i], norm_post[i], mw, init)
                new_c.append(cj)
                new_n.append(nj)
                new_m.append(mj)
        return x.reshape(batch, seq_len, d), new_c, new_n, new_m

    y_prompt, cs, ns, ms = run_trunk(x_prompt, 0, False, None)
    dk = ml_wq.shape[3]
    new_state_c = jnp.concatenate(cs, axis=1)
    new_state_n = jnp.stack([jnp.swapaxes(n[:, :, :, 0, :], 1, 2) for n in ns], axis=1)
    new_state_m = jnp.stack([jnp.swapaxes(m[:, :, :, 0, 0], 1, 2) for m in ms], axis=1)

    def lat_init(j):
        c0 = state_C[:, j:j + 1]
        n0 = jnp.swapaxes(state_n[:, j], 1, 2)[:, :, :, None, :]
        m0 = jnp.broadcast_to(jnp.swapaxes(state_m[:, j], 1, 2)[:, :, :, None, None],
                              (bs, n_heads, 2, 1, 128))
        return c0, n0, m0

    y_sample, _, _, _ = run_trunk(x_sample, 1, True, lat_init)
    return (y_prompt, y_sample, new_state_c, new_state_n, new_state_m)
```

```python
import functools
import math

import numpy as np
import jax
import jax.numpy as jnp
from jax import lax
from jax.experimental import pallas as pl
from jax.experimental.pallas import tpu as pltpu

F32 = jnp.float32
BF16 = jnp.bfloat16
EPS = 1e-6
N_BANDS = 16
SUBLANES = 8
SUB_TILES = 2
HALO = 16
SCAN_CHUNK = 256
HEADS_PER_STEP = 2
FFT_N2 = 128
DIRECT_CONV_MAX_L = 512
VMEM_LIMIT = 56 * 1024 * 1024
HI = lax.Precision.HIGHEST


def _cparams(sem):
    return pltpu.CompilerParams(dimension_semantics=sem, vmem_limit_bytes=VMEM_LIMIT)


def _tile(n, pref, mult):
    if n <= pref:
        return n
    t = (pref // mult) * mult
    while t >= mult:
        if n % t == 0:
            return t
        t -= mult
    return n


def _row_tile(batch, seq_len, pref, mult):
    if seq_len >= pref:
        return _tile(seq_len, pref, mult)
    k = max(1, pref // seq_len)
    while batch % k:
        k -= 1
    return seq_len * k


def _silu(x):
    return x * jax.nn.sigmoid(x)


def _ada_kernel(c_ref, w_ref, b_ref, o_ref):
    a = _silu(c_ref[...]).astype(BF16)
    o_ref[0] = jnp.dot(a, w_ref[0].astype(BF16), preferred_element_type=F32) + b_ref[0]


def _ada(conds, ada_w, ada_b):
    depth, d, n = ada_w.shape
    rows = conds.shape[0]
    tn = _tile(n, 512, 128)
    return pl.pallas_call(
        _ada_kernel,
        grid=(depth, n // tn),
        in_specs=[pl.BlockSpec((rows, d), lambda l, j: (0, 0)),
                  pl.BlockSpec((1, d, tn), lambda l, j: (l, 0, j)),
                  pl.BlockSpec((1, 1, tn), lambda l, j: (l, 0, j))],
        out_specs=pl.BlockSpec((1, rows, tn), lambda l, j: (l, 0, j)),
        out_shape=jax.ShapeDtypeStruct((depth, rows, n), F32),
        compiler_params=_cparams(("arbitrary", "arbitrary")),
        name="ada_mod",
    )(conds, ada_w, ada_b.reshape(depth, 1, n))


def _inproj_kernel(x_ref, g_ref, shift_ref, scale_ref, w_ref, o_ref, u_scr):
    @pl.when(pl.program_id(1) == 0)
    def _():
        x = x_ref[...]
        r = lax.rsqrt(jnp.mean(x * x, axis=-1, keepdims=True) + EPS)
        u = x * r * g_ref[...]
        u_scr[...] = (u * (1.0 + scale_ref[0]) + shift_ref[0]).astype(BF16)

    o_ref[...] = jnp.dot(u_scr[...], w_ref[...].astype(BF16), preferred_element_type=F32).astype(o_ref.dtype)


def _cond_index(trunk, tm):
    batch, seq_len, cond0, per_seq = trunk
    if per_seq:
        return lambda i: cond0 + (i * tm) // seq_len
    return lambda i: cond0


def _cond_row_tile(trunk, pref):
    batch, seq_len, _, per_seq = trunk
    if per_seq:
        return _tile(seq_len, pref, 16)
    return _tile(batch * seq_len, pref, 16)


def _inproj(x, norm_g, mod, trunk, w, out_dtype):
    t, d = x.shape
    n = w.shape[1]
    tm = _cond_row_tile(trunk, 1024)
    tn = _tile(n, 1024, 128)
    cidx = _cond_index(trunk, tm)
    return pl.pallas_call(
        _inproj_kernel,
        grid=(t // tm, n // tn),
        in_specs=[pl.BlockSpec((tm, d), lambda i, j: (i, 0)),
                  pl.BlockSpec((1, d), lambda i, j: (0, 0)),
                  pl.BlockSpec((1, 1, d), lambda i, j: (cidx(i), 0, 0)),
                  pl.BlockSpec((1, 1, d), lambda i, j: (cidx(i), 0, 1)),
                  pl.BlockSpec((d, tn), lambda i, j: (0, j))],
        out_specs=pl.BlockSpec((tm, tn), lambda i, j: (i, j)),
        out_shape=jax.ShapeDtypeStruct((t, n), out_dtype),
        scratch_shapes=[pltpu.VMEM((tm, d), BF16)],
        compiler_params=_cparams(("arbitrary", "arbitrary")),
        name="in_proj",
    )(x, norm_g.reshape(1, d), mod, mod, w)


def _outproj_epilogue(acc, x_ref, g_ref, gate_ref, o_ref):
    y = acc
    r = lax.rsqrt(jnp.mean(y * y, axis=-1, keepdims=True) + EPS)
    o_ref[...] = x_ref[...] + gate_ref[0] * (y * r * g_ref[...])


def _outproj_kernel(y_ref, w_ref, x_ref, g_ref, gate_ref, o_ref, acc):
    k = pl.program_id(1)

    @pl.when(k == 0)
    def _():
        acc[...] = jnp.zeros_like(acc)

    acc[...] += jnp.dot(y_ref[...], w_ref[...].astype(BF16), preferred_element_type=F32)

    @pl.when(k == pl.num_programs(1) - 1)
    def _():
        _outproj_epilogue(acc[...], x_ref, g_ref, gate_ref, o_ref)


def _outproj(y, w, x, norm_g, mod, trunk):
    t, e = y.shape
    d = w.shape[1]
    tm = _cond_row_tile(trunk, 512)
    tk = _tile(e, 2048, 128)
    cidx = _cond_index(trunk, tm)
    return pl.pallas_call(
        _outproj_kernel,
        grid=(t // tm, e // tk),
        in_specs=[pl.BlockSpec((tm, tk), lambda i, k: (i, k)),
                  pl.BlockSpec((tk, d), lambda i, k: (k, 0)),
                  pl.BlockSpec((tm, d), lambda i, k: (i, 0)),
                  pl.BlockSpec((1, d), lambda i, k: (0, 0)),
                  pl.BlockSpec((1, 1, d), lambda i, k: (cidx(i), 0, 2))],
        out_specs=pl.BlockSpec((tm, d), lambda i, k: (i, 0)),
        out_shape=jax.ShapeDtypeStruct((t, d), F32),
        scratch_shapes=[pltpu.VMEM((tm, d), F32)],
        compiler_params=_cparams(("arbitrary", "arbitrary")),
        name="out_proj",
    )(y, w, x, norm_g.reshape(1, d), mod)


def _conv3(x, prev_row, next_row, w, b, row0, seq_len):
    tm = x.shape[0]
    rows = lax.broadcasted_iota(jnp.int32, (tm, 1), 0)
    pos = (rows + row0) % seq_len
    prev = jnp.where(rows == 0, prev_row, pltpu.roll(x, 1, 0))
    nxt = jnp.where(rows == tm - 1, next_row, pltpu.roll(x, tm - 1, 0))
    prev = jnp.where(pos == 0, 0.0, prev)
    nxt = jnp.where(pos == seq_len - 1, 0.0, nxt)
    return w[0:1] * prev + w[1:2] * x + w[2:3] * nxt + b


def _halo_specs(tm, tc, nrows, colblk):
    nh = nrows // HALO
    r = tm // HALO
    return [pl.BlockSpec((tm, tc), lambda i, j: (i, colblk(j))),
            pl.BlockSpec((HALO, tc), lambda i, j: (jnp.maximum(i * r - 1, 0), colblk(j))),
            pl.BlockSpec((HALO, tc), lambda i, j: (jnp.minimum((i + 1) * r, nh - 1), colblk(j)))]


def _hy_inproj_kernel(seq_len, x_ref, xp_ref, xn_ref, g_ref, shift_ref, scale_ref,
                      w0_ref, w1_ref, w2_ref, wz_ref, cw, cb, vx_ref, g0_ref, u_scr):
    tm = x_ref.shape[0]
    ext = tm + 2 * HALO

    @pl.when(pl.program_id(1) == 0)
    def _():
        def modulated(x):
            r = lax.rsqrt(jnp.mean(x * x, axis=-1, keepdims=True) + EPS)
            return ((x * r * g_ref[...]) * (1.0 + scale_ref[0]) + shift_ref[0]).astype(BF16)

        u_scr[0:HALO] = modulated(xp_ref[...])
        u_scr[HALO:HALO + tm] = modulated(x_ref[...])
        u_scr[HALO + tm:ext] = modulated(xn_ref[...])

    u = u_scr[...]
    rows = lax.broadcasted_iota(jnp.int32, (tm, 1), 0)
    pos = (rows + pl.program_id(0) * tm) % seq_len
    first = pos == 0
    last = pos == seq_len - 1

    def conv(w_ref, k):
        r = jnp.dot(u, w_ref[...].astype(BF16), preferred_element_type=F32)
        prev = jnp.where(first, 0.0, pltpu.roll(r, 1, 0)[HALO:HALO + tm])
        nxt = jnp.where(last, 0.0, pltpu.roll(r, ext - 1, 0)[HALO:HALO + tm])
        w = cw[k]
        return w[0:1] * prev + w[1:2] * r[HALO:HALO + tm] + w[2:3] * nxt + cb[k]

    x1c = conv(w1_ref, 1)
    vc = conv(w2_ref, 2)
    vx_ref[...] = (vc * x1c).astype(vx_ref.dtype)
    x0c = conv(w0_ref, 0)
    z = jnp.dot(u_scr[HALO:HALO + tm], wz_ref[...].astype(BF16), preferred_element_type=F32)
    g0_ref[...] = (x0c * _silu(z)).astype(g0_ref.dtype)


def _hy_inproj(x, norm_g, mod, trunk, w, conv_w, conv_b, e):
    t, d = x.shape
    seq_len = trunk[1]
    tm = _cond_row_tile(trunk, 1024)
    tc = _tile(e, 256, 128)
    nb = e // tc
    cidx = _cond_index(trunk, tm)
    nh = t // HALO
    r = tm // HALO
    wspec = lambda part: pl.BlockSpec((d, tc), lambda i, j: (0, part * nb + j))
    cw = conv_w.reshape(3, 3, e).transpose(1, 0, 2)
    cb = conv_b.reshape(3, 1, e)
    return pl.pallas_call(
        functools.partial(_hy_inproj_kernel, seq_len),
        grid=(t // tm, nb),
        in_specs=[pl.BlockSpec((tm, d), lambda i, j: (i, 0)),
                  pl.BlockSpec((HALO, d), lambda i, j: (jnp.maximum(i * r - 1, 0), 0)),
                  pl.BlockSpec((HALO, d), lambda i, j: (jnp.minimum((i + 1) * r, nh - 1), 0)),
                  pl.BlockSpec((1, d), lambda i, j: (0, 0)),
                  pl.BlockSpec((1, 1, d), lambda i, j: (cidx(i), 0, 0)),
                  pl.BlockSpec((1, 1, d), lambda i, j: (cidx(i), 0, 1)),
                  wspec(0), wspec(1), wspec(2), wspec(3),
                  pl.BlockSpec((3, 3, tc), lambda i, j: (0, 0, j)),
                  pl.BlockSpec((3, 1, tc), lambda i, j: (0, 0, j))],
        out_specs=[pl.BlockSpec((tm, tc), lambda i, j: (i, j))] * 2,
        out_shape=[jax.ShapeDtypeStruct((t, e), BF16), jax.ShapeDtypeStruct((t, e), BF16)],
        scratch_shapes=[pltpu.VMEM((tm + 2 * HALO, d), BF16)],
        compiler_params=_cparams(("arbitrary", "arbitrary")),
        name="hy_in_proj_gate",
    )(x, x, x, norm_g.reshape(1, d), mod, mod, w, w, w, w, cw, cb)


def _filter_tables(seq_len):
    l = seq_len
    t = np.linspace(0.0, 1.0, l)
    w = 2.0 * math.pi * np.arange(l) / l
    f = np.linspace(1e-4, N_BANDS - 1, N_BANDS)
    z = np.concatenate([t[:, None], np.cos(f[None] * w[:, None]), -np.sin(f[None] * w[:, None])], axis=-1)
    pos = np.concatenate([np.arange(l), [0], np.arange(l - 1, 0, -1)])
    z2 = np.zeros((2 * l, 128), np.float32)
    z2[:, :z.shape[1]] = z[pos]
    aux = np.zeros((2 * l, 128), np.float32)
    aux[:, 0] = t[pos]
    aux[:, 1] = 1.0
    aux[l, 1] = 0.0
    return jnp.asarray(z2.T.copy()), jnp.asarray(aux)


def _filter_ffn_kernel(z_ref, w1, b1, w2, b2, w3, b3, fr, o_ref):
    f = fr[...]
    h = jnp.sin(f * (jnp.dot(w1[...], z_ref[...], precision=HI, preferred_element_type=F32) + b1[...]))
    h = jnp.sin(f * (jnp.dot(w2[...], h, precision=HI, preferred_element_type=F32) + b2[...]))
    h = jnp.sin(f * (jnp.dot(w3[...], h, precision=HI, preferred_element_type=F32) + b3[...]))
    o_ref[...] = h.astype(o_ref.dtype)


def _filter_expand_kernel(h_ref, aux_ref, w4, dec, o_ref):
    k = lax.dot_general(h_ref[...], w4[...].astype(BF16), (((0,), (0,)), ((), ())),
                        preferred_element_type=F32)
    t = aux_ref[:, 0:1]
    keep = aux_ref[:, 1:2]
    o_ref[...] = (k * jnp.exp(-t * jnp.abs(dec[0])) * keep).astype(o_ref.dtype)


def _filters(seq_len, w1, b1, w2, b2, w3, b3, w4, freq, decay):
    e = decay.shape[-1]
    fo = w2.shape[0]
    z2t, aux = _filter_tables(seq_len)
    w1t = jnp.zeros((fo, 128), F32).at[:, :w1.shape[0]].set(w1.T)
    tr = _tile(seq_len, 1024, 128)
    tc = _tile(e, 2048, 128)
    nb = e // tc
    nr = seq_len // tr
    small = lambda a: pl.BlockSpec(a.shape, lambda *_: (0,) * a.ndim)
    b1c, b2c, b3c, frc = (a.reshape(fo, 1) for a in (b1, b2, b3, freq))
    w2t, w3t = w2.T, w3.T
    h3 = pl.pallas_call(
        _filter_ffn_kernel,
        grid=(2 * nr,),
        in_specs=[pl.BlockSpec((128, tr), lambda i: (0, i)),
                  small(w1t), small(b1c), small(w2t), small(b2c), small(w3t), small(b3c), small(frc)],
        out_specs=pl.BlockSpec((fo, tr), lambda i: (0, i)),
        out_shape=jax.ShapeDtypeStruct((fo, 2 * seq_len), BF16),
        compiler_params=_cparams(("arbitrary",)),
        name="hy_filter_ffn",
    )(z2t, w1t, b1c, w2t, b2c, w3t, b3c, frc)
    return pl.pallas_call(
        _filter_expand_kernel,
        grid=(2 * nr, nb),
        in_specs=[pl.BlockSpec((fo, tr), lambda i, j: (0, i)),
                  pl.BlockSpec((tr, 128), lambda i, j: (i, 0)),
                  pl.BlockSpec((fo, tc), lambda i, j: (0, (i // nr) * nb + j)),
                  pl.BlockSpec((1, 1, tc), lambda i, j: (i // nr, 0, j))],
        out_specs=pl.BlockSpec((tr, tc), lambda i, j: (i, j)),
        out_shape=jax.ShapeDtypeStruct((2 * seq_len, e), BF16),
        compiler_params=_cparams(("arbitrary", "arbitrary")),
        name="hy_filter_expand",
    )(h3, aux, w4, decay.reshape(2, 1, e))


def _cmm_kernel(w_ref, x_ref, o_ref):
    o_ref[0] = jnp.dot(w_ref[...], x_ref[0], preferred_element_type=F32).astype(o_ref.dtype)


def _cmm(w, x, out_dtype):
    m, k = w.shape
    b, _, n = x.shape
    tn = _tile(n, 4096, 128)
    return pl.pallas_call(
        _cmm_kernel,
        grid=(b, n // tn),
        in_specs=[pl.BlockSpec((m, k), lambda bi, j: (0, 0)),
                  pl.BlockSpec((1, k, tn), lambda bi, j: (bi, 0, j))],
        out_specs=pl.BlockSpec((1, m, tn), lambda bi, j: (bi, 0, j)),
        out_shape=jax.ShapeDtypeStruct((b, m, n), out_dtype),
        compiler_params=_cparams(("arbitrary", "arbitrary")),
        name="const_lhs_matmul",
    )(w, x)


def _direct_mats(seq_len):
    l = seq_len
    n = 2 * l
    f = np.arange(l)[:, None]
    t = np.arange(n)[None, :]
    ang = 2.0 * math.pi * ((f * t) % n) / n
    fwd = np.concatenate([np.cos(ang), -np.sin(ang)], axis=0)
    fwd[l] = np.cos(math.pi * np.arange(n))
    tt = np.arange(l)[:, None]
    ff = np.arange(l)[None, :]
    ang2 = 2.0 * math.pi * ((tt * ff) % n) / n
    wgt = np.where(ff == 0, 1.0, 2.0) / n
    inv = np.concatenate([wgt * np.cos(ang2), -wgt * np.sin(ang2)], axis=1)
    inv[:, l] = np.cos(math.pi * np.arange(l)) / n
    return fwd, inv


def _direct_conv_kernel(seq_len, f_ref, v_ref, x_ref, k_ref, g0_ref, bias_ref, o_ref):
    l = seq_len
    x = x_ref[0]
    s = jnp.dot(f_ref[...], x, preferred_element_type=F32)
    sre, sim = s[:l], s[l:]
    kre, kim = k_ref[:l, :], k_ref[l:, :]
    row0 = lax.broadcasted_iota(jnp.int32, sre.shape, 0) == 0
    yre = sre * kre - jnp.where(row0, 0.0, sim * kim)
    yim = jnp.where(row0, sim * kim, sre * kim + sim * kre)
    y = jnp.concatenate([yre, yim], axis=0).astype(BF16)
    out = jnp.dot(v_ref[...], y, preferred_element_type=F32)
    out = out + x.astype(F32) * bias_ref[...]
    o_ref[0] = (out * g0_ref[0].astype(F32)).astype(o_ref.dtype)


def _longconv_direct(vx, g0, kfilt, bias, batch, seq_len):
    e = vx.shape[-1]
    l = seq_len
    fwd, inv = _direct_mats(l)
    f_full = jnp.asarray(fwd, BF16)
    f_data = jnp.asarray(fwd[:, :l], BF16)
    v_mat = jnp.asarray(inv, BF16)
    kspec = _cmm(f_full, kfilt.astype(BF16)[None], F32)[0]
    tc = _tile(e, 1024, 128)
    out = pl.pallas_call(
        functools.partial(_direct_conv_kernel, l),
        grid=(e // tc, batch),
        in_specs=[pl.BlockSpec((2 * l, l), lambda j, b: (0, 0)),
                  pl.BlockSpec((l, 2 * l), lambda j, b: (0, 0)),
                  pl.BlockSpec((1, l, tc), lambda j, b: (b, 0, j)),
                  pl.BlockSpec((2 * l, tc), lambda j, b: (0, j)),
                  pl.BlockSpec((1, l, tc), lambda j, b: (b, 0, j)),
                  pl.BlockSpec((1, tc), lambda j, b: (0, j))],
        out_specs=pl.BlockSpec((1, l, tc), lambda j, b: (b, 0, j)),
        out_shape=jax.ShapeDtypeStruct((batch, l, e), BF16),
        compiler_params=_cparams(("arbitrary", "arbitrary")),
        name="longconv_direct",
    )(f_data, v_mat, vx.reshape(batch, l, e), kspec, g0.reshape(batch, l, e), bias.reshape(1, e))
    return out.reshape(batch * l, e)


def _two_level_mats(seq_len):
    n = 2 * seq_len
    n2 = FFT_N2
    n1 = n // n2
    h2 = n2 // 2
    ns = 8 * ((h2 + 1 + 7) // 8)
    f2 = np.arange(h2 + 1)[:, None]
    t2 = np.arange(n2)[None, :]
    ang = 2.0 * math.pi * ((f2 * t2) % n2) / n2
    f1m = np.zeros((2 * ns, n2))
    f1m[0:2 * (h2 + 1):2] = np.cos(ang)
    f1m[1:2 * (h2 + 1):2] = -np.sin(ang)
    wgt = np.where((f2 == 0) | (f2 == h2), 1.0, 2.0) / n
    g1m = np.zeros((h2, 2 * ns))
    g1m[:, 0:2 * (h2 + 1):2] = (wgt * np.cos(ang[:, :h2])).T
    g1m[:, 1:2 * (h2 + 1):2] = -(wgt * np.sin(ang[:, :h2])).T
    t1 = np.arange(n1)[None, :]
    f1 = np.arange(n1)[:, None]
    rm = np.zeros((ns, 2 * n1, 2 * n1))
    pm = np.zeros((ns, 2 * n1, 2 * n1))
    for s in range(h2 + 1):
        a = 2.0 * math.pi * ((t1 * (n2 * f1 + s)) % n) / n
        mr, mi = np.cos(a), -np.sin(a)
        rm[s] = np.block([[mr, -mi], [mi, mr]])
        pr, pi = mr.T, -mi.T
        pm[s] = np.block([[pr, -pi], [pi, pr]])
    return n1, n2, ns, f1m, g1m, rm, pm


def _slab_spec_kernel(r_ref, a_ref, o_ref):
    n1 = a_ref.shape[3]
    a = jnp.concatenate([a_ref[0, 0, 0], a_ref[0, 0, 1]], axis=0)
    s = jnp.dot(r_ref[0], a, preferred_element_type=F32)
    o_ref[0, 0] = s[:n1].astype(o_ref.dtype)
    o_ref[0, 1] = s[n1:].astype(o_ref.dtype)


def _slab_conv_kernel(r_ref, p_ref, a_ref, k_ref, o_ref):
    n1 = a_ref.shape[3]
    kre, kim = k_ref[0, 0].astype(F32), k_ref[0, 1].astype(F32)
    for b in range(a_ref.shape[0]):
        a = jnp.concatenate([a_ref[b, 0, 0], a_ref[b, 0, 1]], axis=0)
        s = jnp.dot(r_ref[0], a, preferred_element_type=F32)
        sre, sim = s[:n1], s[n1:]
        y = jnp.concatenate([sre * kre - sim * kim, sre * kim + sim * kre], axis=0).astype(BF16)
        o = jnp.dot(p_ref[0], y, preferred_element_type=F32)
        o_ref[b, 0, 0] = o[:n1].astype(o_ref.dtype)
        o_ref[b, 0, 1] = o[n1:].astype(o_ref.dtype)


def _dft_l1_kernel(k_ref, x_ref, o_ref):
    x = x_ref[0].astype(F32)
    nt2, _, tc = x.shape
    halves = []
    for g in range(SUB_TILES):
        xg = x[:, SUBLANES * g:SUBLANES * (g + 1), :].reshape(nt2 * SUBLANES, tc).astype(BF16)
        a = jnp.dot(k_ref[...], xg, preferred_element_type=F32)
        halves.append(a.reshape(-1, SUBLANES, tc))
    o_ref[0] = jnp.concatenate(halves, axis=1).astype(o_ref.dtype)


def _dft_l1_inv_kernel(g_ref, a_ref, vx_ref, g0_ref, bias_ref, o_ref):
    a = a_ref[0].astype(F32)
    nr, _, tc = a.shape
    halves = []
    for g in range(SUB_TILES):
        ag = a[:, SUBLANES * g:SUBLANES * (g + 1), :].reshape(nr * SUBLANES, tc).astype(BF16)
        y = jnp.dot(g_ref[...], ag, preferred_element_type=F32)
        halves.append(y.reshape(-1, SUBLANES, tc))
    y = jnp.concatenate(halves, axis=1)
    y = y + vx_ref[0].astype(F32) * bias_ref[...]
    o_ref[0] = (y * g0_ref[0].astype(F32)).astype(o_ref.dtype)


def _dft_l1(kmat, x4, out_dtype):
    b, nt2, n1, e = x4.shape
    m = kmat.shape[0] // SUBLANES
    tc = _tile(e, 1024, 128)
    rows = SUBLANES * SUB_TILES
    return pl.pallas_call(
        _dft_l1_kernel,
        grid=(b, n1 // rows, e // tc),
        in_specs=[pl.BlockSpec(kmat.shape, lambda bi, i, j: (0, 0)),
                  pl.BlockSpec((1, nt2, rows, tc), lambda bi, i, j: (bi, 0, i, j))],
        out_specs=pl.BlockSpec((1, m, rows, tc), lambda bi, i, j: (bi, 0, i, j)),
        out_shape=jax.ShapeDtypeStruct((b, m, n1, e), out_dtype),
        compiler_params=_cparams(("arbitrary", "arbitrary", "arbitrary")),
        name="dft_level1",
    )(kmat, x4)


def _longconv_two_level(vx, g0, kfilt, bias, batch, seq_len):
    e = vx.shape[-1]
    l = seq_len
    n1, n2, ns, f1m, g1m, rm, pm = _two_level_mats(l)
    h2 = n2 // 2
    eye = np.eye(SUBLANES)
    k_full = jnp.asarray(np.kron(f1m, eye), BF16)
    k_data = jnp.asarray(np.kron(f1m[:, :h2], eye), BF16)
    g8 = jnp.asarray(np.kron(g1m, eye), BF16)
    rmat = jnp.asarray(rm, BF16)
    pmat = jnp.asarray(pm, BF16)
    tc = _tile(e, 4096, 128)
    nc = e // tc

    ka = _dft_l1(k_full, kfilt.reshape(1, n2, n1, e), BF16).reshape(1, ns, 2, n1, e)
    kspec = pl.pallas_call(
        _slab_spec_kernel,
        grid=(ns, nc),
        in_specs=[pl.BlockSpec((1, 2 * n1, 2 * n1), lambda s, j: (s, 0, 0)),
                  pl.BlockSpec((1, 1, 2, n1, tc), lambda s, j: (0, s, 0, 0, j))],
        out_specs=pl.BlockSpec((1, 2, n1, tc), lambda s, j: (s, 0, 0, j)),
        out_shape=jax.ShapeDtypeStruct((ns, 2, n1, e), BF16),
        compiler_params=_cparams(("arbitrary", "arbitrary")),
        name="filter_slab_dft",
    )(rmat, ka)

    vx4 = vx.reshape(batch, h2, n1, e)
    a = _dft_l1(k_data, vx4, BF16).reshape(batch, ns, 2, n1, e)
    a2 = pl.pallas_call(
        _slab_conv_kernel,
        grid=(ns, nc),
        in_specs=[pl.BlockSpec((1, 2 * n1, 2 * n1), lambda s, j: (s, 0, 0)),
                  pl.BlockSpec((1, 2 * n1, 2 * n1), lambda s, j: (s, 0, 0)),
                  pl.BlockSpec((batch, 1, 2, n1, tc), lambda s, j: (0, s, 0, 0, j)),
                  pl.BlockSpec((1, 2, n1, tc), lambda s, j: (s, 0, 0, j))],
        out_specs=pl.BlockSpec((batch, 1, 2, n1, tc), lambda s, j: (0, s, 0, 0, j)),
        out_shape=jax.ShapeDtypeStruct((batch, ns, 2, n1, e), BF16),
        compiler_params=_cparams(("arbitrary", "arbitrary")),
        name="slab_conv",
    )(rmat, pmat, a, kspec)
    a2 = a2.reshape(batch, 2 * ns, n1, e)
    rows = SUBLANES * SUB_TILES
    tci = _tile(e, 1024, 128)
    blk = lambda r: pl.BlockSpec((1, r, rows, tci), lambda bi, i, j: (bi, 0, i, j))
    out = pl.pallas_call(
        _dft_l1_inv_kernel,
        grid=(batch, n1 // rows, e // tci),
        in_specs=[pl.BlockSpec(g8.shape, lambda bi, i, j: (0, 0)),
                  blk(2 * ns), blk(h2), blk(h2),
                  pl.BlockSpec((1, 1, tci), lambda bi, i, j: (0, 0, j))],
        out_specs=blk(h2),
        out_shape=jax.ShapeDtypeStruct((batch, h2, n1, e), BF16),
        compiler_params=_cparams(("arbitrary", "arbitrary", "arbitrary")),
        name="dft_level1_inverse",
    )(g8, a2, vx4, g0.reshape(batch, h2, n1, e), bias.reshape(1, 1, e))
    return out.reshape(batch * l, e)


def _longconv(vx, g0, kfilt, bias, batch, seq_len):
    if seq_len <= DIRECT_CONV_MAX_L:
        return _longconv_direct(vx, g0, kfilt, bias, batch, seq_len)
    return _longconv_two_level(vx, g0, kfilt, bias, batch, seq_len)


def _ml_qkv_kernel(seq_len, n_heads, dk, x_ref, xp_ref, xn_ref, cw_ref, cb_ref, wq_ref, wk_ref, wv_ref,
                   wgq_ref, wgk_ref, wgv_ref, bg_ref, q_ref, k_ref, kt_ref, v_ref, xc_ref, g_ref, gt_ref, gacc):
    h = pl.program_id(1)
    tm = x_ref.shape[0]
    q_len = gt_ref.shape[2]
    x = x_ref[...].astype(F32)
    conv = _conv3(x, xp_ref[HALO - 1:HALO, :].astype(F32), xn_ref[0:1, :].astype(F32),
                  cw_ref[...], cb_ref[...], pl.program_id(0) * tm, seq_len)
    xc = _silu(conv)
    xc_ref[...] = xc.astype(xc_ref.dtype)
    xcb = xc.astype(BF16)
    q = jnp.dot(xcb, wq_ref[0], preferred_element_type=F32).astype(BF16)
    kf = jnp.dot(xcb, wk_ref[0], preferred_element_type=F32) * (dk ** -0.5)
    k = kf.astype(BF16)
    v = jnp.dot(x.astype(BF16), wv_ref[0], preferred_element_type=F32).astype(BF16)
    q_ref[...] = q
    k_ref[...] = k
    v_ref[...] = v
    for c in range(tm // q_len):
        kt_ref[c] = kf[c * q_len:(c + 1) * q_len, :].T.astype(BF16)
    part = (jnp.dot(q, wgq_ref[0], preferred_element_type=F32)
            + jnp.dot(k, wgk_ref[0], preferred_element_type=F32)
            + jnp.dot(v, wgv_ref[0], preferred_element_type=F32))

    @pl.when(h == 0)
    def _():
        gacc[...] = part + bg_ref[...]

    @pl.when(h > 0)
    def _():
        gacc[...] += part

    @pl.when(h == n_heads - 1)
    def _():
        g = gacc[...]
        col = lax.broadcasted_iota(jnp.int32, g.shape, 1)
        is_forget = (col % (2 * n_heads)) >= n_heads
        g = jnp.where(is_forget, jax.nn.log_sigmoid(g), g)
        col_q = lax.broadcasted_iota(jnp.int32, (q_len, g.shape[1]), 1)
        forget_q = (col_q % (2 * n_heads)) >= n_heads
        bwd_q = (col_q // (2 * n_heads)) == 1
        r_i = lax.broadcasted_iota(jnp.int32, (q_len, q_len), 0)
        c_i = lax.broadcasted_iota(jnp.int32, (q_len, q_len), 1)
        tril = (c_i <= r_i).astype(F32)
        for c in range(tm // q_len):
            gc = g[c * q_len:(c + 1) * q_len, :]
            pre = jnp.dot(tril, gc, precision=HI, preferred_element_type=F32)
            tot = jnp.sum(gc, axis=0, keepdims=True)
            suf = tot - pre + gc
            gc = jnp.where(forget_q, jnp.where(bwd_q, suf, pre), gc)
            g_ref[c * q_len:(c + 1) * q_len, :] = gc
            gt_ref[c] = gc.T


def _ml_qkv(p, conv_w, conv_b, wq, wk, wv, w_gate, b_gate, batch, seq_len, e):
    t = p.shape[0]
    n_heads, dh, dk = wq.shape
    dv = wv.shape[2]
    q_len = min(SCAN_CHUNK, seq_len)
    tm = _row_tile(batch, seq_len, 1024, q_len)
    ng = w_gate.shape[1]
    gpad = 128
    wgq = jnp.zeros((n_heads, dk, gpad), BF16).at[:, :, :ng].set(
        w_gate[:n_heads * dk].reshape(n_heads, dk, ng).astype(BF16))
    wgk = jnp.zeros((n_heads, dk, gpad), BF16).at[:, :, :ng].set(
        w_gate[n_heads * dk:2 * n_heads * dk].reshape(n_heads, dk, ng).astype(BF16))
    wgv = jnp.zeros((n_heads, dv, gpad), BF16).at[:, :, :ng].set(
        w_gate[2 * n_heads * dk:].reshape(n_heads, dv, ng).astype(BF16))
    bg = jnp.zeros((1, gpad), F32).at[0, :ng].set(b_gate)
    specs = _halo_specs(tm, dh, t, lambda j: j)
    specs += [pl.BlockSpec((3, dh), lambda i, j: (0, j)),
              pl.BlockSpec((1, dh), lambda i, j: (0, j)),
              pl.BlockSpec((1, dh, dk), lambda i, j: (j, 0, 0)),
              pl.BlockSpec((1, dh, dk), lambda i, j: (j, 0, 0)),
              pl.BlockSpec((1, dh, dv), lambda i, j: (j, 0, 0)),
              pl.BlockSpec((1, dk, gpad), lambda i, j: (j, 0, 0)),
              pl.BlockSpec((1, dk, gpad), lambda i, j: (j, 0, 0)),
              pl.BlockSpec((1, dv, gpad), lambda i, j: (j, 0, 0)),
              pl.BlockSpec((1, gpad), lambda i, j: (0, 0))]
    return pl.pallas_call(
        functools.partial(_ml_qkv_kernel, seq_len, n_heads, dk),
        grid=(t // tm, n_heads),
        in_specs=specs,
        out_specs=[pl.BlockSpec((tm, dk), lambda i, j: (i, j)),
                   pl.BlockSpec((tm, dk), lambda i, j: (i, j)),
                   pl.BlockSpec((tm // q_len, dk, q_len), lambda i, j: (i, j, 0)),
                   pl.BlockSpec((tm, dv), lambda i, j: (i, j)),
                   pl.BlockSpec((tm, dh), lambda i, j: (i, j)),
                   pl.BlockSpec((tm, gpad), lambda i, j: (i, 0)),
                   pl.BlockSpec((tm // q_len, gpad, q_len), lambda i, j: (i, 0, 0))],
        out_shape=[jax.ShapeDtypeStruct((t, n_heads * dk), BF16),
                   jax.ShapeDtypeStruct((t, n_heads * dk), BF16),
                   jax.ShapeDtypeStruct((t // q_len, n_heads * dk, q_len), BF16),
                   jax.ShapeDtypeStruct((t, n_heads * dv), BF16),
                   jax.ShapeDtypeStruct((t, e), BF16),
                   jax.ShapeDtypeStruct((t, gpad), F32),
                   jax.ShapeDtypeStruct((t // q_len, gpad, q_len), F32)],
        scratch_shapes=[pltpu.VMEM((tm, gpad), F32)],
        compiler_params=_cparams(("arbitrary", "arbitrary")),
        name="ml_qkv_gates",
    )(p, p, p, conv_w, conv_b.reshape(1, e), wq.astype(BF16), wk.astype(BF16), wv.astype(BF16),
      wgq, wgk, wgv, bg)


def _ml_scan_kernel(n_heads, has_init, *refs):
    if has_init:
        (q_ref, k_ref, kt_ref, v_ref, g_ref, gt_ref, c0_ref, n0_ref, m0_ref,
         h_ref, cout_ref, nout_ref, mout_ref, c_scr, n_scr, m_scr) = refs
    else:
        (q_ref, k_ref, kt_ref, v_ref, g_ref, gt_ref,
         h_ref, cout_ref, nout_ref, mout_ref, c_scr, n_scr, m_scr) = refs
    head = pl.program_id(1)
    seq_len = q_ref.shape[0]
    q_len = gt_ref.shape[2]
    n_chunks = seq_len // q_len

    if has_init:
        for d in range(2):
            c_scr[d] = c0_ref[0, 0, d, 0]
            n_scr[d] = n0_ref[0, 0, d]
            m_scr[d] = m0_ref[0, 0, d]
    h_ref[...] = jnp.zeros_like(h_ref)

    row_i = lax.broadcasted_iota(jnp.int32, (q_len, q_len), 0)
    col_i = lax.broadcasted_iota(jnp.int32, (q_len, q_len), 1)
    lane_g = lax.broadcasted_iota(jnp.int32, (q_len, g_ref.shape[1]), 1)

    def direction(d, c, from_zero=False):
        icol = d * 2 * n_heads + head
        fcol = icol + n_heads
        start = c * q_len if isinstance(c, int) else pl.multiple_of(c * q_len, q_len)
        rows = pl.ds(start, q_len)
        g = g_ref[rows, :]
        i_col = jnp.sum(jnp.where(lane_g == icol, g, 0.0), axis=1, keepdims=True)
        b_col = jnp.sum(jnp.where(lane_g == fcol, g, 0.0), axis=1, keepdims=True)
        i_row = gt_ref[c, pl.ds(icol, 1), :]
        b_row = gt_ref[c, pl.ds(fcol, 1), :]
        mask = (col_i <= row_i) if d == 0 else (col_i >= row_i)
        last = q_len - 1 if d == 0 else 0
        g_tot = jnp.sum(jnp.where(col_i[0:1, :] == last, b_row, 0.0), axis=1, keepdims=True)
        m_prev = 0.0 if from_zero else m_scr[d][:, 0:1]
        if not from_zero:
            n_prev = n_scr[d]
            c_prev = c_scr[d]

        qc = q_ref[rows, :]
        kc = k_ref[rows, :]
        vc = v_ref[rows, :]
        dmat = jnp.where(mask, b_col - b_row + i_row, -jnp.inf)
        inter = b_col + m_prev
        m_t = jnp.maximum(inter, jnp.max(dmat, axis=1, keepdims=True))
        qk = lax.dot_general(qc, kc, (((1,), (1,)), ((), ())), preferred_element_type=F32)
        s = qk * jnp.exp(dmat - m_t)
        if from_zero:
            num = jnp.dot(s.astype(BF16), vc, preferred_element_type=F32)
            qn = jnp.sum(s, axis=1, keepdims=True)
        else:
            w_inter = jnp.exp(inter - m_t)
            num = (w_inter * jnp.dot(qc, c_prev.astype(BF16), preferred_element_type=F32)
                   + jnp.dot(s.astype(BF16), vc, preferred_element_type=F32))
            qn = (w_inter * jnp.sum(qc.astype(F32) * n_prev, axis=1, keepdims=True)
                  + jnp.sum(s, axis=1, keepdims=True))
        den = jnp.maximum(jnp.abs(qn), jnp.exp(-m_t))
        h_ref[rows, :] += num / den

        a_col = g_tot - b_col + i_col
        a_row = g_tot - b_row + i_row
        m_new = jnp.maximum(g_tot + m_prev, jnp.max(a_row, axis=1, keepdims=True))
        kwt = (kt_ref[c].astype(F32) * jnp.exp(a_row - m_new)).astype(BF16)
        kv = jnp.dot(kwt, vc, preferred_element_type=F32)
        ksum = jnp.sum(kc.astype(F32) * jnp.exp(a_col - m_new), axis=0, keepdims=True)
        if from_zero:
            c_scr[d] = kv
            n_scr[d] = ksum
        else:
            dec = jnp.exp(g_tot + m_prev - m_new)
            c_scr[d] = dec * c_prev + kv
            n_scr[d] = dec * n_prev + ksum
        m_scr[d] = jnp.broadcast_to(m_new, m_scr.shape[1:])

    def body(j, carry):
        direction(0, j)
        direction(1, n_chunks - 1 - j)
        return carry

    first = 0
    if not has_init:
        direction(0, 0, from_zero=True)
        direction(1, n_chunks - 1, from_zero=True)
        first = 1
    lax.fori_loop(first, n_chunks, body, 0)
    for d in range(2):
        cout_ref[0, 0, d, 0] = c_scr[d]
        nout_ref[0, 0, d] = n_scr[d]
        mout_ref[0, 0, d] = m_scr[d]


def _ml_scan(q, k, kt, v, g, gt, batch, seq_len, n_heads, init=None):
    dk = q.shape[1] // n_heads
    dv = v.shape[1] // n_heads
    q_len = gt.shape[2]
    gpad = g.shape[1]
    nq = seq_len // q_len
    in_specs = [pl.BlockSpec((seq_len, dk), lambda b, h: (b, h)),
                pl.BlockSpec((seq_len, dk), lambda b, h: (b, h)),
                pl.BlockSpec((nq, dk, q_len), lambda b, h: (b, h, 0)),
                pl.BlockSpec((seq_len, dv), lambda b, h: (b, h)),
                pl.BlockSpec((seq_len, gpad), lambda b, h: (b, 0)),
                pl.BlockSpec((nq, gpad, q_len), lambda b, h: (b, 0, 0))]
    args = [q, k, kt, v, g, gt]
    if init is not None:
        c0, n0, m0 = init
        in_specs += [pl.BlockSpec((1, 1, 2, 1, dk, dv), lambda b, h: (b, 0, 0, h, 0, 0)),
                     pl.BlockSpec((1, 1, 2, 1, dk), lambda b, h: (b, h, 0, 0, 0)),
                     pl.BlockSpec((1, 1, 2, 1, 128), lambda b, h: (b, h, 0, 0, 0))]
        args += [c0, n0, m0]
    out = pl.pallas_call(
        functools.partial(_ml_scan_kernel, n_heads, init is not None),
        grid=(batch, n_heads),
        in_specs=in_specs,
        out_specs=[pl.BlockSpec((seq_len, dv), lambda b, h: (b, h)),
                   pl.BlockSpec((1, 1, 2, 1, dk, dv), lambda b, h: (b, 0, 0, h, 0, 0)),
                   pl.BlockSpec((1, 1, 2, 1, dk), lambda b, h: (b, h, 0, 0, 0)),
                   pl.BlockSpec((1, 1, 2, 1, 128), lambda b, h: (b, h, 0, 0, 0))],
        out_shape=[jax.ShapeDtypeStruct((batch * seq_len, n_heads * dv), F32),
                   jax.ShapeDtypeStruct((batch, 1, 2, n_heads, dk, dv), F32),
                   jax.ShapeDtypeStruct((batch, n_heads, 2, 1, dk), F32),
                   jax.ShapeDtypeStruct((batch, n_heads, 2, 1, 128), F32)],
        scratch_shapes=[pltpu.VMEM((2, dk, dv), F32), pltpu.VMEM((2, 1, dk), F32),
                        pltpu.VMEM((2, 1, 128), F32)],
        compiler_params=_cparams(("arbitrary", "arbitrary")),
        name="ml_scan",
    )(*args)
    return out


def _ml_out_kernel(h_ref, o_ref_in, xc_ref, z_ref, nw_ref, sk_ref, w_ref, x_ref, g_ref, gate_ref, out_ref, acc):
    k = pl.program_id(1)

    @pl.when(k == 0)
    def _():
        acc[...] = jnp.zeros_like(acc)

    dv = w_ref.shape[0] // HEADS_PER_STEP
    ys = []
    for hd in range(HEADS_PER_STEP):
        cols = slice(hd * dv, (hd + 1) * dv)
        hh = jax.nn.sigmoid(o_ref_in[:, cols].astype(F32)) * h_ref[:, cols]
        mu = jnp.mean(hh, axis=-1, keepdims=True)
        var = jnp.mean(jnp.square(hh - mu), axis=-1, keepdims=True)
        hn = (hh - mu) * lax.rsqrt(var + EPS) * nw_ref[:, cols]
        hn = hn + sk_ref[:, cols] * xc_ref[:, cols].astype(F32)
        ys.append((hn * _silu(z_ref[:, cols].astype(F32))).astype(BF16))
    y = jnp.concatenate(ys, axis=1)
    acc[...] += jnp.dot(y, w_ref[...].astype(BF16), preferred_element_type=F32)

    @pl.when(k == pl.num_programs(1) - 1)
    def _():
        _outproj_epilogue(acc[...], x_ref, g_ref, gate_ref, out_ref)


def _ml_out(h, p, xc, norm_w, skip, w, x, norm_g, mod, trunk, n_heads):
    t, e = h.shape
    d = w.shape[1]
    assert n_heads % HEADS_PER_STEP == 0
    nk = n_heads // HEADS_PER_STEP
    dv = HEADS_PER_STEP * (e // n_heads)
    tm = _cond_row_tile(trunk, 512)
    cidx = _cond_index(trunk, tm)
    return pl.pallas_call(
        _ml_out_kernel,
        grid=(t // tm, nk),
        in_specs=[pl.BlockSpec((tm, dv), lambda i, k: (i, k)),
                  pl.BlockSpec((tm, dv), lambda i, k: (i, 2 * nk + k)),
                  pl.BlockSpec((tm, dv), lambda i, k: (i, k)),
                  pl.BlockSpec((tm, dv), lambda i, k: (i, nk + k)),
                  pl.BlockSpec((1, dv), lambda i, k: (0, k)),
                  pl.BlockSpec((1, dv), lambda i, k: (0, k)),
                  pl.BlockSpec((dv, d), lambda i, k: (k, 0)),
                  pl.BlockSpec((tm, d), lambda i, k: (i, 0)),
                  pl.BlockSpec((1, d), lambda i, k: (0, 0)),
                  pl.BlockSpec((1, 1, d), lambda i, k: (cidx(i), 0, 2))],
        out_specs=pl.BlockSpec((tm, d), lambda i, k: (i, 0)),
        out_shape=jax.ShapeDtypeStruct((t, d), F32),
        scratch_shapes=[pltpu.VMEM((tm, d), F32)],
        compiler_params=_cparams(("arbitrary", "arbitrary")),
        name="ml_out_proj",
    )(h, p, xc, p, norm_w.reshape(1, e), skip.reshape(1, e), w, x, norm_g.reshape(1, d), mod)


def _hyena_layer(x, mod, trunk, norm_pre, norm_post, hw):
    (w_in, conv_w, conv_b, w1, b1, w2, b2, w3, b3, w4, freq, decay, bias, w_out) = hw
    batch, seq_len = trunk[0], trunk[1]
    e = w_out.shape[0]
    vx, g0 = _hy_inproj(x, norm_pre, mod, trunk, w_in, conv_w, conv_b, e)
    kfilt = _filters(seq_len, w1, b1, w2, b2, w3, b3, w4, freq, decay)
    y = _longconv(vx, g0, kfilt, bias, batch, seq_len)
    return _outproj(y, w_out.astype(BF16), x, norm_post, mod, trunk)


def _mlstm_layer(x, mod, trunk, norm_pre, norm_post, mw, init):
    (w_in, conv_w, conv_b, wq, wk, wv, w_gate, b_gate, norm_w, skip, w_out) = mw
    batch, seq_len = trunk[0], trunk[1]
    e = w_out.shape[0]
    n_heads = wq.shape[0]
    p = _inproj(x, norm_pre, mod, trunk, w_in, BF16)
    q, k, kt, v, xc, g, gt = _ml_qkv(p, conv_w, conv_b, wq, wk, wv, w_gate, b_gate, batch, seq_len, e)
    h, c_new, n_new, m_new = _ml_scan(q, k, kt, v, g, gt, batch, seq_len, n_heads, init)
    x_new = _ml_out(h, p, xc, norm_w, skip, w_out.astype(BF16), x, norm_post, mod, trunk, n_heads)
    return x_new, c_new, n_new, m_new


def kernel(x_prompt, x_sample, state_C, state_n, state_m, c, c_ctx, norm_pre, norm_post, ada_w, ada_b,
           hy_w_in, hy_conv_w, hy_conv_b, hy_ffn_w1, hy_ffn_b1, hy_ffn_w2, hy_ffn_b2, hy_ffn_w3, hy_ffn_b3,
           hy_ffn_w4, hy_sin_freq, hy_decay, hy_bias, hy_w_out, ml_w_in, ml_conv_w, ml_conv_b, ml_wq, ml_wk,
           ml_wv, ml_w_gate, ml_b_gate, ml_norm, ml_skip, ml_w_out):
    depth = norm_pre.shape[0]
    bp, lp, d = x_prompt.shape
    bs, ls, _ = x_sample.shape
    n_heads = ml_wq.shape[1]

    nrow = 8 * ((1 + bs + 7) // 8)
    conds = jnp.zeros((nrow, d), F32).at[0].set(c_ctx).at[1:1 + bs].set(c)
    mods = _ada(conds, ada_w, ada_b)

    def run_trunk(x3, cond0, per_seq, init_states):
        batch, seq_len, _ = x3.shape
        trunk = (batch, seq_len, cond0, per_seq)
        x = x3.reshape(batch * seq_len, d)
        new_c, new_n, new_m = [], [], []
        for i in range(depth):
            mod = mods[i].reshape(nrow, 1, 3 * d)
            j = i // 2
            if i % 2 == 0:
                hw = (hy_w_in[j], hy_conv_w[j], hy_conv_b[j], hy_ffn_w1[j], hy_ffn_b1[j], hy_ffn_w2[j],
                      hy_ffn_b2[j], hy_ffn_w3[j], hy_ffn_b3[j], hy_ffn_w4[j], hy_sin_freq[j], hy_decay[j],
                      hy_bias[j], hy_w_out[j])
                x = _hyena_layer(x, mod, trunk, norm_pre[i], norm_post[i], hw)
            else:
                mw = (ml_w_in[j], ml_conv_w[j], ml_conv_b[j], ml_wq[j], ml_wk[j], ml_wv[j], ml_w_gate[j],
                      ml_b_gate[j], ml_norm[j], ml_skip[j], ml_w_out[j])
                init = None if init_states is None else init_states(j)
                x, cj, nj, mj = _mlstm_layer(x, mod, trunk, norm_pre[i], norm_post[i], mw, init)
                new_c.append(cj)
                new_n.append(nj)
                new_m.append(mj)
        return x.reshape(batch, seq_len, d), new_c, new_n, new_m

    y_prompt, cs, ns, ms = run_trunk(x_prompt, 0, False, None)
    dk = ml_wq.shape[3]
    new_state_c = jnp.concatenate(cs, axis=1)
    new_state_n = jnp.stack([jnp.swapaxes(n[:, :, :, 0, :], 1, 2) for n in ns], axis=1)
    new_state_m = jnp.stack([jnp.swapaxes(m[:, :, :, 0, 0], 1, 2) for m in ms], axis=1)

    def lat_init(j):
        c0 = state_C[:, j:j + 1]
        n0 = jnp.swapaxes(state_n[:, j], 1, 2)[:, :, :, None, :]
        m0 = jnp.broadcast_to(jnp.swapaxes(state_m[:, j], 1, 2)[:, :, :, None, None],
                              (bs, n_heads, 2, 1, 128))
        return c0, n0, m0

    y_sample, _, _, _ = run_trunk(x_sample, 1, True, lat_init)
    return (y_prompt, y_sample, new_state_c, new_state_n, new_state_m)
```

```python
import functools
import math

import numpy as np
import jax
import jax.numpy as jnp
from jax import lax
from jax.experimental import pallas as pl
from jax.experimental.pallas import tpu as pltpu

F32 = jnp.float32
BF16 = jnp.bfloat16
EPS = 1e-6
N_BANDS = 16
SUBLANES = 8
SUB_TILES = 2
HALO = 16
SCAN_CHUNK = 256
HEADS_PER_STEP = 2
FFT_N2 = 128
DIRECT_CONV_MAX_L = 512
VMEM_LIMIT = 56 * 1024 * 1024
HI = lax.Precision.HIGHEST


def _cparams(sem):
    return pltpu.CompilerParams(dimension_semantics=sem, vmem_limit_bytes=VMEM_LIMIT)


def _tile(n, pref, mult):
    if n <= pref:
        return n
    t = (pref // mult) * mult
    while t >= mult:
        if n % t == 0:
            return t
        t -= mult
    return n


def _row_tile(batch, seq_len, pref, mult):
    if seq_len >= pref:
        return _tile(seq_len, pref, mult)
    k = max(1, pref // seq_len)
    while batch % k:
        k -= 1
    return seq_len * k


def _silu(x):
    return x * jax.nn.sigmoid(x)


def _ada_kernel(c_ref, w_ref, b_ref, o_ref):
    a = _silu(c_ref[...]).astype(BF16)
    o_ref[0] = jnp.dot(a, w_ref[0].astype(BF16), preferred_element_type=F32) + b_ref[0]


def _ada(conds, ada_w, ada_b):
    depth, d, n = ada_w.shape
    rows = conds.shape[0]
    tn = _tile(n, 512, 128)
    return pl.pallas_call(
        _ada_kernel,
        grid=(depth, n // tn),
        in_specs=[pl.BlockSpec((rows, d), lambda l, j: (0, 0)),
                  pl.BlockSpec((1, d, tn), lambda l, j: (l, 0, j)),
                  pl.BlockSpec((1, 1, tn), lambda l, j: (l, 0, j))],
        out_specs=pl.BlockSpec((1, rows, tn), lambda l, j: (l, 0, j)),
        out_shape=jax.ShapeDtypeStruct((depth, rows, n), F32),
        compiler_params=_cparams(("arbitrary", "arbitrary")),
        name="ada_mod",
    )(conds, ada_w, ada_b.reshape(depth, 1, n))


def _inproj_kernel(x_ref, g_ref, shift_ref, scale_ref, w_ref, o_ref, u_scr):
    @pl.when(pl.program_id(1) == 0)
    def _():
        x = x_ref[...]
        r = lax.rsqrt(jnp.mean(x * x, axis=-1, keepdims=True) + EPS)
        u = x * r * g_ref[...]
        u_scr[...] = (u * (1.0 + scale_ref[0]) + shift_ref[0]).astype(BF16)

    o_ref[...] = jnp.dot(u_scr[...], w_ref[...].astype(BF16), preferred_element_type=F32).astype(o_ref.dtype)


def _cond_index(trunk, tm):
    batch, seq_len, cond0, per_seq = trunk
    if per_seq:
        return lambda i: cond0 + (i * tm) // seq_len
    return lambda i: cond0


def _cond_row_tile(trunk, pref):
    batch, seq_len, _, per_seq = trunk
    if per_seq:
        return _tile(seq_len, pref, 16)
    return _tile(batch * seq_len, pref, 16)


def _inproj(x, norm_g, mod, trunk, w, out_dtype):
    t, d = x.shape
    n = w.shape[1]
    tm = _cond_row_tile(trunk, 1024)
    tn = _tile(n, 1024, 128)
    cidx = _cond_index(trunk, tm)
    return pl.pallas_call(
        _inproj_kernel,
        grid=(t // tm, n // tn),
        in_specs=[pl.BlockSpec((tm, d), lambda i, j: (i, 0)),
                  pl.BlockSpec((1, d), lambda i, j: (0, 0)),
                  pl.BlockSpec((1, 1, d), lambda i, j: (cidx(i), 0, 0)),
                  pl.BlockSpec((1, 1, d), lambda i, j: (cidx(i), 0, 1)),
                  pl.BlockSpec((d, tn), lambda i, j: (0, j))],
        out_specs=pl.BlockSpec((tm, tn), lambda i, j: (i, j)),
        out_shape=jax.ShapeDtypeStruct((t, n), out_dtype),
        scratch_shapes=[pltpu.VMEM((tm, d), BF16)],
        compiler_params=_cparams(("arbitrary", "arbitrary")),
        name="in_proj",
    )(x, norm_g.reshape(1, d), mod, mod, w)


def _outproj_epilogue(acc, x_ref, g_ref, gate_ref, o_ref):
    y = acc
    r = lax.rsqrt(jnp.mean(y * y, axis=-1, keepdims=True) + EPS)
    o_ref[...] = x_ref[...] + gate_ref[0] * (y * r * g_ref[...])


def _outproj_kernel(y_ref, w_ref, x_ref, g_ref, gate_ref, o_ref, acc):
    k = pl.program_id(1)

    @pl.when(k == 0)
    def _():
        acc[...] = jnp.zeros_like(acc)

    acc[...] += jnp.dot(y_ref[...], w_ref[...].astype(BF16), preferred_element_type=F32)

    @pl.when(k == pl.num_programs(1) - 1)
    def _():
        _outproj_epilogue(acc[...], x_ref, g_ref, gate_ref, o_ref)


def _outproj(y, w, x, norm_g, mod, trunk):
    t, e = y.shape
    d = w.shape[1]
    tm = _cond_row_tile(trunk, 512)
    tk = _tile(e, 2048, 128)
    cidx = _cond_index(trunk, tm)
    return pl.pallas_call(
        _outproj_kernel,
        grid=(t // tm, e // tk),
        in_specs=[pl.BlockSpec((tm, tk), lambda i, k: (i, k)),
                  pl.BlockSpec((tk, d), lambda i, k: (k, 0)),
                  pl.BlockSpec((tm, d), lambda i, k: (i, 0)),
                  pl.BlockSpec((1, d), lambda i, k: (0, 0)),
                  pl.BlockSpec((1, 1, d), lambda i, k: (cidx(i), 0, 2))],
        out_specs=pl.BlockSpec((tm, d), lambda i, k: (i, 0)),
        out_shape=jax.ShapeDtypeStruct((t, d), F32),
        scratch_shapes=[pltpu.VMEM((tm, d), F32)],
        compiler_params=_cparams(("arbitrary", "arbitrary")),
        name="out_proj",
    )(y, w, x, norm_g.reshape(1, d), mod)


def _conv3(x, prev_row, next_row, w, b, row0, seq_len):
    tm = x.shape[0]
    rows = lax.broadcasted_iota(jnp.int32, (tm, 1), 0)
    pos = (rows + row0) % seq_len
    prev = jnp.where(rows == 0, prev_row, pltpu.roll(x, 1, 0))
    nxt = jnp.where(rows == tm - 1, next_row, pltpu.roll(x, tm - 1, 0))
    prev = jnp.where(pos == 0, 0.0, prev)
    nxt = jnp.where(pos == seq_len - 1, 0.0, nxt)
    return w[0:1] * prev + w[1:2] * x + w[2:3] * nxt + b


def _halo_specs(tm, tc, nrows, colblk):
    nh = nrows // HALO
    r = tm // HALO
    return [pl.BlockSpec((tm, tc), lambda i, j: (i, colblk(j))),
            pl.BlockSpec((HALO, tc), lambda i, j: (jnp.maximum(i * r - 1, 0), colblk(j))),
            pl.BlockSpec((HALO, tc), lambda i, j: (jnp.minimum((i + 1) * r, nh - 1), colblk(j)))]


def _hy_inproj_kernel(seq_len, x_ref, xp_ref, xn_ref, g_ref, shift_ref, scale_ref,
                      w0_ref, w1_ref, w2_ref, wz_ref, cw, cb, vx_ref, g0_ref, u_scr):
    tm = x_ref.shape[0]
    ext = tm + 2 * HALO

    @pl.when(pl.program_id(1) == 0)
    def _():
        def modulated(x):
            r = lax.rsqrt(jnp.mean(x * x, axis=-1, keepdims=True) + EPS)
            return ((x * r * g_ref[...]) * (1.0 + scale_ref[0]) + shift_ref[0]).astype(BF16)

        u_scr[0:HALO] = modulated(xp_ref[...])
        u_scr[HALO:HALO + tm] = modulated(x_ref[...])
        u_scr[HALO + tm:ext] = modulated(xn_ref[...])

    u = u_scr[...]
    rows = lax.broadcasted_iota(jnp.int32, (tm, 1), 0)
    pos = (rows + pl.program_id(0) * tm) % seq_len
    first = pos == 0
    last = pos == seq_len - 1

    def conv(w_ref, k):
        r = jnp.dot(u, w_ref[...].astype(BF16), preferred_element_type=F32)
        prev = jnp.where(first, 0.0, pltpu.roll(r, 1, 0)[HALO:HALO + tm])
        nxt = jnp.where(last, 0.0, pltpu.roll(r, ext - 1, 0)[HALO:HALO + tm])
        w = cw[k]
        return w[0:1] * prev + w[1:2] * r[HALO:HALO + tm] + w[2:3] * nxt + cb[k]

    x1c = conv(w1_ref, 1)
    vc = conv(w2_ref, 2)
    vx_ref[...] = (vc * x1c).astype(vx_ref.dtype)
    x0c = conv(w0_ref, 0)
    z = jnp.dot(u_scr[HALO:HALO + tm], wz_ref[...].astype(BF16), preferred_element_type=F32)
    g0_ref[...] = (x0c * _silu(z)).astype(g0_ref.dtype)


def _hy_inproj(x, norm_g, mod, trunk, w, conv_w, conv_b, e):
    t, d = x.shape
    seq_len = trunk[1]
    tm = _cond_row_tile(trunk, 1024)
    tc = _tile(e, 256, 128)
    nb = e // tc
    cidx = _cond_index(trunk, tm)
    nh = t // HALO
    r = tm // HALO
    wspec = lambda part: pl.BlockSpec((d, tc), lambda i, j: (0, part * nb + j))
    cw = conv_w.reshape(3, 3, e).transpose(1, 0, 2)
    cb = conv_b.reshape(3, 1, e)
    return pl.pallas_call(
        functools.partial(_hy_inproj_kernel, seq_len),
        grid=(t // tm, nb),
        in_specs=[pl.BlockSpec((tm, d), lambda i, j: (i, 0)),
                  pl.BlockSpec((HALO, d), lambda i, j: (jnp.maximum(i * r - 1, 0), 0)),
                  pl.BlockSpec((HALO, d), lambda i, j: (jnp.minimum((i + 1) * r, nh - 1), 0)),
                  pl.BlockSpec((1, d), lambda i, j: (0, 0)),
                  pl.BlockSpec((1, 1, d), lambda i, j: (cidx(i), 0, 0)),
                  pl.BlockSpec((1, 1, d), lambda i, j: (cidx(i), 0, 1)),
                  wspec(0), wspec(1), wspec(2), wspec(3),
                  pl.BlockSpec((3, 3, tc), lambda i, j: (0, 0, j)),
                  pl.BlockSpec((3, 1, tc), lambda i, j: (0, 0, j))],
        out_specs=[pl.BlockSpec((tm, tc), lambda i, j: (i, j))] * 2,
        out_shape=[jax.ShapeDtypeStruct((t, e), BF16), jax.ShapeDtypeStruct((t, e), BF16)],
        scratch_shapes=[pltpu.VMEM((tm + 2 * HALO, d), BF16)],
        compiler_params=_cparams(("arbitrary", "arbitrary")),
        name="hy_in_proj_gate",
    )(x, x, x, norm_g.reshape(1, d), mod, mod, w, w, w, w, cw, cb)


def _filter_tables(seq_len):
    l = seq_len
    t = np.linspace(0.0, 1.0, l)
    w = 2.0 * math.pi * np.arange(l) / l
    f = np.linspace(1e-4, N_BANDS - 1, N_BANDS)
    z = np.concatenate([t[:, None], np.cos(f[None] * w[:, None]), -np.sin(f[None] * w[:, None])], axis=-1)
    pos = np.concatenate([np.arange(l), [0], np.arange(l - 1, 0, -1)])
    z2 = np.zeros((2 * l, 128), np.float32)
    z2[:, :z.shape[1]] = z[pos]
    aux = np.zeros((2 * l, 128), np.float32)
    aux[:, 0] = t[pos]
    aux[:, 1] = 1.0
    aux[l, 1] = 0.0
    return jnp.asarray(z2.T.copy()), jnp.asarray(aux)


def _filter_ffn_kernel(z_ref, w1, b1, w2, b2, w3, b3, fr, o_ref):
    f = fr[...]
    h = jnp.sin(f * (jnp.dot(w1[...], z_ref[...], precision=HI, preferred_element_type=F32) + b1[...]))
    h = jnp.sin(f * (jnp.dot(w2[...], h, precision=HI, preferred_element_type=F32) + b2[...]))
    h = jnp.sin(f * (jnp.dot(w3[...], h, precision=HI, preferred_element_type=F32) + b3[...]))
    o_ref[...] = h.astype(o_ref.dtype)


def _filter_expand_kernel(h_ref, aux_ref, w4, dec, o_ref):
    k = lax.dot_general(h_ref[...], w4[...].astype(BF16), (((0,), (0,)), ((), ())),
                        preferred_element_type=F32)
    t = aux_ref[:, 0:1]
    keep = aux_ref[:, 1:2]
    o_ref[...] = (k * jnp.exp(-t * jnp.abs(dec[0])) * keep).astype(o_ref.dtype)


def _filters(seq_len, w1, b1, w2, b2, w3, b3, w4, freq, decay):
    e = decay.shape[-1]
    fo = w2.shape[0]
    z2t, aux = _filter_tables(seq_len)
    w1t = jnp.zeros((fo, 128), F32).at[:, :w1.shape[0]].set(w1.T)
    tr = _tile(seq_len, 1024, 128)
    tc = _tile(e, 2048, 128)
    nb = e // tc
    nr = seq_len // tr
    small = lambda a: pl.BlockSpec(a.shape, lambda *_: (0,) * a.ndim)
    b1c, b2c, b3c, frc = (a.reshape(fo, 1) for a in (b1, b2, b3, freq))
    w2t, w3t = w2.T, w3.T
    h3 = pl.pallas_call(
        _filter_ffn_kernel,
        grid=(2 * nr,),
        in_specs=[pl.BlockSpec((128, tr), lambda i: (0, i)),
                  small(w1t), small(b1c), small(w2t), small(b2c), small(w3t), small(b3c), small(frc)],
        out_specs=pl.BlockSpec((fo, tr), lambda i: (0, i)),
        out_shape=jax.ShapeDtypeStruct((fo, 2 * seq_len), BF16),
        compiler_params=_cparams(("arbitrary",)),
        name="hy_filter_ffn",
    )(z2t, w1t, b1c, w2t, b2c, w3t, b3c, frc)
    return pl.pallas_call(
        _filter_expand_kernel,
        grid=(2 * nr, nb),
        in_specs=[pl.BlockSpec((fo, tr), lambda i, j: (0, i)),
                  pl.BlockSpec((tr, 128), lambda i, j: (i, 0)),
                  pl.BlockSpec((fo, tc), lambda i, j: (0, (i // nr) * nb + j)),
                  pl.BlockSpec((1, 1, tc), lambda i, j: (i // nr, 0, j))],
        out_specs=pl.BlockSpec((tr, tc), lambda i, j: (i, j)),
        out_shape=jax.ShapeDtypeStruct((2 * seq_len, e), BF16),
        compiler_params=_cparams(("arbitrary", "arbitrary")),
        name="hy_filter_expand",
    )(h3, aux, w4, decay.reshape(2, 1, e))


def _cmm_kernel(w_ref, x_ref, o_ref):
    o_ref[0] = jnp.dot(w_ref[...], x_ref[0], preferred_element_type=F32).astype(o_ref.dtype)


def _cmm(w, x, out_dtype):
    m, k = w.shape
    b, _, n = x.shape
    tn = _tile(n, 4096, 128)
    return pl.pallas_call(
        _cmm_kernel,
        grid=(b, n // tn),
        in_specs=[pl.BlockSpec((m, k), lambda bi, j: (0, 0)),
                  pl.BlockSpec((1, k, tn), lambda bi, j: (bi, 0, j))],
        out_specs=pl.BlockSpec((1, m, tn), lambda bi, j: (bi, 0, j)),
        out_shape=jax.ShapeDtypeStruct((b, m, n), out_dtype),
        compiler_params=_cparams(("arbitrary", "arbitrary")),
        name="const_lhs_matmul",
    )(w, x)


def _direct_mats(seq_len):
    l = seq_len
    n = 2 * l
    f = np.arange(l)[:, None]
    t = np.arange(n)[None, :]
    ang = 2.0 * math.pi * ((f * t) % n) / n
    fwd = np.concatenate([np.cos(ang), -np.sin(ang)], axis=0)
    fwd[l] = np.cos(math.pi * np.arange(n))
    tt = np.arange(l)[:, None]
    ff = np.arange(l)[None, :]
    ang2 = 2.0 * math.pi * ((tt * ff) % n) / n
    wgt = np.where(ff == 0, 1.0, 2.0) / n
    inv = np.concatenate([wgt * np.cos(ang2), -wgt * np.sin(ang2)], axis=1)
    inv[:, l] = np.cos(math.pi * np.arange(l)) / n
    return fwd, inv


def _direct_conv_kernel(seq_len, f_ref, v_ref, x_ref, k_ref, g0_ref, bias_ref, o_ref):
    l = seq_len
    x = x_ref[0]
    s = jnp.dot(f_ref[...], x, preferred_element_type=F32)
    sre, sim = s[:l], s[l:]
    kre, kim = k_ref[:l, :], k_ref[l:, :]
    row0 = lax.broadcasted_iota(jnp.int32, sre.shape, 0) == 0
    yre = sre * kre - jnp.where(row0, 0.0, sim * kim)
    yim = jnp.where(row0, sim * kim, sre * kim + sim * kre)
    y = jnp.concatenate([yre, yim], axis=0).astype(BF16)
    out = jnp.dot(v_ref[...], y, preferred_element_type=F32)
    out = out + x.astype(F32) * bias_ref[...]
    o_ref[0] = (out * g0_ref[0].astype(F32)).astype(o_ref.dtype)


def _longconv_direct(vx, g0, kfilt, bias, batch, seq_len):
    e = vx.shape[-1]
    l = seq_len
    fwd, inv = _direct_mats(l)
    f_full = jnp.asarray(fwd, BF16)
    f_data = jnp.asarray(fwd[:, :l], BF16)
    v_mat = jnp.asarray(inv, BF16)
    kspec = _cmm(f_full, kfilt.astype(BF16)[None], F32)[0]
    tc = _tile(e, 2048, 128)
    out = pl.pallas_call(
        functools.partial(_direct_conv_kernel, l),
        grid=(e // tc, batch),
        in_specs=[pl.BlockSpec((2 * l, l), lambda j, b: (0, 0)),
                  pl.BlockSpec((l, 2 * l), lambda j, b: (0, 0)),
                  pl.BlockSpec((1, l, tc), lambda j, b: (b, 0, j)),
                  pl.BlockSpec((2 * l, tc), lambda j, b: (0, j)),
                  pl.BlockSpec((1, l, tc), lambda j, b: (b, 0, j)),
                  pl.BlockSpec((1, tc), lambda j, b: (0, j))],
        out_specs=pl.BlockSpec((1, l, tc), lambda j, b: (b, 0, j)),
        out_shape=jax.ShapeDtypeStruct((batch, l, e), BF16),
        compiler_params=_cparams(("arbitrary", "arbitrary")),
        name="longconv_direct",
    )(f_data, v_mat, vx.reshape(batch, l, e), kspec, g0.reshape(batch, l, e), bias.reshape(1, e))
    return out.reshape(batch * l, e)


def _two_level_mats(seq_len):
    n = 2 * seq_len
    n2 = FFT_N2
    n1 = n // n2
    h2 = n2 // 2
    ns = 8 * ((h2 + 1 + 7) // 8)
    f2 = np.arange(h2 + 1)[:, None]
    t2 = np.arange(n2)[None, :]
    ang = 2.0 * math.pi * ((f2 * t2) % n2) / n2
    f1m = np.zeros((2 * ns, n2))
    f1m[0:2 * (h2 + 1):2] = np.cos(ang)
    f1m[1:2 * (h2 + 1):2] = -np.sin(ang)
    wgt = np.where((f2 == 0) | (f2 == h2), 1.0, 2.0) / n
    g1m = np.zeros((h2, 2 * ns))
    g1m[:, 0:2 * (h2 + 1):2] = (wgt * np.cos(ang[:, :h2])).T
    g1m[:, 1:2 * (h2 + 1):2] = -(wgt * np.sin(ang[:, :h2])).T
    t1 = np.arange(n1)[None, :]
    f1 = np.arange(n1)[:, None]
    rm = np.zeros((ns, 2 * n1, 2 * n1))
    pm = np.zeros((ns, 2 * n1, 2 * n1))
    for s in range(h2 + 1):
        a = 2.0 * math.pi * ((t1 * (n2 * f1 + s)) % n) / n
        mr, mi = np.cos(a), -np.sin(a)
        rm[s] = np.block([[mr, -mi], [mi, mr]])
        pr, pi = mr.T, -mi.T
        pm[s] = np.block([[pr, -pi], [pi, pr]])
    return n1, n2, ns, f1m, g1m, rm, pm


def _slab_spec_kernel(r_ref, a_ref, o_ref):
    n1 = a_ref.shape[3]
    a = jnp.concatenate([a_ref[0, 0, 0], a_ref[0, 0, 1]], axis=0)
    s = jnp.dot(r_ref[0], a, preferred_element_type=F32)
    o_ref[0, 0] = s[:n1].astype(o_ref.dtype)
    o_ref[0, 1] = s[n1:].astype(o_ref.dtype)


def _slab_conv_kernel(r_ref, p_ref, a_ref, k_ref, o_ref):
    n1 = a_ref.shape[3]
    kre, kim = k_ref[0, 0].astype(F32), k_ref[0, 1].astype(F32)
    for b in range(a_ref.shape[0]):
        a = jnp.concatenate([a_ref[b, 0, 0], a_ref[b, 0, 1]], axis=0)
        s = jnp.dot(r_ref[0], a, preferred_element_type=F32)
        sre, sim = s[:n1], s[n1:]
        y = jnp.concatenate([sre * kre - sim * kim, sre * kim + sim * kre], axis=0).astype(BF16)
        o = jnp.dot(p_ref[0], y, preferred_element_type=F32)
        o_ref[b, 0, 0] = o[:n1].astype(o_ref.dtype)
        o_ref[b, 0, 1] = o[n1:].astype(o_ref.dtype)


def _dft_l1_kernel(k_ref, x_ref, o_ref):
    x = x_ref[0].astype(F32)
    nt2, _, tc = x.shape
    halves = []
    for g in range(SUB_TILES):
        xg = x[:, SUBLANES * g:SUBLANES * (g + 1), :].reshape(nt2 * SUBLANES, tc).astype(BF16)
        a = jnp.dot(k_ref[...], xg, preferred_element_type=F32)
        halves.append(a.reshape(-1, SUBLANES, tc))
    o_ref[0] = jnp.concatenate(halves, axis=1).astype(o_ref.dtype)


def _dft_l1_inv_kernel(g_ref, a_ref, vx_ref, g0_ref, bias_ref, o_ref):
    a = a_ref[0].astype(F32)
    nr, _, tc = a.shape
    halves = []
    for g in range(SUB_TILES):
        ag = a[:, SUBLANES * g:SUBLANES * (g + 1), :].reshape(nr * SUBLANES, tc).astype(BF16)
        y = jnp.dot(g_ref[...], ag, preferred_element_type=F32)
        halves.append(y.reshape(-1, SUBLANES, tc))
    y = jnp.concatenate(halves, axis=1)
    y = y + vx_ref[0].astype(F32) * bias_ref[...]
    o_ref[0] = (y * g0_ref[0].astype(F32)).astype(o_ref.dtype)


def _dft_l1(kmat, x4, out_dtype):
    b, nt2, n1, e = x4.shape
    m = kmat.shape[0] // SUBLANES
    tc = _tile(e, 1024, 128)
    rows = SUBLANES * SUB_TILES
    return pl.pallas_call(
        _dft_l1_kernel,
        grid=(b, n1 // rows, e // tc),
        in_specs=[pl.BlockSpec(kmat.shape, lambda bi, i, j: (0, 0)),
                  pl.BlockSpec((1, nt2, rows, tc), lambda bi, i, j: (bi, 0, i, j))],
        out_specs=pl.BlockSpec((1, m, rows, tc), lambda bi, i, j: (bi, 0, i, j)),
        out_shape=jax.ShapeDtypeStruct((b, m, n1, e), out_dtype),
        compiler_params=_cparams(("arbitrary", "arbitrary", "arbitrary")),
        name="dft_level1",
    )(kmat, x4)


def _longconv_two_level(vx, g0, kfilt, bias, batch, seq_len):
    e = vx.shape[-1]
    l = seq_len
    n1, n2, ns, f1m, g1m, rm, pm = _two_level_mats(l)
    h2 = n2 // 2
    eye = np.eye(SUBLANES)
    k_full = jnp.asarray(np.kron(f1m, eye), BF16)
    k_data = jnp.asarray(np.kron(f1m[:, :h2], eye), BF16)
    g8 = jnp.asarray(np.kron(g1m, eye), BF16)
    rmat = jnp.asarray(rm, BF16)
    pmat = jnp.asarray(pm, BF16)
    tc = _tile(e, 4096, 128)
    nc = e // tc

    ka = _dft_l1(k_full, kfilt.reshape(1, n2, n1, e), BF16).reshape(1, ns, 2, n1, e)
    kspec = pl.pallas_call(
        _slab_spec_kernel,
        grid=(ns, nc),
        in_specs=[pl.BlockSpec((1, 2 * n1, 2 * n1), lambda s, j: (s, 0, 0)),
                  pl.BlockSpec((1, 1, 2, n1, tc), lambda s, j: (0, s, 0, 0, j))],
        out_specs=pl.BlockSpec((1, 2, n1, tc), lambda s, j: (s, 0, 0, j)),
        out_shape=jax.ShapeDtypeStruct((ns, 2, n1, e), BF16),
        compiler_params=_cparams(("arbitrary", "arbitrary")),
        name="filter_slab_dft",
    )(rmat, ka)

    vx4 = vx.reshape(batch, h2, n1, e)
    a = _dft_l1(k_data, vx4, BF16).reshape(batch, ns, 2, n1, e)
    a2 = pl.pallas_call(
        _slab_conv_kernel,
        grid=(ns, nc),
        in_specs=[pl.BlockSpec((1, 2 * n1, 2 * n1), lambda s, j: (s, 0, 0)),
                  pl.BlockSpec((1, 2 * n1, 2 * n1), lambda s, j: (s, 0, 0)),
                  pl.BlockSpec((batch, 1, 2, n1, tc), lambda s, j: (0, s, 0, 0, j)),
                  pl.BlockSpec((1, 2, n1, tc), lambda s, j: (s, 0, 0, j))],
        out_specs=pl.BlockSpec((batch, 1, 2, n1, tc), lambda s, j: (0, s, 0, 0, j)),
        out_shape=jax.ShapeDtypeStruct((batch, ns, 2, n1, e), BF16),
        compiler_params=_cparams(("arbitrary", "arbitrary")),
        name="slab_conv",
    )(rmat, pmat, a, kspec)
    a2 = a2.reshape(batch, 2 * ns, n1, e)
    rows = SUBLANES * SUB_TILES
    tci = _tile(e, 1024, 128)
    blk = lambda r: pl.BlockSpec((1, r, rows, tci), lambda bi, i, j: (bi, 0, i, j))
    out = pl.pallas_call(
        _dft_l1_inv_kernel,
        grid=(batch, n1 // rows, e // tci),
        in_specs=[pl.BlockSpec(g8.shape, lambda bi, i, j: (0, 0)),
                  blk(2 * ns), blk(h2), blk(h2),
                  pl.BlockSpec((1, 1, tci), lambda bi, i, j: (0, 0, j))],
        out_specs=blk(h2),
        out_shape=jax.ShapeDtypeStruct((batch, h2, n1, e), BF16),
        compiler_params=_cparams(("arbitrary", "arbitrary", "arbitrary")),
        name="dft_level1_inverse",
    )(g8, a2, vx4, g0.reshape(batch, h2, n1, e), bias.reshape(1, 1, e))
    return out.reshape(batch * l, e)


def _longconv(vx, g0, kfilt, bias, batch, seq_len):
    if seq_len <= DIRECT_CONV_MAX_L:
        return _longconv_direct(vx, g0, kfilt, bias, batch, seq_len)
    return _longconv_two_level(vx, g0, kfilt, bias, batch, seq_len)


def _ml_qkv_kernel(seq_len, n_heads, dk, x_ref, xp_ref, xn_ref, cw_ref, cb_ref, wq_ref, wk_ref, wv_ref,
                   wgq_ref, wgk_ref, wgv_ref, bg_ref, q_ref, k_ref, kt_ref, v_ref, xc_ref, g_ref, gt_ref, gacc):
    h = pl.program_id(1)
    tm = x_ref.shape[0]
    q_len = gt_ref.shape[2]
    x = x_ref[...].astype(F32)
    conv = _conv3(x, xp_ref[HALO - 1:HALO, :].astype(F32), xn_ref[0:1, :].astype(F32),
                  cw_ref[...], cb_ref[...], pl.program_id(0) * tm, seq_len)
    xc = _silu(conv)
    xc_ref[...] = xc.astype(xc_ref.dtype)
    xcb = xc.astype(BF16)
    q = jnp.dot(xcb, wq_ref[0], preferred_element_type=F32).astype(BF16)
    kf = jnp.dot(xcb, wk_ref[0], preferred_element_type=F32) * (dk ** -0.5)
    k = kf.astype(BF16)
    v = jnp.dot(x.astype(BF16), wv_ref[0], preferred_element_type=F32).astype(BF16)
    q_ref[...] = q
    k_ref[...] = k
    v_ref[...] = v
    for c in range(tm // q_len):
        kt_ref[c] = kf[c * q_len:(c + 1) * q_len, :].T.astype(BF16)
    part = (jnp.dot(q, wgq_ref[0], preferred_element_type=F32)
            + jnp.dot(k, wgk_ref[0], preferred_element_type=F32)
            + jnp.dot(v, wgv_ref[0], preferred_element_type=F32))

    @pl.when(h == 0)
    def _():
        gacc[...] = part + bg_ref[...]

    @pl.when(h > 0)
    def _():
        gacc[...] += part

    @pl.when(h == n_heads - 1)
    def _():
        g = gacc[...]
        col = lax.broadcasted_iota(jnp.int32, g.shape, 1)
        is_forget = (col % (2 * n_heads)) >= n_heads
        g = jnp.where(is_forget, jax.nn.log_sigmoid(g), g)
        col_q = lax.broadcasted_iota(jnp.int32, (q_len, g.shape[1]), 1)
        forget_q = (col_q % (2 * n_heads)) >= n_heads
        bwd_q = (col_q // (2 * n_heads)) == 1
        r_i = lax.broadcasted_iota(jnp.int32, (q_len, q_len), 0)
        c_i = lax.broadcasted_iota(jnp.int32, (q_len, q_len), 1)
        tril = (c_i <= r_i).astype(F32)
        for c in range(tm // q_len):
            gc = g[c * q_len:(c + 1) * q_len, :]
            pre = jnp.dot(tril, gc, precision=HI, preferred_element_type=F32)
            tot = jnp.sum(gc, axis=0, keepdims=True)
            suf = tot - pre + gc
            gc = jnp.where(forget_q, jnp.where(bwd_q, suf, pre), gc)
            g_ref[c * q_len:(c + 1) * q_len, :] = gc
            gt_ref[c] = gc.T


def _ml_qkv(p, conv_w, conv_b, wq, wk, wv, w_gate, b_gate, batch, seq_len, e):
    t = p.shape[0]
    n_heads, dh, dk = wq.shape
    dv = wv.shape[2]
    q_len = min(SCAN_CHUNK, seq_len)
    tm = _row_tile(batch, seq_len, 2048, q_len)
    ng = w_gate.shape[1]
    gpad = 128
    wgq = jnp.zeros((n_heads, dk, gpad), BF16).at[:, :, :ng].set(
        w_gate[:n_heads * dk].reshape(n_heads, dk, ng).astype(BF16))
    wgk = jnp.zeros((n_heads, dk, gpad), BF16).at[:, :, :ng].set(
        w_gate[n_heads * dk:2 * n_heads * dk].reshape(n_heads, dk, ng).astype(BF16))
    wgv = jnp.zeros((n_heads, dv, gpad), BF16).at[:, :, :ng].set(
        w_gate[2 * n_heads * dk:].reshape(n_heads, dv, ng).astype(BF16))
    bg = jnp.zeros((1, gpad), F32).at[0, :ng].set(b_gate)
    specs = _halo_specs(tm, dh, t, lambda j: j)
    specs += [pl.BlockSpec((3, dh), lambda i, j: (0, j)),
              pl.BlockSpec((1, dh), lambda i, j: (0, j)),
              pl.BlockSpec((1, dh, dk), lambda i, j: (j, 0, 0)),
              pl.BlockSpec((1, dh, dk), lambda i, j: (j, 0, 0)),
              pl.BlockSpec((1, dh, dv), lambda i, j: (j, 0, 0)),
              pl.BlockSpec((1, dk, gpad), lambda i, j: (j, 0, 0)),
              pl.BlockSpec((1, dk, gpad), lambda i, j: (j, 0, 0)),
              pl.BlockSpec((1, dv, gpad), lambda i, j: (j, 0, 0)),
              pl.BlockSpec((1, gpad), lambda i, j: (0, 0))]
    return pl.pallas_call(
        functools.partial(_ml_qkv_kernel, seq_len, n_heads, dk),
        grid=(t // tm, n_heads),
        in_specs=specs,
        out_specs=[pl.BlockSpec((tm, dk), lambda i, j: (i, j)),
                   pl.BlockSpec((tm, dk), lambda i, j: (i, j)),
                   pl.BlockSpec((tm // q_len, dk, q_len), lambda i, j: (i, j, 0)),
                   pl.BlockSpec((tm, dv), lambda i, j: (i, j)),
                   pl.BlockSpec((tm, dh), lambda i, j: (i, j)),
                   pl.BlockSpec((tm, gpad), lambda i, j: (i, 0)),
                   pl.BlockSpec((tm // q_len, gpad, q_len), lambda i, j: (i, 0, 0))],
        out_shape=[jax.ShapeDtypeStruct((t, n_heads * dk), BF16),
                   jax.ShapeDtypeStruct((t, n_heads * dk), BF16),
                   jax.ShapeDtypeStruct((t // q_len, n_heads * dk, q_len), BF16),
                   jax.ShapeDtypeStruct((t, n_heads * dv), BF16),
                   jax.ShapeDtypeStruct((t, e), BF16),
                   jax.ShapeDtypeStruct((t, gpad), F32),
                   jax.ShapeDtypeStruct((t // q_len, gpad, q_len), F32)],
        scratch_shapes=[pltpu.VMEM((tm, gpad), F32)],
        compiler_params=_cparams(("arbitrary", "arbitrary")),
        name="ml_qkv_gates",
    )(p, p, p, conv_w, conv_b.reshape(1, e), wq.astype(BF16), wk.astype(BF16), wv.astype(BF16),
      wgq, wgk, wgv, bg)


def _ml_scan_kernel(n_heads, has_init, *refs):
    if has_init:
        (q_ref, k_ref, kt_ref, v_ref, g_ref, gt_ref, c0_ref, n0_ref, m0_ref,
         h_ref, cout_ref, nout_ref, mout_ref, c_scr, n_scr, m_scr) = refs
    else:
        (q_ref, k_ref, kt_ref, v_ref, g_ref, gt_ref,
         h_ref, cout_ref, nout_ref, mout_ref, c_scr, n_scr, m_scr) = refs
    head = pl.program_id(1)
    seq_len = q_ref.shape[0]
    q_len = gt_ref.shape[2]
    n_chunks = seq_len // q_len

    if has_init:
        for d in range(2):
            c_scr[d] = c0_ref[0, 0, d, 0]
            n_scr[d] = n0_ref[0, 0, d]
            m_scr[d] = m0_ref[0, 0, d]
    h_ref[...] = jnp.zeros_like(h_ref)

    row_i = lax.broadcasted_iota(jnp.int32, (q_len, q_len), 0)
    col_i = lax.broadcasted_iota(jnp.int32, (q_len, q_len), 1)
    lane_g = lax.broadcasted_iota(jnp.int32, (q_len, g_ref.shape[1]), 1)

    def direction(d, c, from_zero=False):
        icol = d * 2 * n_heads + head
        fcol = icol + n_heads
        start = c * q_len if isinstance(c, int) else pl.multiple_of(c * q_len, q_len)
        rows = pl.ds(start, q_len)
        g = g_ref[rows, :]
        i_col = jnp.sum(jnp.where(lane_g == icol, g, 0.0), axis=1, keepdims=True)
        b_col = jnp.sum(jnp.where(lane_g == fcol, g, 0.0), axis=1, keepdims=True)
        i_row = gt_ref[c, pl.ds(icol, 1), :]
        b_row = gt_ref[c, pl.ds(fcol, 1), :]
        mask = (col_i <= row_i) if d == 0 else (col_i >= row_i)
        last = q_len - 1 if d == 0 else 0
        g_tot = jnp.sum(jnp.where(col_i[0:1, :] == last, b_row, 0.0), axis=1, keepdims=True)
        m_prev = 0.0 if from_zero else m_scr[d][:, 0:1]
        if not from_zero:
            n_prev = n_scr[d]
            c_prev = c_scr[d]

        qc = q_ref[rows, :]
        kc = k_ref[rows, :]
        vc = v_ref[rows, :]
        dmat = jnp.where(mask, b_col - b_row + i_row, -jnp.inf)
        inter = b_col + m_prev
        m_t = jnp.maximum(inter, jnp.max(dmat, axis=1, keepdims=True))
        qk = lax.dot_general(qc, kc, (((1,), (1,)), ((), ())), preferred_element_type=F32)
        s = qk * jnp.exp(dmat - m_t)
        if from_zero:
            num = jnp.dot(s.astype(BF16), vc, preferred_element_type=F32)
            qn = jnp.sum(s, axis=1, keepdims=True)
        else:
            w_inter = jnp.exp(inter - m_t)
            num = (w_inter * jnp.dot(qc, c_prev.astype(BF16), preferred_element_type=F32)
                   + jnp.dot(s.astype(BF16), vc, preferred_element_type=F32))
            qn = (w_inter * jnp.sum(qc.astype(F32) * n_prev, axis=1, keepdims=True)
                  + jnp.sum(s, axis=1, keepdims=True))
        den = jnp.maximum(jnp.abs(qn), jnp.exp(-m_t))
        h_ref[rows, :] += num / den

        a_col = g_tot - b_col + i_col
        a_row = g_tot - b_row + i_row
        m_new = jnp.maximum(g_tot + m_prev, jnp.max(a_row, axis=1, keepdims=True))
        kwt = (kt_ref[c].astype(F32) * jnp.exp(a_row - m_new)).astype(BF16)
        kv = jnp.dot(kwt, vc, preferred_element_type=F32)
        ksum = jnp.sum(kc.astype(F32) * jnp.exp(a_col - m_new), axis=0, keepdims=True)
        if from_zero:
            c_scr[d] = kv
            n_scr[d] = ksum
        else:
            dec = jnp.exp(g_tot + m_prev - m_new)
            c_scr[d] = dec * c_prev + kv
            n_scr[d] = dec * n_prev + ksum
        m_scr[d] = jnp.broadcast_to(m_new, m_scr.shape[1:])

    def body(j, carry):
        direction(0, j)
        direction(1, n_chunks - 1 - j)
        return carry

    first = 0
    if not has_init:
        direction(0, 0, from_zero=True)
        direction(1, n_chunks - 1, from_zero=True)
        first = 1
    lax.fori_loop(first, n_chunks, body, 0)
    for d in range(2):
        cout_ref[0, 0, d, 0] = c_scr[d]
        nout_ref[0, 0, d] = n_scr[d]
        mout_ref[0, 0, d] = m_scr[d]


def _ml_scan(q, k, kt, v, g, gt, batch, seq_len, n_heads, init=None):
    dk = q.shape[1] // n_heads
    dv = v.shape[1] // n_heads
    q_len = gt.shape[2]
    gpad = g.shape[1]
    nq = seq_len // q_len
    in_specs = [pl.BlockSpec((seq_len, dk), lambda b, h: (b, h)),
                pl.BlockSpec((seq_len, dk), lambda b, h: (b, h)),
                pl.BlockSpec((nq, dk, q_len), lambda b, h: (b, h, 0)),
                pl.BlockSpec((seq_len, dv), lambda b, h: (b, h)),
                pl.BlockSpec((seq_len, gpad), lambda b, h: (b, 0)),
                pl.BlockSpec((nq, gpad, q_len), lambda b, h: (b, 0, 0))]
    args = [q, k, kt, v, g, gt]
    if init is not None:
        c0, n0, m0 = init
        in_specs += [pl.BlockSpec((1, 1, 2, 1, dk, dv), lambda b, h: (b, 0, 0, h, 0, 0)),
                     pl.BlockSpec((1, 1, 2, 1, dk), lambda b, h: (b, h, 0, 0, 0)),
                     pl.BlockSpec((1, 1, 2, 1, 128), lambda b, h: (b, h, 0, 0, 0))]
        args += [c0, n0, m0]
    out = pl.pallas_call(
        functools.partial(_ml_scan_kernel, n_heads, init is not None),
        grid=(batch, n_heads),
        in_specs=in_specs,
        out_specs=[pl.BlockSpec((seq_len, dv), lambda b, h: (b, h)),
                   pl.BlockSpec((1, 1, 2, 1, dk, dv), lambda b, h: (b, 0, 0, h, 0, 0)),
                   pl.BlockSpec((1, 1, 2, 1, dk), lambda b, h: (b, h, 0, 0, 0)),
                   pl.BlockSpec((1, 1, 2, 1, 128), lambda b, h: (b, h, 0, 0, 0))],
        out_shape=[jax.ShapeDtypeStruct((batch * seq_len, n_heads * dv), F32),
                   jax.ShapeDtypeStruct((batch, 1, 2, n_heads, dk, dv), F32),
                   jax.ShapeDtypeStruct((batch, n_heads, 2, 1, dk), F32),
                   jax.ShapeDtypeStruct((batch, n_heads, 2, 1, 128), F32)],
        scratch_shapes=[pltpu.VMEM((2, dk, dv), F32), pltpu.VMEM((2, 1, dk), F32),
                        pltpu.VMEM((2, 1, 128), F32)],
        compiler_params=_cparams(("arbitrary", "arbitrary")),
        name="ml_scan",
    )(*args)
    return out


def _ml_out_kernel(h_ref, o_ref_in, xc_ref, z_ref, nw_ref, sk_ref, w_ref, x_ref, g_ref, gate_ref, out_ref, acc):
    k = pl.program_id(1)

    @pl.when(k == 0)
    def _():
        acc[...] = jnp.zeros_like(acc)

    dv = w_ref.shape[0] // HEADS_PER_STEP
    ys = []
    for hd in range(HEADS_PER_STEP):
        cols = slice(hd * dv, (hd + 1) * dv)
        hh = jax.nn.sigmoid(o_ref_in[:, cols].astype(F32)) * h_ref[:, cols]
        mu = jnp.mean(hh, axis=-1, keepdims=True)
        var = jnp.mean(jnp.square(hh - mu), axis=-1, keepdims=True)
        hn = (hh - mu) * lax.rsqrt(var + EPS) * nw_ref[:, cols]
        hn = hn + sk_ref[:, cols] * xc_ref[:, cols].astype(F32)
        ys.append((hn * _silu(z_ref[:, cols].astype(F32))).astype(BF16))
    y = jnp.concatenate(ys, axis=1)
    acc[...] += jnp.dot(y, w_ref[...].astype(BF16), preferred_element_type=F32)

    @pl.when(k == pl.num_programs(1) - 1)
    def _():
        _outproj_epilogue(acc[...], x_ref, g_ref, gate_ref, out_ref)


def _ml_out(h, p, xc, norm_w, skip, w, x, norm_g, mod, trunk, n_heads):
    t, e = h.shape
    d = w.shape[1]
    assert n_heads % HEADS_PER_STEP == 0
    nk = n_heads // HEADS_PER_STEP
    dv = HEADS_PER_STEP * (e // n_heads)
    tm = _cond_row_tile(trunk, 512)
    cidx = _cond_index(trunk, tm)
    return pl.pallas_call(
        _ml_out_kernel,
        grid=(t // tm, nk),
        in_specs=[pl.BlockSpec((tm, dv), lambda i, k: (i, k)),
                  pl.BlockSpec((tm, dv), lambda i, k: (i, 2 * nk + k)),
                  pl.BlockSpec((tm, dv), lambda i, k: (i, k)),
                  pl.BlockSpec((tm, dv), lambda i, k: (i, nk + k)),
                  pl.BlockSpec((1, dv), lambda i, k: (0, k)),
                  pl.BlockSpec((1, dv), lambda i, k: (0, k)),
                  pl.BlockSpec((dv, d), lambda i, k: (k, 0)),
                  pl.BlockSpec((tm, d), lambda i, k: (i, 0)),
                  pl.BlockSpec((1, d), lambda i, k: (0, 0)),
                  pl.BlockSpec((1, 1, d), lambda i, k: (cidx(i), 0, 2))],
        out_specs=pl.BlockSpec((tm, d), lambda i, k: (i, 0)),
        out_shape=jax.ShapeDtypeStruct((t, d), F32),
        scratch_shapes=[pltpu.VMEM((tm, d), F32)],
        compiler_params=_cparams(("arbitrary", "arbitrary")),
        name="ml_out_proj",
    )(h, p, xc, p, norm_w.reshape(1, e), skip.reshape(1, e), w, x, norm_g.reshape(1, d), mod)


def _hyena_layer(x, mod, trunk, norm_pre, norm_post, hw):
    (w_in, conv_w, conv_b, w1, b1, w2, b2, w3, b3, w4, freq, decay, bias, w_out) = hw
    batch, seq_len = trunk[0], trunk[1]
    e = w_out.shape[0]
    vx, g0 = _hy_inproj(x, norm_pre, mod, trunk, w_in, conv_w, conv_b, e)
    kfilt = _filters(seq_len, w1, b1, w2, b2, w3, b3, w4, freq, decay)
    y = _longconv(vx, g0, kfilt, bias, batch, seq_len)
    return _outproj(y, w_out.astype(BF16), x, norm_post, mod, trunk)


def _mlstm_layer(x, mod, trunk, norm_pre, norm_post, mw, init):
    (w_in, conv_w, conv_b, wq, wk, wv, w_gate, b_gate, norm_w, skip, w_out) = mw
    batch, seq_len = trunk[0], trunk[1]
    e = w_out.shape[0]
    n_heads = wq.shape[0]
    p = _inproj(x, norm_pre, mod, trunk, w_in, BF16)
    q, k, kt, v, xc, g, gt = _ml_qkv(p, conv_w, conv_b, wq, wk, wv, w_gate, b_gate, batch, seq_len, e)
    h, c_new, n_new, m_new = _ml_scan(q, k, kt, v, g, gt, batch, seq_len, n_heads, init)
    x_new = _ml_out(h, p, xc, norm_w, skip, w_out.astype(BF16), x, norm_post, mod, trunk, n_heads)
    return x_new, c_new, n_new, m_new


def kernel(x_prompt, x_sample, state_C, state_n, state_m, c, c_ctx, norm_pre, norm_post, ada_w, ada_b,
           hy_w_in, hy_conv_w, hy_conv_b, hy_ffn_w1, hy_ffn_b1, hy_ffn_w2, hy_ffn_b2, hy_ffn_w3, hy_ffn_b3,
           hy_ffn_w4, hy_sin_freq, hy_decay, hy_bias, hy_w_out, ml_w_in, ml_conv_w, ml_conv_b, ml_wq, ml_wk,
           ml_wv, ml_w_gate, ml_b_gate, ml_norm, ml_skip, ml_w_out):
    depth = norm_pre.shape[0]
    bp, lp, d = x_prompt.shape
    bs, ls, _ = x_sample.shape
    n_heads = ml_wq.shape[1]

    nrow = 8 * ((1 + bs + 7) // 8)
    conds = jnp.zeros((nrow, d), F32).at[0].set(c_ctx).at[1:1 + bs].set(c)
    mods = _ada(conds, ada_w, ada_b)

    def run_trunk(x3, cond0, per_seq, init_states):
        batch, seq_len, _ = x3.shape
        trunk = (batch, seq_len, cond0, per_seq)
        x = x3.reshape(batch * seq_len, d)
        new_c, new_n, new_m = [], [], []
        for i in range(depth):
            mod = mods[i].reshape(nrow, 1, 3 * d)
            j = i // 2
            if i % 2 == 0:
                hw = (hy_w_in[j], hy_conv_w[j], hy_conv_b[j], hy_ffn_w1[j], hy_ffn_b1[j], hy_ffn_w2[j],
                      hy_ffn_b2[j], hy_ffn_w3[j], hy_ffn_b3[j], hy_ffn_w4[j], hy_sin_freq[j], hy_decay[j],
                      hy_bias[j], hy_w_out[j])
                x = _hyena_layer(x, mod, trunk, norm_pre[i], norm_post[i], hw)
            else:
                mw = (ml_w_in[j], ml_conv_w[j], ml_conv_b[j], ml_wq[j], ml_wk[j], ml_wv[j], ml_w_gate[j],
                      ml_b_gate[j], ml_norm[j], ml_skip[j], ml_w_out[j])
                init = None if init_states is None else init_states(j)
                x, cj, nj, mj = _mlstm_layer(x, mod, trunk, norm_pre[i], norm_post[i], mw, init)
                new_c.append(cj)
                new_n.append(nj)
                new_m.append(mj)
        return x.reshape(batch, seq_len, d), new_c, new_n, new_m

    y_prompt, cs, ns, ms = run_trunk(x_prompt, 0, False, None)
    dk = ml_wq.shape[3]
    new_state_c = jnp.concatenate(cs, axis=1)
    new_state_n = jnp.stack([jnp.swapaxes(n[:, :, :, 0, :], 1, 2) for n in ns], axis=1)
    new_state_m = jnp.stack([jnp.swapaxes(m[:, :, :, 0, 0], 1, 2) for m in ms], axis=1)

    def lat_init(j):
        c0 = state_C[:, j:j + 1]
        n0 = jnp.swapaxes(state_n[:, j], 1, 2)[:, :, :, None, :]
        m0 = jnp.broadcast_to(jnp.swapaxes(state_m[:, j], 1, 2)[:, :, :, None, None],
                              (bs, n_heads, 2, 1, 128))
        return c0, n0, m0

    y_sample, _, _, _ = run_trunk(x_sample, 1, True, lat_init)
    return (y_prompt, y_sample, new_state_c, new_state_n, new_state_m)
```
